```python
import math
import jax, jax.numpy as jnp
from jax import lax
import numpy as np


D_MODEL = 1024
BATCH = 16
SEQ = 4096
DEPTH = 1

SSD_EXPAND = 2
D_INNER = SSD_EXPAND * D_MODEL
SSD_HEAD_DIM = 64
SSD_HEADS = D_INNER // SSD_HEAD_DIM
SSD_GROUPS = 8
D_STATE = 128
SSD_CONV = 4
SSD_CHUNK = 128
SSD_CONV_DIM = D_INNER + 2 * SSD_GROUPS * D_STATE
SSD_NORM_GROUP = D_INNER // SSD_GROUPS
ATTN_WINDOWS = (128, 512, 2048)
ATTN_DILATIONS = (1, 4, 16)
ATTN_N_GROUPS = 3
ATTN_HEADS_PER_GROUP = 8
ATTN_HEAD_DIM = 64
ATTN_BLOCK = 128
ATTN_WIDTH = ATTN_N_GROUPS * ATTN_HEADS_PER_GROUP * ATTN_HEAD_DIM
ATTN_OUT = ATTN_HEADS_PER_GROUP * ATTN_HEAD_DIM
D_FF = 2816
FFN_CONV = 3
EPS = 1e-6
IN_WIDTHS = (D_INNER, SSD_CONV_DIM, SSD_HEADS, ATTN_WIDTH, ATTN_WIDTH, ATTN_WIDTH, D_MODEL, D_MODEL)
IN_SPLITS = tuple(sum(IN_WIDTHS[:i + 1]) for i in range(len(IN_WIDTHS) - 1))
D_IN_PROJ = sum(IN_WIDTHS)

kernel_name = 'hybrid_ssd_dilated_attn_block'


def rms_norm(x, g):
    xf = x.astype(jnp.float32)
    y = xf * lax.rsqrt(jnp.mean(xf * xf, axis=-1, keepdims=True) + EPS)
    return (y * g.astype(jnp.float32)).astype(x.dtype)


def causal_dwconv(x, w, bias):
    K = w.shape[0]
    s = x.shape[1]
    xp = jnp.pad(x, ((0, 0), (K - 1, 0), (0, 0)))
    out = bias
    for i in range(K):
        out = out + xp[:, i:i + s] * w[i]
    return out


def ssd_chunked(xs, dt, A, Bm, Cm):
    b, s, H, P = xs.shape
    G, N = Bm.shape[2], Bm.shape[3]
    K = H // G
    Q = SSD_CHUNK
    c = s // Q
    X = (xs * dt[..., None]).reshape(b, c, Q, G, K, P)
    a = (dt * A).reshape(b, c, Q, G, K).transpose(0, 1, 3, 4, 2)
    a_cs = jnp.cumsum(a, axis=-1)
    Bc = Bm.reshape(b, c, Q, G, N)
    Cc = Cm.reshape(b, c, Q, G, N)
    causal = jnp.tril(jnp.ones((Q, Q), dtype=bool))
    seg = a_cs[..., :, None] - a_cs[..., None, :]
    Ldec = jnp.exp(jnp.where(causal, seg, -jnp.inf))
    CB = jnp.einsum('bclgn,bcsgn->bcgls', Cc, Bc)
    y_diag = jnp.einsum('bcgkls,bcsgkp->bclgkp', CB[:, :, :, None] * Ldec, X)
    decay_states = jnp.exp(a_cs[..., -1:] - a_cs)
    states = jnp.einsum('bclgn,bcgkl,bclgkp->bcgkpn', Bc, decay_states, X)
    chunk_decay = jnp.exp(a_cs[..., -1])

    def step(carry, inp):
        st, dec = inp
        return carry * dec[..., None, None] + st, carry

    init = jnp.zeros((b, G, K, P, N), jnp.float32)
    _, prev = lax.scan(step, init, (jnp.moveaxis(states, 1, 0), jnp.moveaxis(chunk_decay, 1, 0)))
    prev = jnp.moveaxis(prev, 0, 1)
    y_off = jnp.einsum('bclgn,bcgkpn,bcgkl->bclgkp', Cc, prev, jnp.exp(a_cs))
    return (y_diag + y_off).reshape(b, s, H, P)


def ssd_branch(z, xbc, dt_raw, conv_w, conv_b, dt_bias, a_log, d_skip, norm_g):
    b, s, _ = z.shape
    f32 = jnp.float32
    xbc = jax.nn.silu(causal_dwconv(xbc, conv_w, conv_b))
    xs, Bm, Cm = jnp.split(xbc, (D_INNER, D_INNER + SSD_GROUPS * D_STATE), axis=-1)
    xs = xs.reshape(b, s, SSD_HEADS, SSD_HEAD_DIM).astype(f32)
    Bm = Bm.reshape(b, s, SSD_GROUPS, D_STATE).astype(f32)
    Cm = Cm.reshape(b, s, SSD_GROUPS, D_STATE).astype(f32)
    dt = jax.nn.softplus(dt_raw.astype(f32) + dt_bias.astype(f32))
    A = -jnp.exp(a_log.astype(f32))
    y = ssd_chunked(xs, dt, A, Bm, Cm) + d_skip.astype(f32)[:, None] * xs
    y = y.reshape(b, s, D_INNER) * jax.nn.silu(z.astype(f32))
    y = rms_norm(y.reshape(b, s, SSD_GROUPS, SSD_NORM_GROUP), norm_g.reshape(SSD_GROUPS, SSD_NORM_GROUP))
    return y.reshape(b, s, D_INNER).astype(z.dtype)


def dilated_window_attention(q, k, v, dil, n_back):
    b, s, h, hd = q.shape
    L = s // dil
    nb = -(-L // ATTN_BLOCK)
    Lp = nb * ATTN_BLOCK

    def to_sub(t):
        t = t.reshape(b, L, dil, h, hd).transpose(0, 2, 1, 3, 4).reshape(b * dil, L, h, hd)
        t = jnp.pad(t, ((0, 0), (0, Lp - L), (0, 0), (0, 0)))
        return t.reshape(b * dil, nb, ATTN_BLOCK, h, hd)

    def with_prev(t):
        prev = jnp.pad(t, ((0, 0), (1, 0), (0, 0), (0, 0), (0, 0)))[:, :-1]
        return jnp.concatenate([prev, t], axis=2)

    qb = to_sub(q)
    kc = with_prev(to_sub(k))
    vc = with_prev(to_sub(v))
    scores = jnp.einsum('znqhd,znkhd->znhqk', qb, kc) * (hd ** -0.5)
    qi = jnp.arange(ATTN_BLOCK)[:, None]
    ki = jnp.arange(2 * ATTN_BLOCK)[None, :]
    dist = ATTN_BLOCK + qi - ki
    band = (dist >= 0) & (dist <= n_back)
    key_pos = (jnp.arange(nb)[:, None, None] - 1) * ATTN_BLOCK + ki[None]
    mask = band[None] & (key_pos >= 0)
    scores = jnp.where(mask[None, :, None], scores, -jnp.inf)
    m = jnp.max(scores, axis=-1, keepdims=True)
    p = jnp.exp(scores - m)
    den = jnp.sum(p, axis=-1, keepdims=True)
    o = jnp.einsum('znhqk,znkhd->znqhd', p / den, vc)
    lse = (m + jnp.log(den))[..., 0]
    o = o.reshape(b, dil, Lp, h, hd)[:, :, :L].transpose(0, 2, 1, 3, 4).reshape(b, s, h, hd)
    lse = lse.transpose(0, 1, 3, 2).reshape(b, dil, Lp, h)[:, :, :L].transpose(0, 2, 1, 3).reshape(b, s, h)
    return o, lse


def attn_branch(q, k, v, q_norm_g, k_norm_g):
    b, s, _ = q.shape
    f32 = jnp.float32
    shp = (b, s, ATTN_N_GROUPS, ATTN_HEADS_PER_GROUP, ATTN_HEAD_DIM)
    qn = rms_norm(q.reshape(shp), q_norm_g).astype(f32)
    kn = rms_norm(k.reshape(shp), k_norm_g).astype(f32)
    vv = v.reshape(shp).astype(f32)
    outs, lses = [], []
    for gi in range(ATTN_N_GROUPS):
        o, lse = dilated_window_attention(qn[:, :, gi], kn[:, :, gi], vv[:, :, gi],
                                          ATTN_DILATIONS[gi], ATTN_WINDOWS[gi] // ATTN_DILATIONS[gi])
        outs.append(o)
        lses.append(lse)
    wts = jax.nn.softmax(jnp.stack(lses, axis=0), axis=0)
    o = jnp.einsum('gbsh,gbshd->bshd', wts, jnp.stack(outs, axis=0))
    return o.reshape(b, s, ATTN_OUT).astype(q.dtype)


def setup_inputs(seed: int = 0) -> dict:
    key = jax.random.key(seed)
    ks = jax.random.split(key, 20)
    f32 = jnp.float32

    def dense(k, shape, fan_in):
        return jax.random.normal(k, shape, f32) * fan_in ** -0.5

    def gain(k, shape):
        return 1.0 + 0.05 * jax.random.normal(k, shape, f32)

    x = jax.random.normal(ks[0], (BATCH, SEQ, D_MODEL), f32)
    norm1_g = gain(ks[1], (DEPTH, D_MODEL))
    w_in = dense(ks[2], (DEPTH, D_MODEL, D_IN_PROJ), D_MODEL)
    ssd_conv_w = dense(ks[3], (DEPTH, SSD_CONV, SSD_CONV_DIM), SSD_CONV)
    ssd_conv_b = 0.02 * jax.random.normal(ks[4], (DEPTH, SSD_CONV_DIM), f32)
    dt0 = jnp.exp(jax.random.uniform(ks[5], (DEPTH, SSD_HEADS), f32, math.log(1e-3), math.log(1e-1)))
    dt_bias = dt0 + jnp.log(-jnp.expm1(-dt0))
    a_log = jnp.log(jax.random.uniform(ks[6], (DEPTH, SSD_HEADS), f32, 1.0, 16.0))
    d_skip = 1.0 + 0.1 * jax.random.normal(ks[7], (DEPTH, SSD_HEADS), f32)
    ssd_norm_g = gain(ks[8], (DEPTH, D_INNER))
    w_ssd_proj = dense(ks[9], (DEPTH, D_INNER, D_MODEL), D_INNER)
    q_norm_g = gain(ks[10], (DEPTH, ATTN_HEAD_DIM))
    k_norm_g = gain(ks[11], (DEPTH, ATTN_HEAD_DIM))
    w_attn_proj = dense(ks[12], (DEPTH, ATTN_OUT, D_MODEL), ATTN_OUT)
    w_out = dense(ks[13], (DEPTH, D_MODEL, D_MODEL), D_MODEL)
    norm2_g = gain(ks[14], (DEPTH, D_MODEL))
    w_up = dense(ks[15], (DEPTH, D_MODEL, 2 * D_FF), D_MODEL)
    ffn_conv_w = dense(ks[16], (DEPTH, FFN_CONV, 2 * D_FF), FFN_CONV)
    ffn_conv_b = 0.02 * jax.random.normal(ks[17], (DEPTH, 2 * D_FF), f32)
    w_down = dense(ks[18], (DEPTH, D_FF, D_MODEL), D_FF)
    return {'x': x, 'norm1_g': norm1_g, 'w_in': w_in, 'ssd_conv_w': ssd_conv_w, 'ssd_conv_b': ssd_conv_b,
            'dt_bias': dt_bias, 'a_log': a_log, 'd_skip': d_skip, 'ssd_norm_g': ssd_norm_g,
            'w_ssd_proj': w_ssd_proj, 'q_norm_g': q_norm_g, 'k_norm_g': k_norm_g,
            'w_attn_proj': w_attn_proj, 'w_out': w_out, 'norm2_g': norm2_g, 'w_up': w_up,
            'ffn_conv_w': ffn_conv_w, 'ffn_conv_b': ffn_conv_b, 'w_down': w_down}


def reference(x, norm1_g, w_in, ssd_conv_w, ssd_conv_b, dt_bias, a_log, d_skip, ssd_norm_g,
              w_ssd_proj, q_norm_g, k_norm_g, w_attn_proj, w_out, norm2_g, w_up,
              ffn_conv_w, ffn_conv_b, w_down):
    for l in range(DEPTH):
        h = rms_norm(x, norm1_g[l])
        proj = h @ w_in[l]
        z, xbc, dt_raw, q, k, v, g_ssd, g_attn = jnp.split(proj, IN_SPLITS, axis=-1)
        y_ssd = ssd_branch(z, xbc, dt_raw, ssd_conv_w[l], ssd_conv_b[l], dt_bias[l], a_log[l],
                           d_skip[l], ssd_norm_g[l])
        y_attn = attn_branch(q, k, v, q_norm_g[l], k_norm_g[l])
        merged = (jax.nn.sigmoid(g_ssd) * (y_ssd @ w_ssd_proj[l])
                  + jax.nn.sigmoid(g_attn) * (y_attn @ w_attn_proj[l]))
        x = x + merged @ w_out[l]
        h2 = rms_norm(x, norm2_g[l])
        u = causal_dwconv(h2 @ w_up[l], ffn_conv_w[l], ffn_conv_b[l])
        u_gate, u_val = jnp.split(u, 2, axis=-1)
        x = x + (jax.nn.silu(u_gate) * u_val) @ w_down[l]
    return x
```

```python
import functools

import jax
import jax.numpy as jnp
from jax import lax
from jax.experimental import pallas as pl
from jax.experimental.pallas import tpu as pltpu

F32 = jnp.float32
BF16 = jnp.bfloat16

D_MODEL = 1024
D_INNER = 2048
SSD_HEADS = 32
SSD_HEAD_DIM = 64
SSD_GROUPS = 8
SSD_HEADS_PER_GROUP = SSD_HEADS // SSD_GROUPS
D_STATE = 128
SSD_CONV = 4
SSD_CHUNK = 128
SSD_CONV_DIM = D_INNER + 2 * SSD_GROUPS * D_STATE
SSD_GROUP_WIDTH = D_INNER // SSD_GROUPS
ATTN_WINDOWS = (128, 512, 2048)
ATTN_DILATIONS = (1, 4, 16)
ATTN_HEADS = 8
ATTN_HEAD_DIM = 64
ATTN_BLOCK = 128
ATTN_GROUP_WIDTH = ATTN_HEADS * ATTN_HEAD_DIM
ATTN_WIDTH = 3 * ATTN_GROUP_WIDTH
D_FF = 2816
FFN_CONV = 3
EPS = 1e-6
LANES = 128
HALO = 8

COL_XBC = 0
COL_Z = COL_XBC + SSD_CONV_DIM
COL_GSSD = COL_Z + D_INNER
COL_GATTN = COL_GSSD + D_MODEL
COL_Q = COL_GATTN + D_MODEL
COL_K = COL_Q + ATTN_WIDTH
COL_V = COL_K + ATTN_WIDTH
PROJ_WIDTH = COL_V + ATTN_WIDTH

VMEM_LIMIT = 56 * 1024 * 1024


def _silu(v):
    return v * jax.nn.sigmoid(v)


def _in_proj_kernel(x_ref, g_ref, w_ref, wdt_ref, proj_ref, dt_ref, h_scr):
    @pl.when(pl.program_id(1) == 0)
    def _():
        x = x_ref[...]
        ms = jnp.mean(x * x, axis=-1, keepdims=True)
        h = (x * lax.rsqrt(ms + EPS) * g_ref[...]).astype(BF16)
        h_scr[...] = h
        dt_ref[...] = jnp.dot(h, wdt_ref[...], preferred_element_type=F32)

    proj_ref[...] = jnp.dot(h_scr[...], w_ref[...], preferred_element_type=F32).astype(BF16)


def _in_proj(x2, g1, w_main, w_dt, tm, tn):
    m = x2.shape[0]
    return pl.pallas_call(
        _in_proj_kernel,
        grid=(m // tm, PROJ_WIDTH // tn),
        in_specs=[
            pl.BlockSpec((tm, D_MODEL), lambda i, j: (i, 0)),
            pl.BlockSpec((1, D_MODEL), lambda i, j: (0, 0)),
            pl.BlockSpec((D_MODEL, tn), lambda i, j: (0, j)),
            pl.BlockSpec((D_MODEL, LANES), lambda i, j: (0, 0)),
        ],
        out_specs=[
            pl.BlockSpec((tm, tn), lambda i, j: (i, j)),
            pl.BlockSpec((tm, LANES), lambda i, j: (i, 0)),
        ],
        out_shape=[
            jax.ShapeDtypeStruct((m, PROJ_WIDTH), BF16),
            jax.ShapeDtypeStruct((m, LANES), F32),
        ],
        scratch_shapes=[pltpu.VMEM((tm, D_MODEL), BF16)],
        compiler_params=pltpu.CompilerParams(
            dimension_semantics=("arbitrary", "arbitrary"), vmem_limit_bytes=VMEM_LIMIT),
        name="in_proj",
    )(x2, g1, w_main, w_dt)


def _ssd_kernel(xbc_ref, z_ref, dt_ref, cw_ref, cb_ref, dtb_ref, alog_ref, dskip_ref, ng_ref,
                tri_ref, expand_ref, y_ref, xp_scr, act_scr, state_scr):
    q = SSD_CHUNK

    @pl.when(pl.program_id(1) == 0)
    def _():
        xp_scr[0:HALO, :] = jnp.zeros((HALO, SSD_CONV_DIM), F32)
        state_scr[...] = jnp.zeros(state_scr.shape, F32)

    xp_scr[HALO:HALO + q, :] = xbc_ref[...].astype(F32)
    acc = jnp.broadcast_to(cb_ref[...], (q, SSD_CONV_DIM))
    for i in range(SSD_CONV):
        acc = acc + xp_scr[pl.ds(HALO - (SSD_CONV - 1) + i, q), :] * cw_ref[i:i + 1, :]
    act_scr[...] = _silu(acc)
    xp_scr[0:HALO, :] = xp_scr[q:q + HALO, :]

    dt = jax.nn.softplus(dt_ref[...] + dtb_ref[...])
    a = dt * (-jnp.exp(alog_ref[...]))
    a_cs = jnp.dot(tri_ref[...], a, preferred_element_type=F32, precision=lax.Precision.HIGHEST)
    a_cs_t = a_cs.T
    dt_t = dt.T
    e_in = jnp.exp(a_cs)
    w_out = jnp.exp(a_cs[q - 1:q, :] - a_cs) * dt
    e_exp = jnp.dot(e_in, expand_ref[...], preferred_element_type=F32, precision=lax.Precision.HIGHEST)
    w_exp = jnp.dot(w_out, expand_ref[...], preferred_element_type=F32, precision=lax.Precision.HIGHEST)

    li = lax.broadcasted_iota(jnp.int32, (q, q), 0)
    si = lax.broadcasted_iota(jnp.int32, (q, q), 1)
    causal = li >= si

    for g in range(SSD_GROUPS):
        gs = slice(g * SSD_GROUP_WIDTH, (g + 1) * SSD_GROUP_WIDTH)
        xs = act_scr[:, gs]
        b_g = act_scr[:, D_INNER + g * D_STATE:D_INNER + (g + 1) * D_STATE].astype(BF16)
        c_g = act_scr[:, D_INNER + (SSD_GROUPS + g) * D_STATE:
                      D_INNER + (SSD_GROUPS + g + 1) * D_STATE].astype(BF16)
        cb = lax.dot_general(c_g, b_g, (((1,), (1,)), ((), ())), preferred_element_type=F32)
        st = state_scr[g]
        y_off = jnp.dot(c_g, st.astype(BF16), preferred_element_type=F32) * e_exp[:, gs]
        xs_b = xs.astype(BF16)
        ys = []
        for k in range(SSD_HEADS_PER_GROUP):
            h = g * SSD_HEADS_PER_GROUP + k
            seg = a_cs[:, h:h + 1] - a_cs_t[h:h + 1, :]
            ldec = jnp.exp(jnp.where(causal, seg, -jnp.inf))
            mm = (cb * ldec * dt_t[h:h + 1, :]).astype(BF16)
            ys.append(jnp.dot(mm, xs_b[:, k * SSD_HEAD_DIM:(k + 1) * SSD_HEAD_DIM],
                              preferred_element_type=F32))
        y = jnp.concatenate(ys, axis=1) + y_off + dskip_ref[:, gs] * xs
        y = y * _silu(z_ref[:, gs].astype(F32))
        ms = jnp.mean(y * y, axis=-1, keepdims=True)
        y_ref[:, gs] = (y * lax.rsqrt(ms + EPS) * ng_ref[:, gs]).astype(y_ref.dtype)
        xw = (xs * w_exp[:, gs]).astype(BF16)
        upd = lax.dot_general(b_g, xw, (((0,), (0,)), ((), ())), preferred_element_type=F32)
        state_scr[g] = st * e_exp[q - 1:q, gs] + upd


def _ssd(proj, dt_raw, conv_w, conv_b, dt_bias, a_log, d_skip_exp, norm_g, tri, expand, batch, seq):
    q = SSD_CHUNK
    nc = seq // q
    row = lambda b, c: b * nc + c
    const = lambda b, c: (0, 0)
    return pl.pallas_call(
        _ssd_kernel,
        grid=(batch, nc),
        in_specs=[
            pl.BlockSpec((q, SSD_CONV_DIM), lambda b, c: (row(b, c), COL_XBC // SSD_CONV_DIM)),
            pl.BlockSpec((q, D_INNER), lambda b, c: (row(b, c), COL_Z // D_INNER)),
            pl.BlockSpec((q, LANES), lambda b, c: (row(b, c), 0)),
            pl.BlockSpec((SSD_CONV, SSD_CONV_DIM), const),
            pl.BlockSpec((1, SSD_CONV_DIM), const),
            pl.BlockSpec((1, LANES), const),
            pl.BlockSpec((1, LANES), const),
            pl.BlockSpec((1, D_INNER), const),
            pl.BlockSpec((1, D_INNER), const),
            pl.BlockSpec((q, q), const),
            pl.BlockSpec((LANES, D_INNER), const),
        ],
        out_specs=pl.BlockSpec((q, D_INNER), lambda b, c: (row(b, c), 0)),
        out_shape=jax.ShapeDtypeStruct((batch * seq, D_INNER), BF16),
        scratch_shapes=[
            pltpu.VMEM((q + HALO, SSD_CONV_DIM), F32),
            pltpu.VMEM((q, SSD_CONV_DIM), F32),
            pltpu.VMEM((SSD_GROUPS, D_STATE, SSD_GROUP_WIDTH), F32),
        ],
        compiler_params=pltpu.CompilerParams(
            dimension_semantics=("arbitrary", "arbitrary"), vmem_limit_bytes=VMEM_LIMIT),
        name="ssd",
    )(proj, proj, dt_raw, conv_w, conv_b, dt_bias, a_log, d_skip_exp, norm_g, tri, expand)


def _attn_kernel(q_ref, kc_ref, kp_ref, vc_ref, vp_ref, gq_ref, gk_ref, o_ref, lse_ref, *, n_back):
    blk = ATTN_BLOCK
    n = pl.program_id(2)
    qi = lax.broadcasted_iota(jnp.int32, (blk, 2 * blk), 0)
    ki = lax.broadcasted_iota(jnp.int32, (blk, 2 * blk), 1)
    dist = blk + qi - ki
    key_pos = (n - 1) * blk + ki
    mask = (dist >= 0) & (dist <= n_back) & (key_pos >= 0)

    def head_norm(t, gain):
        ms = jnp.mean(t * t, axis=-1, keepdims=True)
        return (t * lax.rsqrt(ms + EPS) * gain).astype(BF16)

    lse_ref[...] = jnp.zeros(lse_ref.shape, F32)
    for h in range(ATTN_HEADS):
        hs = slice(h * ATTN_HEAD_DIM, (h + 1) * ATTN_HEAD_DIM)
        qn = head_norm(q_ref[:, hs].astype(F32), gq_ref[...])
        k2 = jnp.concatenate([kp_ref[:, hs], kc_ref[:, hs]], axis=0).astype(F32)
        kn = head_norm(k2, gk_ref[...])
        v2 = jnp.concatenate([vp_ref[:, hs], vc_ref[:, hs]], axis=0)
        s = lax.dot_general(qn, kn, (((1,), (1,)), ((), ())), preferred_element_type=F32)
        s = jnp.where(mask, s * (ATTN_HEAD_DIM ** -0.5), -jnp.inf)
        m = jnp.max(s, axis=-1, keepdims=True)
        p = jnp.exp(s - m)
        den = jnp.sum(p, axis=-1, keepdims=True)
        o = jnp.dot(p.astype(BF16), v2, preferred_element_type=F32) * (1.0 / den)
        o_ref[:, hs] = o.astype(o_ref.dtype)
        lse_ref[:, h:h + 1] = m + jnp.log(den)


def _attn_group(proj, gq, gk, gi, batch, seq):
    dil = ATTN_DILATIONS[gi]
    n_back = ATTN_WINDOWS[gi] // dil
    sub_len = seq // dil
    nb = sub_len // ATTN_BLOCK
    blocks_per_row = PROJ_WIDTH // ATTN_GROUP_WIDTH
    view = proj.reshape(batch, sub_len, dil * PROJ_WIDTH)
    blk = (None, ATTN_BLOCK, ATTN_GROUP_WIDTH)

    def col(base):
        return base // ATTN_GROUP_WIDTH + gi

    def cur(base):
        return pl.BlockSpec(blk, lambda b, r, n: (b, n, r * blocks_per_row + col(base)))

    def prev(base):
        return pl.BlockSpec(blk, lambda b, r, n: (b, jnp.maximum(n - 1, 0), r * blocks_per_row + col(base)))

    o, lse = pl.pallas_call(
        functools.partial(_attn_kernel, n_back=n_back),
        grid=(batch, dil, nb),
        in_specs=[cur(COL_Q), cur(COL_K), prev(COL_K), cur(COL_V), prev(COL_V),
                  pl.BlockSpec((1, ATTN_HEAD_DIM), lambda b, r, n: (0, 0)),
                  pl.BlockSpec((1, ATTN_HEAD_DIM), lambda b, r, n: (0, 0))],
        out_specs=[pl.BlockSpec(blk, lambda b, r, n: (b, n, r)),
                   pl.BlockSpec((None, ATTN_BLOCK, LANES), lambda b, r, n: (b, n, r))],
        out_shape=[jax.ShapeDtypeStruct((batch, sub_len, dil * ATTN_GROUP_WIDTH), BF16),
                   jax.ShapeDtypeStruct((batch, sub_len, dil * LANES), F32)],
        compiler_params=pltpu.CompilerParams(
            dimension_semantics=("arbitrary", "arbitrary", "arbitrary"), vmem_limit_bytes=VMEM_LIMIT),
        name=f"attn_g{gi}",
    )(view, view, view, view, view, gq, gk)
    return o.reshape(batch * seq, ATTN_GROUP_WIDTH), lse.reshape(batch * seq, LANES)


def _merge_kernel(x_ref, yssd_ref, o0_ref, o1_ref, o2_ref, l0_ref, l1_ref, l2_ref, gssd_ref, gattn_ref,
                  wssd_ref, wattn_ref, wout_ref, hexp_ref, g2_ref, x1_ref, h2_ref):
    l0, l1, l2 = l0_ref[...], l1_ref[...], l2_ref[...]
    mx = jnp.maximum(jnp.maximum(l0, l1), l2)
    e0, e1, e2 = jnp.exp(l0 - mx), jnp.exp(l1 - mx), jnp.exp(l2 - mx)
    inv = 1.0 / (e0 + e1 + e2)
    y_attn = jnp.zeros(o0_ref.shape, F32)
    for e, o_ref in ((e0, o0_ref), (e1, o1_ref), (e2, o2_ref)):
        wexp = jnp.dot(e * inv, hexp_ref[...], preferred_element_type=F32, precision=lax.Precision.HIGHEST)
        y_attn = y_attn + wexp * o_ref[...].astype(F32)
    p_ssd = jnp.dot(yssd_ref[...], wssd_ref[...], preferred_element_type=F32)
    p_attn = jnp.dot(y_attn.astype(BF16), wattn_ref[...], preferred_element_type=F32)
    merged = (jax.nn.sigmoid(gssd_ref[...].astype(F32)) * p_ssd
              + jax.nn.sigmoid(gattn_ref[...].astype(F32)) * p_attn)
    x1 = x_ref[...] + jnp.dot(merged.astype(BF16), wout_ref[...], preferred_element_type=F32)
    x1_ref[...] = x1
    ms = jnp.mean(x1 * x1, axis=-1, keepdims=True)
    h2_ref[...] = (x1 * lax.rsqrt(ms + EPS) * g2_ref[...]).astype(h2_ref.dtype)


def _merge(x2, y_ssd, outs, lses, proj, w_ssd, w_attn, w_out, head_expand, g2, tm):
    m = x2.shape[0]
    rows = lambda width, cb=0: pl.BlockSpec((tm, width), lambda i: (i, cb))
    const = lambda shape: pl.BlockSpec(shape, lambda i: (0, 0))
    return pl.pallas_call(
        _merge_kernel,
        grid=(m // tm,),
        in_specs=[rows(D_MODEL), rows(D_INNER),
                  rows(ATTN_GROUP_WIDTH), rows(ATTN_GROUP_WIDTH), rows(ATTN_GROUP_WIDTH),
                  rows(LANES), rows(LANES), rows(LANES),
                  rows(D_MODEL, COL_GSSD // D_MODEL), rows(D_MODEL, COL_GATTN // D_MODEL),
                  const((D_INNER, D_MODEL)), const((ATTN_GROUP_WIDTH, D_MODEL)), const((D_MODEL, D_MODEL)),
                  const((LANES, ATTN_GROUP_WIDTH)), const((1, D_MODEL))],
        out_specs=[rows(D_MODEL), rows(D_MODEL)],
        out_shape=[jax.ShapeDtypeStruct((m, D_MODEL), F32), jax.ShapeDtypeStruct((m, D_MODEL), BF16)],
        compiler_params=pltpu.CompilerParams(
            dimension_semantics=("arbitrary",), vmem_limit_bytes=VMEM_LIMIT),
        name="merge",
    )(x2, y_ssd, outs[0], outs[1], outs[2], lses[0], lses[1], lses[2], proj, proj,
      w_ssd, w_attn, w_out, head_expand, g2)


def _ffn_kernel(h_ref, x1_ref, wg_ref, wv_ref, cwg_ref, cwv_ref, cbg_ref, cbv_ref, wd_ref, out_ref,
                ug_scr, uv_scr, halo_g, halo_v, acc_scr, *, tiles_per_seq):
    i = pl.program_id(0)
    f = pl.program_id(1)
    tm = h_ref.shape[0]
    first = (i % tiles_per_seq) == 0

    def conv(u_scr, halo, w_ref, b_ref, wmat_ref):
        u_scr[HALO:HALO + tm, :] = jnp.dot(h_ref[...], wmat_ref[...], preferred_element_type=F32)
        u_scr[0:HALO, :] = jnp.where(first, 0.0, halo[f])
        acc = jnp.broadcast_to(b_ref[...], (tm, u_scr.shape[1]))
        for t in range(FFN_CONV):
            acc = acc + u_scr[pl.ds(HALO - (FFN_CONV - 1) + t, tm), :] * w_ref[t:t + 1, :]
        halo[f] = u_scr[tm:tm + HALO, :]
        return acc

    gate = conv(ug_scr, halo_g, cwg_ref, cbg_ref, wg_ref)
    val = conv(uv_scr, halo_v, cwv_ref, cbv_ref, wv_ref)
    part = jnp.dot((_silu(gate) * val).astype(BF16), wd_ref[...], preferred_element_type=F32)

    @pl.when(f == 0)
    def _():
        acc_scr[...] = x1_ref[...] + part

    @pl.when(f > 0)
    def _():
        acc_scr[...] += part

    @pl.when(f == pl.num_programs(1) - 1)
    def _():
        out_ref[...] = acc_scr[...]


def _ffn(h2, x1, w_up, conv_w, conv_b, w_down, tm, tf, seq):
    m = h2.shape[0]
    nf = D_FF // tf
    return pl.pallas_call(
        functools.partial(_ffn_kernel, tiles_per_seq=seq // tm),
        grid=(m // tm, nf),
        in_specs=[
            pl.BlockSpec((tm, D_MODEL), lambda i, f: (i, 0)),
            pl.BlockSpec((tm, D_MODEL), lambda i, f: (i, 0)),
            pl.BlockSpec((D_MODEL, tf), lambda i, f: (0, f)),
            pl.BlockSpec((D_MODEL, tf), lambda i, f: (0, nf + f)),
            pl.BlockSpec((FFN_CONV, tf), lambda i, f: (0, f)),
            pl.BlockSpec((FFN_CONV, tf), lambda i, f: (0, nf + f)),
            pl.BlockSpec((1, tf), lambda i, f: (0, f)),
            pl.BlockSpec((1, tf), lambda i, f: (0, nf + f)),
            pl.BlockSpec((tf, D_MODEL), lambda i, f: (f, 0)),
        ],
        out_specs=pl.BlockSpec((tm, D_MODEL), lambda i, f: (i, 0)),
        out_shape=jax.ShapeDtypeStruct((m, D_MODEL), F32),
        scratch_shapes=[
            pltpu.VMEM((tm + HALO, tf), F32),
            pltpu.VMEM((tm + HALO, tf), F32),
            pltpu.VMEM((nf, HALO, tf), F32),
            pltpu.VMEM((nf, HALO, tf), F32),
            pltpu.VMEM((tm, D_MODEL), F32),
        ],
        compiler_params=pltpu.CompilerParams(
            dimension_semantics=("arbitrary", "arbitrary"), vmem_limit_bytes=VMEM_LIMIT),
        name="ffn",
    )(h2, x1, w_up, w_up, conv_w, conv_w, conv_b, conv_b, w_down)


def _pad_lanes(v):
    return jnp.pad(v.astype(F32), (0, LANES - v.shape[0])).reshape(1, LANES)


def kernel(x, norm1_g, w_in, ssd_conv_w, ssd_conv_b, dt_bias, a_log, d_skip, ssd_norm_g, w_ssd_proj,
           q_norm_g, k_norm_g, w_attn_proj, w_out, norm2_g, w_up, ffn_conv_w, ffn_conv_b, w_down):
    batch, seq, _ = x.shape
    assert norm1_g.shape[0] == 1, "single-layer block"
    x2 = x.reshape(batch * seq, D_MODEL)

    w = w_in[0]
    o_z, o_xbc = 0, D_INNER
    o_dt = o_xbc + SSD_CONV_DIM
    o_q = o_dt + SSD_HEADS
    o_g = o_q + 3 * ATTN_WIDTH
    w_main = jnp.concatenate(
        [w[:, o_xbc:o_dt], w[:, o_z:o_xbc], w[:, o_g:], w[:, o_q:o_g]], axis=1).astype(BF16)
    w_dt = jnp.pad(w[:, o_dt:o_q], ((0, 0), (0, LANES - SSD_HEADS))).astype(BF16)

    proj, dt_raw = _in_proj(x2, norm1_g[0].reshape(1, D_MODEL), w_main, w_dt, tm=1024, tn=1280)

    tri = jnp.tril(jnp.ones((SSD_CHUNK, SSD_CHUNK), F32))
    head_of_col = jnp.arange(D_INNER) // SSD_HEAD_DIM
    expand = (jnp.arange(LANES)[:, None] == head_of_col[None, :]).astype(F32)
    y_ssd = _ssd(proj, dt_raw, ssd_conv_w[0], ssd_conv_b[0].reshape(1, -1), _pad_lanes(dt_bias[0]),
                 _pad_lanes(a_log[0]), jnp.repeat(d_skip[0], SSD_HEAD_DIM).reshape(1, D_INNER),
                 ssd_norm_g[0].reshape(1, D_INNER), tri, expand, batch, seq)

    gq = q_norm_g[0].reshape(1, ATTN_HEAD_DIM)
    gk = k_norm_g[0].reshape(1, ATTN_HEAD_DIM)
    outs, lses = zip(*[_attn_group(proj, gq, gk, gi, batch, seq) for gi in range(3)])

    attn_head_of_col = jnp.arange(ATTN_GROUP_WIDTH) // ATTN_HEAD_DIM
    head_expand = (jnp.arange(LANES)[:, None] == attn_head_of_col[None, :]).astype(F32)
    x1, h2 = _merge(x2, y_ssd, outs, lses, proj, w_ssd_proj[0].astype(BF16), w_attn_proj[0].astype(BF16),
                    w_out[0].astype(BF16), head_expand, norm2_g[0].reshape(1, D_MODEL), tm=512)

    out = _ffn(h2, x1, w_up[0].astype(BF16), ffn_conv_w[0], ffn_conv_b[0].reshape(1, -1),
               w_down[0].astype(BF16), tm=512, tf=1408, seq=seq)
    return out.reshape(batch, seq, D_MODEL)
```

```python
import functools

import jax
import jax.numpy as jnp
from jax import lax
from jax.experimental import pallas as pl
from jax.experimental.pallas import tpu as pltpu

F32 = jnp.float32
BF16 = jnp.bfloat16

D_MODEL = 1024
D_INNER = 2048
SSD_HEADS = 32
SSD_HEAD_DIM = 64
SSD_GROUPS = 8
SSD_HEADS_PER_GROUP = SSD_HEADS // SSD_GROUPS
D_STATE = 128
SSD_CONV = 4
SSD_CHUNK = 128
SSD_CONV_DIM = D_INNER + 2 * SSD_GROUPS * D_STATE
SSD_GROUP_WIDTH = D_INNER // SSD_GROUPS
ATTN_WINDOWS = (128, 512, 2048)
ATTN_DILATIONS = (1, 4, 16)
ATTN_GROUPS = 3
ATTN_HEADS = 8
ATTN_HEAD_DIM = 64
ATTN_BLOCK = 128
ATTN_GROUP_WIDTH = ATTN_HEADS * ATTN_HEAD_DIM
ATTN_WIDTH = ATTN_GROUPS * ATTN_GROUP_WIDTH
D_FF = 2816
FFN_CONV = 3
EPS = 1e-6
LANES = 128
HALO = 8
ATTN_SLAB = ATTN_BLOCK * max(ATTN_DILATIONS)
ATTN_SLAB_BLOCKS = ATTN_SLAB // ATTN_BLOCK
ATTN_LANE_CHUNKS = ATTN_GROUP_WIDTH // LANES

COL_XBC = 0
COL_Z = COL_XBC + SSD_CONV_DIM
COL_GSSD = COL_Z + D_INNER
COL_GATTN = COL_GSSD + D_MODEL
PROJ_WIDTH = COL_GATTN + D_MODEL

VMEM_LIMIT = 56 * 1024 * 1024


def _silu(v):
    return v * jax.nn.sigmoid(v)


def _rms_rows(x, gain):
    ms = jnp.mean(x * x, axis=-1, keepdims=True)
    return x * lax.rsqrt(ms + EPS) * gain


def _in_proj_main_kernel(x_ref, g_ref, w_ref, wdt_ref, proj_ref, dt_ref, h_scr):
    @pl.when(pl.program_id(1) == 0)
    def _():
        h = _rms_rows(x_ref[...], g_ref[...]).astype(BF16)
        h_scr[...] = h
        dt_ref[...] = jnp.dot(h, wdt_ref[...], preferred_element_type=F32)

    proj_ref[...] = jnp.dot(h_scr[...], w_ref[...], preferred_element_type=F32).astype(BF16)


def _in_proj_main(x2, g1, w_main, w_dt, tm, tn):
    m = x2.shape[0]
    return pl.pallas_call(
        _in_proj_main_kernel,
        grid=(m // tm, PROJ_WIDTH // tn),
        in_specs=[
            pl.BlockSpec((tm, D_MODEL), lambda i, j: (i, 0)),
            pl.BlockSpec((1, D_MODEL), lambda i, j: (0, 0)),
            pl.BlockSpec((D_MODEL, tn), lambda i, j: (0, j)),
            pl.BlockSpec((D_MODEL, LANES), lambda i, j: (0, 0)),
        ],
        out_specs=[
            pl.BlockSpec((tm, tn), lambda i, j: (i, j)),
            pl.BlockSpec((tm, LANES), lambda i, j: (i, 0)),
        ],
        out_shape=[
            jax.ShapeDtypeStruct((m, PROJ_WIDTH), BF16),
            jax.ShapeDtypeStruct((m, LANES), F32),
        ],
        scratch_shapes=[pltpu.VMEM((tm, D_MODEL), BF16)],
        compiler_params=pltpu.CompilerParams(
            dimension_semantics=("arbitrary", "arbitrary"), vmem_limit_bytes=VMEM_LIMIT),
        name="in_proj_main",
    )(x2, g1, w_main, w_dt)


def _in_proj_qkv_kernel(x_ref, g_ref, w_ref, gain_ref, seg_ref, out_ref, h_scr):
    j = pl.program_id(1)

    @pl.when(j == 0)
    def _():
        h_scr[...] = _rms_rows(x_ref[...], g_ref[...]).astype(BF16)

    @pl.when(j < 2)
    def _():
        gain = gain_ref[pl.ds(j, 1), :]
        for c in range(ATTN_WIDTH // LANES):
            cs = slice(c * LANES, (c + 1) * LANES)
            t = jnp.dot(h_scr[...], w_ref[:, cs], preferred_element_type=F32)
            ss = jnp.dot((t * t).astype(BF16), seg_ref[...], preferred_element_type=F32)
            out_ref[:, cs] = (t * lax.rsqrt(ss * (1.0 / ATTN_HEAD_DIM) + EPS) * gain).astype(out_ref.dtype)

    @pl.when(j == 2)
    def _():
        out_ref[...] = jnp.dot(h_scr[...], w_ref[...], preferred_element_type=F32).astype(out_ref.dtype)


def _in_proj_qkv(x2, g1, w_qkv, gains, seg, tm):
    m = x2.shape[0]
    return pl.pallas_call(
        _in_proj_qkv_kernel,
        grid=(m // tm, 3),
        in_specs=[
            pl.BlockSpec((tm, D_MODEL), lambda i, j: (i, 0)),
            pl.BlockSpec((1, D_MODEL), lambda i, j: (0, 0)),
            pl.BlockSpec((D_MODEL, ATTN_WIDTH), lambda i, j: (0, j)),
            pl.BlockSpec((2, LANES), lambda i, j: (0, 0)),
            pl.BlockSpec((LANES, LANES), lambda i, j: (0, 0)),
        ],
        out_specs=pl.BlockSpec((tm, ATTN_WIDTH), lambda i, j: (i, j)),
        out_shape=jax.ShapeDtypeStruct((m, 3 * ATTN_WIDTH), BF16),
        scratch_shapes=[pltpu.VMEM((tm, D_MODEL), BF16)],
        compiler_params=pltpu.CompilerParams(
            dimension_semantics=("arbitrary", "arbitrary"), vmem_limit_bytes=VMEM_LIMIT),
        name="in_proj_qkv",
    )(x2, g1, w_qkv, gains, seg)


def _ssd_kernel(xbc_ref, z_ref, dt_ref, cw_ref, cb_ref, dtb_ref, alog_ref, dskip_ref, ng_ref,
                tri_ref, expand_ref, y_ref, xp_scr, act_scr, state_scr):
    q = SSD_CHUNK

    @pl.when(pl.program_id(1) == 0)
    def _():
        xp_scr[0:HALO, :] = jnp.zeros((HALO, SSD_CONV_DIM), F32)
        state_scr[...] = jnp.zeros(state_scr.shape, F32)

    xp_scr[HALO:HALO + q, :] = xbc_ref[...].astype(F32)
    acc = jnp.broadcast_to(cb_ref[...], (q, SSD_CONV_DIM))
    for i in range(SSD_CONV):
        acc = acc + xp_scr[pl.ds(HALO - (SSD_CONV - 1) + i, q), :] * cw_ref[i:i + 1, :]
    act_scr[...] = _silu(acc)
    xp_scr[0:HALO, :] = xp_scr[q:q + HALO, :]

    dt = jax.nn.softplus(dt_ref[...] + dtb_ref[...])
    a = dt * (-jnp.exp(alog_ref[...]))
    a_cs = jnp.dot(tri_ref[...], a, preferred_element_type=F32, precision=lax.Precision.HIGHEST)
    a_cs_t = a_cs.T
    dt_t = dt.T
    e_in = jnp.exp(a_cs)
    w_out = jnp.exp(a_cs[q - 1:q, :] - a_cs) * dt
    e_exp = jnp.dot(e_in, expand_ref[...], preferred_element_type=F32, precision=lax.Precision.HIGHEST)
    w_exp = jnp.dot(w_out, expand_ref[...], preferred_element_type=F32, precision=lax.Precision.HIGHEST)

    li = lax.broadcasted_iota(jnp.int32, (q, q), 0)
    si = lax.broadcasted_iota(jnp.int32, (q, q), 1)
    causal = li >= si

    for g in range(SSD_GROUPS):
        gs = slice(g * SSD_GROUP_WIDTH, (g + 1) * SSD_GROUP_WIDTH)
        xs = act_scr[:, gs]
        b_g = act_scr[:, D_INNER + g * D_STATE:D_INNER + (g + 1) * D_STATE].astype(BF16)
        c_g = act_scr[:, D_INNER + (SSD_GROUPS + g) * D_STATE:
                      D_INNER + (SSD_GROUPS + g + 1) * D_STATE].astype(BF16)
        cb = lax.dot_general(c_g, b_g, (((1,), (1,)), ((), ())), preferred_element_type=F32)
        st = state_scr[g]
        y_off = jnp.dot(c_g, st.astype(BF16), preferred_element_type=F32) * e_exp[:, gs]
        xs_b = xs.astype(BF16)
        ys = []
        for k in range(SSD_HEADS_PER_GROUP):
            h = g * SSD_HEADS_PER_GROUP + k
            seg = a_cs[:, h:h + 1] - a_cs_t[h:h + 1, :]
            ldec = jnp.exp(jnp.where(causal, seg, -jnp.inf))
            mm = (cb * ldec * dt_t[h:h + 1, :]).astype(BF16)
            ys.append(jnp.dot(mm, xs_b[:, k * SSD_HEAD_DIM:(k + 1) * SSD_HEAD_DIM],
                              preferred_element_type=F32))
        y = jnp.concatenate(ys, axis=1) + y_off + dskip_ref[:, gs] * xs
        y = y * _silu(z_ref[:, gs].astype(F32))
        ms = jnp.mean(y * y, axis=-1, keepdims=True)
        y_ref[:, gs] = (y * lax.rsqrt(ms + EPS) * ng_ref[:, gs]).astype(y_ref.dtype)
        xw = (xs * w_exp[:, gs]).astype(BF16)
        upd = lax.dot_general(b_g, xw, (((0,), (0,)), ((), ())), preferred_element_type=F32)
        state_scr[g] = st * e_exp[q - 1:q, gs] + upd


def _ssd(proj, dt_raw, conv_w, conv_b, dt_bias, a_log, d_skip_exp, norm_g, tri, expand, batch, seq):
    q = SSD_CHUNK
    nc = seq // q
    row = lambda b, c: b * nc + c
    const = lambda b, c: (0, 0)
    return pl.pallas_call(
        _ssd_kernel,
        grid=(batch, nc),
        in_specs=[
            pl.BlockSpec((q, SSD_CONV_DIM), lambda b, c: (row(b, c), COL_XBC // SSD_CONV_DIM)),
            pl.BlockSpec((q, D_INNER), lambda b, c: (row(b, c), COL_Z // D_INNER)),
            pl.BlockSpec((q, LANES), lambda b, c: (row(b, c), 0)),
            pl.BlockSpec((SSD_CONV, SSD_CONV_DIM), const),
            pl.BlockSpec((1, SSD_CONV_DIM), const),
            pl.BlockSpec((1, LANES), const),
            pl.BlockSpec((1, LANES), const),
            pl.BlockSpec((1, D_INNER), const),
            pl.BlockSpec((1, D_INNER), const),
            pl.BlockSpec((q, q), const),
            pl.BlockSpec((LANES, D_INNER), const),
        ],
        out_specs=pl.BlockSpec((q, D_INNER), lambda b, c: (row(b, c), 0)),
        out_shape=jax.ShapeDtypeStruct((batch * seq, D_INNER), BF16),
        scratch_shapes=[
            pltpu.VMEM((q + HALO, SSD_CONV_DIM), F32),
            pltpu.VMEM((q, SSD_CONV_DIM), F32),
            pltpu.VMEM((SSD_GROUPS, D_STATE, SSD_GROUP_WIDTH), F32),
        ],
        compiler_params=pltpu.CompilerParams(
            dimension_semantics=("arbitrary", "arbitrary"), vmem_limit_bytes=VMEM_LIMIT),
        name="ssd",
    )(proj, proj, dt_raw, conv_w, conv_b, dt_bias, a_log, d_skip_exp, norm_g, tri, expand)


def _attn_kernel(q_ref, kc_ref, kp_ref, vc_ref, vp_ref, o_ref, lse_ref, stage, q_de, k_de, v_de,
                 *, dil, n_back):
    blk = ATTN_BLOCK
    per_res = ATTN_SLAB_BLOCKS // dil
    prev_rows = blk * dil
    first_slab = pl.program_id(1) == 0

    def deinterleave(cur_ref, prev_ref, dst, run):
        base = 0
        for c in range(ATTN_LANE_CHUNKS):
            cs = slice(c * LANES, (c + 1) * LANES)
            if prev_ref is not None:
                stage[c, 0:prev_rows, :] = prev_ref[:, cs].astype(F32)
                base = prev_rows
            stage[c, base:base + ATTN_SLAB, :] = cur_ref[:, cs].astype(F32)

        def body(bi, carry):
            start = bi // run + dil * blk * (bi % run)
            for c in range(ATTN_LANE_CHUNKS):
                dst[bi, :, c * LANES:(c + 1) * LANES] = (
                    stage[c, pl.ds(start, blk, stride=dil), :].astype(BF16))
            return carry

        lax.fori_loop(0, dil * run, body, 0)

    deinterleave(kc_ref, kp_ref, k_de, per_res + 1)
    deinterleave(vc_ref, vp_ref, v_de, per_res + 1)
    deinterleave(q_ref, None, q_de, per_res)

    lane = lax.broadcasted_iota(jnp.int32, (blk, LANES), 1)
    even = lane < ATTN_HEAD_DIM
    qi = lax.broadcasted_iota(jnp.int32, (blk, 2 * blk), 0)
    ki = lax.broadcasted_iota(jnp.int32, (blk, 2 * blk), 1)
    dist = blk + qi - ki
    band = (dist >= 0) & (dist <= n_back)

    def attend(bi, carry):
        res = bi // per_res
        j = bi % per_res
        key_lo = jnp.where(jnp.logical_and(first_slab, j == 0), blk, 0)
        mask = band & (ki >= key_lo)
        kv = res * (per_res + 1) + j
        start = res + dil * blk * j
        lse_tile = jnp.zeros((blk, LANES), F32)
        for hp in range(ATTN_LANE_CHUNKS):
            ps = slice(hp * LANES, (hp + 1) * LANES)
            q_pair = q_de[bi, :, ps]
            k2 = jnp.concatenate([k_de[kv, :, ps], k_de[kv + 1, :, ps]], axis=0)
            v2 = jnp.concatenate([v_de[kv, :, ps], v_de[kv + 1, :, ps]], axis=0)
            halves = []
            for e, sel in enumerate((even, jnp.logical_not(even))):
                qm = jnp.where(sel, q_pair, jnp.zeros_like(q_pair))
                s = lax.dot_general(qm, k2, (((1,), (1,)), ((), ())), preferred_element_type=F32)
                s = jnp.where(mask, s, -jnp.inf)
                m = jnp.max(s, axis=-1, keepdims=True)
                p = jnp.exp(s - m)
                den = jnp.sum(p, axis=-1, keepdims=True)
                halves.append(jnp.dot(p.astype(BF16), v2, preferred_element_type=F32) * (1.0 / den))
                lse_tile = jnp.where(lane == 2 * hp + e, m + jnp.log(den), lse_tile)
            stage[hp, pl.ds(start, blk, stride=dil), :] = jnp.where(even, halves[0], halves[1])
        lse_ref[pl.ds(start, blk, stride=dil), :] = lse_tile
        return carry

    lax.fori_loop(0, ATTN_SLAB_BLOCKS, attend, 0)
    for c in range(ATTN_LANE_CHUNKS):
        o_ref[:, c * LANES:(c + 1) * LANES] = stage[c, 0:ATTN_SLAB, :].astype(o_ref.dtype)


def _attn_group(qkv, gi, batch, seq):
    dil = ATTN_DILATIONS[gi]
    n_back = ATTN_WINDOWS[gi] // dil
    assert n_back <= ATTN_BLOCK and ATTN_SLAB_BLOCKS % dil == 0 and seq % ATTN_SLAB == 0
    slabs = seq // ATTN_SLAB
    prev_rows = ATTN_BLOCK * dil
    prev_per_slab = ATTN_SLAB // prev_rows

    def cur(which):
        return pl.BlockSpec((ATTN_SLAB, ATTN_GROUP_WIDTH), lambda b, s: (b * slabs + s, which * ATTN_GROUPS + gi))

    def prev(which):
        return pl.BlockSpec(
            (prev_rows, ATTN_GROUP_WIDTH),
            lambda b, s: (jnp.maximum((b * slabs + s) * prev_per_slab - 1, 0), which * ATTN_GROUPS + gi))

    run = ATTN_SLAB_BLOCKS // dil + 1
    return pl.pallas_call(
        functools.partial(_attn_kernel, dil=dil, n_back=n_back),
        grid=(batch, slabs),
        in_specs=[cur(0), cur(1), prev(1), cur(2), prev(2)],
        out_specs=[pl.BlockSpec((ATTN_SLAB, ATTN_GROUP_WIDTH), lambda b, s: (b * slabs + s, 0)),
                   pl.BlockSpec((ATTN_SLAB, LANES), lambda b, s: (b * slabs + s, 0))],
        out_shape=[jax.ShapeDtypeStruct((batch * seq, ATTN_GROUP_WIDTH), BF16),
                   jax.ShapeDtypeStruct((batch * seq, LANES), F32)],
        scratch_shapes=[
            pltpu.VMEM((ATTN_LANE_CHUNKS, prev_rows + ATTN_SLAB, LANES), F32),
            pltpu.VMEM((ATTN_SLAB_BLOCKS, ATTN_BLOCK, ATTN_GROUP_WIDTH), BF16),
            pltpu.VMEM((dil * run, ATTN_BLOCK, ATTN_GROUP_WIDTH), BF16),
            pltpu.VMEM((dil * run, ATTN_BLOCK, ATTN_GROUP_WIDTH), BF16),
        ],
        compiler_params=pltpu.CompilerParams(
            dimension_semantics=("arbitrary", "arbitrary"), vmem_limit_bytes=VMEM_LIMIT),
        name=f"attn_g{gi}",
    )(qkv, qkv, qkv, qkv, qkv)


def _expand_heads(w, hexp):
    hi = w.astype(BF16)
    lo = (w - hi.astype(F32)).astype(BF16)
    return (jnp.dot(hi, hexp, preferred_element_type=F32) + jnp.dot(lo, hexp, preferred_element_type=F32))


def _merge_kernel(x_ref, yssd_ref, o0_ref, o1_ref, o2_ref, l0_ref, l1_ref, l2_ref, gssd_ref, gattn_ref,
                  wssd_ref, wattn_ref, wout_ref, hexp_ref, g2_ref, x1_ref, h2_ref):
    l0, l1, l2 = l0_ref[...], l1_ref[...], l2_ref[...]
    mx = jnp.maximum(jnp.maximum(l0, l1), l2)
    e0, e1, e2 = jnp.exp(l0 - mx), jnp.exp(l1 - mx), jnp.exp(l2 - mx)
    inv = 1.0 / (e0 + e1 + e2)
    y_attn = jnp.zeros(o0_ref.shape, F32)
    for e, o_ref in ((e0, o0_ref), (e1, o1_ref), (e2, o2_ref)):
        y_attn = y_attn + _expand_heads(e * inv, hexp_ref[...]) * o_ref[...].astype(F32)
    p_ssd = jnp.dot(yssd_ref[...], wssd_ref[...], preferred_element_type=F32)
    p_attn = jnp.dot(y_attn.astype(BF16), wattn_ref[...], preferred_element_type=F32)
    merged = (jax.nn.sigmoid(gssd_ref[...].astype(F32)) * p_ssd
              + jax.nn.sigmoid(gattn_ref[...].astype(F32)) * p_attn)
    x1 = x_ref[...] + jnp.dot(merged.astype(BF16), wout_ref[...], preferred_element_type=F32)
    x1_ref[...] = x1
    h2_ref[...] = _rms_rows(x1, g2_ref[...]).astype(h2_ref.dtype)


def _merge(x2, y_ssd, outs, lses, proj, w_ssd, w_attn, w_out, head_expand, g2, tm):
    m = x2.shape[0]
    rows = lambda width, cb=0: pl.BlockSpec((tm, width), lambda i: (i, cb))
    const = lambda shape: pl.BlockSpec(shape, lambda i: (0, 0))
    return pl.pallas_call(
        _merge_kernel,
        grid=(m // tm,),
        in_specs=[rows(D_MODEL), rows(D_INNER),
                  rows(ATTN_GROUP_WIDTH), rows(ATTN_GROUP_WIDTH), rows(ATTN_GROUP_WIDTH),
                  rows(LANES), rows(LANES), rows(LANES),
                  rows(D_MODEL, COL_GSSD // D_MODEL), rows(D_MODEL, COL_GATTN // D_MODEL),
                  const((D_INNER, D_MODEL)), const((ATTN_GROUP_WIDTH, D_MODEL)), const((D_MODEL, D_MODEL)),
                  const((LANES, ATTN_GROUP_WIDTH)), const((1, D_MODEL))],
        out_specs=[rows(D_MODEL), rows(D_MODEL)],
        out_shape=[jax.ShapeDtypeStruct((m, D_MODEL), F32), jax.ShapeDtypeStruct((m, D_MODEL), BF16)],
        compiler_params=pltpu.CompilerParams(
            dimension_semantics=("arbitrary",), vmem_limit_bytes=VMEM_LIMIT),
        name="merge",
    )(x2, y_ssd, outs[0], outs[1], outs[2], lses[0], lses[1], lses[2], proj, proj,
      w_ssd, w_attn, w_out, head_expand, g2)


def _ffn_kernel(h_ref, x1_ref, wg_ref, wv_ref, cwg_ref, cwv_ref, cbg_ref, cbv_ref, wd_ref, out_ref,
                ug_scr, uv_scr, halo_g, halo_v, acc_scr, *, tiles_per_seq):
    i = pl.program_id(0)
    f = pl.program_id(1)
    tm = h_ref.shape[0]
    first = (i % tiles_per_seq) == 0

    def conv(u_scr, halo, w_ref, b_ref, wmat_ref):
        u_scr[HALO:HALO + tm, :] = jnp.dot(h_ref[...], wmat_ref[...], preferred_element_type=F32)
        u_scr[0:HALO, :] = jnp.where(first, 0.0, halo[f])
        acc = jnp.broadcast_to(b_ref[...], (tm, u_scr.shape[1]))
        for t in range(FFN_CONV):
            acc = acc + u_scr[pl.ds(HALO - (FFN_CONV - 1) + t, tm), :] * w_ref[t:t + 1, :]
        halo[f] = u_scr[tm:tm + HALO, :]
        return acc

    gate = conv(ug_scr, halo_g, cwg_ref, cbg_ref, wg_ref)
    val = conv(uv_scr, halo_v, cwv_ref, cbv_ref, wv_ref)
    part = jnp.dot((_silu(gate) * val).astype(BF16), wd_ref[...], preferred_element_type=F32)

    @pl.when(f == 0)
    def _():
        acc_scr[...] = x1_ref[...] + part

    @pl.when(f > 0)
    def _():
        acc_scr[...] += part

    @pl.when(f == pl.num_programs(1) - 1)
    def _():
        out_ref[...] = acc_scr[...]


def _ffn(h2, x1, w_up, conv_w, conv_b, w_down, tm, tf, seq):
    m = h2.shape[0]
    nf = D_FF // tf
    return pl.pallas_call(
        functools.partial(_ffn_kernel, tiles_per_seq=seq // tm),
        grid=(m // tm, nf),
        in_specs=[
            pl.BlockSpec((tm, D_MODEL), lambda i, f: (i, 0)),
            pl.BlockSpec((tm, D_MODEL), lambda i, f: (i, 0)),
            pl.BlockSpec((D_MODEL, tf), lambda i, f: (0, f)),
            pl.BlockSpec((D_MODEL, tf), lambda i, f: (0, nf + f)),
            pl.BlockSpec((FFN_CONV, tf), lambda i, f: (0, f)),
            pl.BlockSpec((FFN_CONV, tf), lambda i, f: (0, nf + f)),
            pl.BlockSpec((1, tf), lambda i, f: (0, f)),
            pl.BlockSpec((1, tf), lambda i, f: (0, nf + f)),
            pl.BlockSpec((tf, D_MODEL), lambda i, f: (f, 0)),
        ],
        out_specs=pl.BlockSpec((tm, D_MODEL), lambda i, f: (i, 0)),
        out_shape=jax.ShapeDtypeStruct((m, D_MODEL), F32),
        scratch_shapes=[
            pltpu.VMEM((tm + HALO, tf), F32),
            pltpu.VMEM((tm + HALO, tf), F32),
            pltpu.VMEM((nf, HALO, tf), F32),
            pltpu.VMEM((nf, HALO, tf), F32),
            pltpu.VMEM((tm, D_MODEL), F32),
        ],
        compiler_params=pltpu.CompilerParams(
            dimension_semantics=("arbitrary", "arbitrary"), vmem_limit_bytes=VMEM_LIMIT),
        name="ffn",
    )(h2, x1, w_up, w_up, conv_w, conv_w, conv_b, conv_b, w_down)


def _pad_lanes(v):
    return jnp.pad(v.astype(F32), (0, LANES - v.shape[0])).reshape(1, LANES)


def kernel(x, norm1_g, w_in, ssd_conv_w, ssd_conv_b, dt_bias, a_log, d_skip, ssd_norm_g, w_ssd_proj,
           q_norm_g, k_norm_g, w_attn_proj, w_out, norm2_g, w_up, ffn_conv_w, ffn_conv_b, w_down):
    batch, seq, _ = x.shape
    assert norm1_g.shape[0] == 1, "single-layer block"
    x2 = x.reshape(batch * seq, D_MODEL)
    g1 = norm1_g[0].reshape(1, D_MODEL)

    w = w_in[0]
    o_xbc = D_INNER
    o_dt = o_xbc + SSD_CONV_DIM
    o_q = o_dt + SSD_HEADS
    o_g = o_q + 3 * ATTN_WIDTH
    w_main = jnp.concatenate([w[:, o_xbc:o_dt], w[:, :o_xbc], w[:, o_g:]], axis=1).astype(BF16)
    w_dt = jnp.pad(w[:, o_dt:o_q], ((0, 0), (0, LANES - SSD_HEADS))).astype(BF16)
    w_qkv = w[:, o_q:o_g].astype(BF16)

    proj, dt_raw = _in_proj_main(x2, g1, w_main, w_dt, tm=1024, tn=2048)

    heads_per_tile = LANES // ATTN_HEAD_DIM
    qk_gains = jnp.stack([jnp.tile(q_norm_g[0] * (ATTN_HEAD_DIM ** -0.5), heads_per_tile),
                          jnp.tile(k_norm_g[0], heads_per_tile)]).astype(F32)
    lane_head = jnp.arange(LANES) // ATTN_HEAD_DIM
    seg = (lane_head[:, None] == lane_head[None, :]).astype(BF16)
    qkv = _in_proj_qkv(x2, g1, w_qkv, qk_gains, seg, tm=1024)

    tri = jnp.tril(jnp.ones((SSD_CHUNK, SSD_CHUNK), F32))
    head_of_col = jnp.arange(D_INNER) // SSD_HEAD_DIM
    expand = (jnp.arange(LANES)[:, None] == head_of_col[None, :]).astype(F32)
    y_ssd = _ssd(proj, dt_raw, ssd_conv_w[0], ssd_conv_b[0].reshape(1, -1), _pad_lanes(dt_bias[0]),
                 _pad_lanes(a_log[0]), jnp.repeat(d_skip[0], SSD_HEAD_DIM).reshape(1, D_INNER),
                 ssd_norm_g[0].reshape(1, D_INNER), tri, expand, batch, seq)

    outs, lses = zip(*[_attn_group(qkv, gi, batch, seq) for gi in range(ATTN_GROUPS)])

    attn_head_of_col = jnp.arange(ATTN_GROUP_WIDTH) // ATTN_HEAD_DIM
    head_expand = (jnp.arange(LANES)[:, None] == attn_head_of_col[None, :]).astype(BF16)
    x1, h2 = _merge(x2, y_ssd, outs, lses, proj, w_ssd_proj[0].astype(BF16), w_attn_proj[0].astype(BF16),
                    w_out[0].astype(BF16), head_expand, norm2_g[0].reshape(1, D_MODEL), tm=512)

    out = _ffn(h2, x1, w_up[0].astype(BF16), ffn_conv_w[0], ffn_conv_b[0].reshape(1, -1),
               w_down[0].astype(BF16), tm=512, tf=1408, seq=seq)
    return out.reshape(batch, seq, D_MODEL)
```

```python
import functools

import jax
import jax.numpy as jnp
from jax import lax
from jax.experimental import pallas as pl
from jax.experimental.pallas import tpu as pltpu

F32 = jnp.float32
BF16 = jnp.bfloat16

D_MODEL = 1024
D_INNER = 2048
SSD_HEADS = 32
SSD_HEAD_DIM = 64
SSD_GROUPS = 8
SSD_HEADS_PER_GROUP = SSD_HEADS // SSD_GROUPS
D_STATE = 128
SSD_CONV = 4
SSD_CHUNK = 128
SSD_CONV_DIM = D_INNER + 2 * SSD_GROUPS * D_STATE
SSD_GROUP_WIDTH = D_INNER // SSD_GROUPS
ATTN_WINDOWS = (128, 512, 2048)
ATTN_DILATIONS = (1, 4, 16)
ATTN_GROUPS = 3
ATTN_HEADS = 8
ATTN_HEAD_DIM = 64
ATTN_BLOCK = 128
ATTN_GROUP_WIDTH = ATTN_HEADS * ATTN_HEAD_DIM
ATTN_WIDTH = ATTN_GROUPS * ATTN_GROUP_WIDTH
D_FF = 2816
FFN_CONV = 3
EPS = 1e-6
LANES = 128
MXU_WIDTH = 256
HALO = 8
ATTN_SLAB = ATTN_BLOCK * max(ATTN_DILATIONS)
ATTN_SLAB_BLOCKS = ATTN_SLAB // ATTN_BLOCK
ATTN_LANE_CHUNKS = ATTN_GROUP_WIDTH // LANES

COL_XBC = 0
COL_Z = COL_XBC + SSD_CONV_DIM
COL_GSSD = COL_Z + D_INNER
COL_GATTN = COL_GSSD + D_MODEL
PROJ_WIDTH = COL_GATTN + D_MODEL

VMEM_LIMIT = 56 * 1024 * 1024


def _sigmoid(v):
    return 0.5 * jnp.tanh(0.5 * v) + 0.5


def _silu(v):
    return v * _sigmoid(v)


def _silu_of_half(hv):
    return hv + hv * jnp.tanh(hv)


def _split_bf16(v, terms):
    parts = []
    for _ in range(terms):
        p = v.astype(BF16)
        parts.append(p)
        v = v - p.astype(F32)
    return parts


def _dot_split(parts, rhs, lhs_side=True):
    acc = None
    for p in parts:
        t = (jnp.dot(p, rhs, preferred_element_type=F32) if lhs_side
             else jnp.dot(rhs, p, preferred_element_type=F32))
        acc = t if acc is None else acc + t
    return acc


def _rms_rows(x, gain):
    ms = jnp.mean(x * x, axis=-1, keepdims=True)
    return x * lax.rsqrt(ms + EPS) * gain


def _in_proj_main_kernel(x_ref, g_ref, w_ref, wdt_ref, proj_ref, dt_ref, h_scr):
    @pl.when(pl.program_id(1) == 0)
    def _():
        h = _rms_rows(x_ref[...], g_ref[...]).astype(BF16)
        h_scr[...] = h
        dt_ref[...] = jnp.dot(h, wdt_ref[...], preferred_element_type=F32)

    proj_ref[...] = jnp.dot(h_scr[...], w_ref[...], preferred_element_type=F32).astype(BF16)


def _in_proj_main(x2, g1, w_main, w_dt, tm, tn):
    m = x2.shape[0]
    return pl.pallas_call(
        _in_proj_main_kernel,
        grid=(m // tm, PROJ_WIDTH // tn),
        in_specs=[
            pl.BlockSpec((tm, D_MODEL), lambda i, j: (i, 0)),
            pl.BlockSpec((1, D_MODEL), lambda i, j: (0, 0)),
            pl.BlockSpec((D_MODEL, tn), lambda i, j: (0, j)),
            pl.BlockSpec((D_MODEL, LANES), lambda i, j: (0, 0)),
        ],
        out_specs=[
            pl.BlockSpec((tm, tn), lambda i, j: (i, j)),
            pl.BlockSpec((tm, LANES), lambda i, j: (i, 0)),
        ],
        out_shape=[
            jax.ShapeDtypeStruct((m, PROJ_WIDTH), BF16),
            jax.ShapeDtypeStruct((m, LANES), F32),
        ],
        scratch_shapes=[pltpu.VMEM((tm, D_MODEL), BF16)],
        compiler_params=pltpu.CompilerParams(
            dimension_semantics=("arbitrary", "arbitrary"), vmem_limit_bytes=VMEM_LIMIT),
        name="in_proj_main",
    )(x2, g1, w_main, w_dt)


def _in_proj_qkv_kernel(x_ref, g_ref, w_ref, gain_ref, seg_ref, out_ref, h_scr, t_scr):
    j = pl.program_id(1)
    seg_w = seg_ref.shape[0]

    @pl.when(j == 0)
    def _():
        h_scr[...] = _rms_rows(x_ref[...], g_ref[...]).astype(BF16)

    @pl.when(j < 2)
    def _():
        t_scr[...] = jnp.dot(h_scr[...], w_ref[...], preferred_element_type=F32)
        gain = gain_ref[pl.ds(j, 1), :]
        for c in range(ATTN_WIDTH // seg_w):
            cs = slice(c * seg_w, (c + 1) * seg_w)
            t = t_scr[:, cs]
            ss = jnp.dot((t * t).astype(BF16), seg_ref[...], preferred_element_type=F32)
            out_ref[:, cs] = (t * lax.rsqrt(ss * (1.0 / ATTN_HEAD_DIM) + EPS) * gain).astype(out_ref.dtype)

    @pl.when(j == 2)
    def _():
        out_ref[...] = jnp.dot(h_scr[...], w_ref[...], preferred_element_type=F32).astype(out_ref.dtype)


def _in_proj_qkv(x2, g1, w_qkv, gains, seg, tm):
    m = x2.shape[0]
    return pl.pallas_call(
        _in_proj_qkv_kernel,
        grid=(m // tm, 3),
        in_specs=[
            pl.BlockSpec((tm, D_MODEL), lambda i, j: (i, 0)),
            pl.BlockSpec((1, D_MODEL), lambda i, j: (0, 0)),
            pl.BlockSpec((D_MODEL, ATTN_WIDTH), lambda i, j: (0, j)),
            pl.BlockSpec(gains.shape, lambda i, j: (0, 0)),
            pl.BlockSpec(seg.shape, lambda i, j: (0, 0)),
        ],
        out_specs=pl.BlockSpec((tm, ATTN_WIDTH), lambda i, j: (i, j)),
        out_shape=jax.ShapeDtypeStruct((m, 3 * ATTN_WIDTH), BF16),
        scratch_shapes=[pltpu.VMEM((tm, D_MODEL), BF16), pltpu.VMEM((tm, ATTN_WIDTH), F32)],
        compiler_params=pltpu.CompilerParams(
            dimension_semantics=("arbitrary", "arbitrary"), vmem_limit_bytes=VMEM_LIMIT),
        name="in_proj_qkv",
    )(x2, g1, w_qkv, gains, seg)


def _ssd_kernel(xbc_ref, z_ref, dt_ref, cw_ref, cb_ref, dtb_ref, alog_ref, dskip_ref, ng_ref,
                tri_ref, expand_ref, y_ref, xp_scr, act_scr, state_scr):
    q = SSD_CHUNK

    @pl.when(pl.program_id(1) == 0)
    def _():
        xp_scr[:, 0:2 * HALO, :] = jnp.zeros((SSD_CONV_DIM // LANES, 2 * HALO, LANES), F32)
        state_scr[...] = jnp.zeros(state_scr.shape, F32)

    for c in range(SSD_CONV_DIM // LANES):
        cs = slice(c * LANES, (c + 1) * LANES)
        xp_scr[c, pl.ds(2 * HALO, q, stride=2), :] = xbc_ref[:, cs].astype(F32)
        acc = jnp.broadcast_to(0.5 * cb_ref[:, cs], (q, LANES))
        for i in range(SSD_CONV):
            tap = xp_scr[c, pl.ds(2 * (HALO - (SSD_CONV - 1) + i), q, stride=2), :]
            acc = acc + tap * (0.5 * cw_ref[i:i + 1, cs])
        act_scr[:, cs] = _silu_of_half(acc)
        xp_scr[c, 0:2 * HALO, :] = xp_scr[c, 2 * q:2 * (q + HALO), :]

    dt = jax.nn.softplus(dt_ref[...] + dtb_ref[...])
    a = dt * (-jnp.exp(alog_ref[...]))
    a_cs = _dot_split(_split_bf16(a, 3), tri_ref[...], lhs_side=False)
    a_cs_t = a_cs.T
    a_cs_t_adj = a_cs_t - jnp.log(dt.T)
    e_in = jnp.exp(a_cs)
    w_out = jnp.exp(a_cs[q - 1:q, :] - a_cs) * dt
    e_exp = _dot_split(_split_bf16(e_in, 2), expand_ref[...])
    w_exp = _dot_split(_split_bf16(w_out, 2), expand_ref[...])

    li = lax.broadcasted_iota(jnp.int32, (q, q), 0)
    si = lax.broadcasted_iota(jnp.int32, (q, q), 1)
    causal = li >= si

    for g in range(SSD_GROUPS):
        gs = slice(g * SSD_GROUP_WIDTH, (g + 1) * SSD_GROUP_WIDTH)
        xs = act_scr[:, gs]
        b_g = act_scr[:, D_INNER + g * D_STATE:D_INNER + (g + 1) * D_STATE].astype(BF16)
        c_g = act_scr[:, D_INNER + (SSD_GROUPS + g) * D_STATE:
                      D_INNER + (SSD_GROUPS + g + 1) * D_STATE].astype(BF16)
        cb = lax.dot_general(c_g, b_g, (((1,), (1,)), ((), ())), preferred_element_type=F32)
        st = state_scr[g]
        y_off = jnp.dot(c_g, st.astype(BF16), preferred_element_type=F32) * e_exp[:, gs]
        xs_b = xs.astype(BF16)
        ys = []
        for k in range(SSD_HEADS_PER_GROUP):
            h = g * SSD_HEADS_PER_GROUP + k
            seg = a_cs[:, h:h + 1] - a_cs_t_adj[h:h + 1, :]
            mm = (cb * jnp.exp(jnp.where(causal, seg, -jnp.inf))).astype(BF16)
            ys.append(jnp.dot(mm, xs_b[:, k * SSD_HEAD_DIM:(k + 1) * SSD_HEAD_DIM],
                              preferred_element_type=F32))
        y = jnp.concatenate(ys, axis=1) + y_off + dskip_ref[:, gs] * xs
        y = y * _silu_of_half(0.5 * z_ref[:, gs].astype(F32))
        ms = jnp.mean(y * y, axis=-1, keepdims=True)
        y_ref[:, gs] = (y * lax.rsqrt(ms + EPS) * ng_ref[:, gs]).astype(y_ref.dtype)
        xw = (xs * w_exp[:, gs]).astype(BF16)
        upd = lax.dot_general(b_g, xw, (((0,), (0,)), ((), ())), preferred_element_type=F32)
        state_scr[g] = st * e_exp[q - 1:q, gs] + upd


def _ssd(proj, dt_raw, conv_w, conv_b, dt_bias, a_log, d_skip_exp, norm_g, tri, expand, batch, seq):
    q = SSD_CHUNK
    nc = seq // q
    row = lambda b, c: b * nc + c
    const = lambda b, c: (0, 0)
    return pl.pallas_call(
        _ssd_kernel,
        grid=(batch, nc),
        in_specs=[
            pl.BlockSpec((q, SSD_CONV_DIM), lambda b, c: (row(b, c), COL_XBC // SSD_CONV_DIM)),
            pl.BlockSpec((q, D_INNER), lambda b, c: (row(b, c), COL_Z // D_INNER)),
            pl.BlockSpec((q, LANES), lambda b, c: (row(b, c), 0)),
            pl.BlockSpec((SSD_CONV, SSD_CONV_DIM), const),
            pl.BlockSpec((1, SSD_CONV_DIM), const),
            pl.BlockSpec((1, LANES), const),
            pl.BlockSpec((1, LANES), const),
            pl.BlockSpec((1, D_INNER), const),
            pl.BlockSpec((1, D_INNER), const),
            pl.BlockSpec((q, q), const),
            pl.BlockSpec((LANES, D_INNER), const),
        ],
        out_specs=pl.BlockSpec((q, D_INNER), lambda b, c: (row(b, c), 0)),
        out_shape=jax.ShapeDtypeStruct((batch * seq, D_INNER), BF16),
        scratch_shapes=[
            pltpu.VMEM((SSD_CONV_DIM // LANES, 2 * (q + HALO), LANES), F32),
            pltpu.VMEM((q, SSD_CONV_DIM), F32),
            pltpu.VMEM((SSD_GROUPS, D_STATE, SSD_GROUP_WIDTH), F32),
        ],
        compiler_params=pltpu.CompilerParams(
            dimension_semantics=("arbitrary", "arbitrary"), vmem_limit_bytes=VMEM_LIMIT),
        name="ssd",
    )(proj, proj, dt_raw, conv_w, conv_b, dt_bias, a_log, d_skip_exp, norm_g, tri, expand)


def _attn_kernel(q_ref, kc_ref, kp_ref, vc_ref, vp_ref, o_ref, lse_ref, stage, q_de, k_de, v_de,
                 *, dil, n_back):
    blk = ATTN_BLOCK
    per_res = ATTN_SLAB_BLOCKS // dil
    prev_rows = blk * dil
    first_slab = pl.program_id(1) == 0

    def deinterleave(cur_ref, prev_ref, dst, run):
        base = 0
        for c in range(ATTN_LANE_CHUNKS):
            cs = slice(c * LANES, (c + 1) * LANES)
            if prev_ref is not None:
                stage[c, 0:prev_rows, :] = prev_ref[:, cs].astype(F32)
                base = prev_rows
            stage[c, base:base + ATTN_SLAB, :] = cur_ref[:, cs].astype(F32)

        def body(bi, carry):
            start = bi // run + dil * blk * (bi % run)
            for c in range(ATTN_LANE_CHUNKS):
                dst[bi, :, c * LANES:(c + 1) * LANES] = (
                    stage[c, pl.ds(start, blk, stride=dil), :].astype(BF16))
            return carry

        lax.fori_loop(0, dil * run, body, 0)

    deinterleave(kc_ref, kp_ref, k_de, per_res + 1)
    deinterleave(vc_ref, vp_ref, v_de, per_res + 1)
    deinterleave(q_ref, None, q_de, per_res)

    lane = lax.broadcasted_iota(jnp.int32, (blk, LANES), 1)
    even = lane < ATTN_HEAD_DIM
    qi = lax.broadcasted_iota(jnp.int32, (blk, 2 * blk), 0)
    ki = lax.broadcasted_iota(jnp.int32, (blk, 2 * blk), 1)
    dist = blk + qi - ki
    band = (dist >= 0) & (dist <= n_back)

    def attend(bi, carry):
        res = bi // per_res
        j = bi % per_res
        key_lo = jnp.where(jnp.logical_and(first_slab, j == 0), blk, 0)
        mask = band & (ki >= key_lo)
        kv = res * (per_res + 1) + j
        start = res + dil * blk * j
        lse_tile = jnp.zeros((blk, LANES), F32)
        for hp in range(ATTN_LANE_CHUNKS):
            ps = slice(hp * LANES, (hp + 1) * LANES)
            q_pair = q_de[bi, :, ps]
            k2 = jnp.concatenate([k_de[kv, :, ps], k_de[kv + 1, :, ps]], axis=0)
            v2 = jnp.concatenate([v_de[kv, :, ps], v_de[kv + 1, :, ps]], axis=0)
            halves = []
            for e, sel in enumerate((even, jnp.logical_not(even))):
                qm = jnp.where(sel, q_pair, jnp.zeros_like(q_pair))
                s = lax.dot_general(qm, k2, (((1,), (1,)), ((), ())), preferred_element_type=F32)
                s = jnp.where(mask, s, -jnp.inf)
                m = jnp.max(s, axis=-1, keepdims=True)
                p = jnp.exp(s - m)
                den = jnp.sum(p, axis=-1, keepdims=True)
                halves.append(jnp.dot(p.astype(BF16), v2, preferred_element_type=F32) * (1.0 / den))
                lse_tile = jnp.where(lane == 2 * hp + e, m + jnp.log(den), lse_tile)
            stage[hp, pl.ds(start, blk, stride=dil), :] = jnp.where(even, halves[0], halves[1])
        lse_ref[pl.ds(start, blk, stride=dil), :] = lse_tile
        return carry

    lax.fori_loop(0, ATTN_SLAB_BLOCKS, attend, 0)
    for c in range(ATTN_LANE_CHUNKS):
        o_ref[:, c * LANES:(c + 1) * LANES] = stage[c, 0:ATTN_SLAB, :].astype(o_ref.dtype)


def _attn_group(qkv, gi, batch, seq):
    dil = ATTN_DILATIONS[gi]
    n_back = ATTN_WINDOWS[gi] // dil
    assert n_back <= ATTN_BLOCK and ATTN_SLAB_BLOCKS % dil == 0 and seq % ATTN_SLAB == 0
    slabs = seq // ATTN_SLAB
    prev_rows = ATTN_BLOCK * dil
    prev_per_slab = ATTN_SLAB // prev_rows

    def cur(which):
        return pl.BlockSpec((ATTN_SLAB, ATTN_GROUP_WIDTH), lambda b, s: (b * slabs + s, which * ATTN_GROUPS + gi))

    def prev(which):
        return pl.BlockSpec(
            (prev_rows, ATTN_GROUP_WIDTH),
            lambda b, s: (jnp.maximum((b * slabs + s) * prev_per_slab - 1, 0), which * ATTN_GROUPS + gi))

    run = ATTN_SLAB_BLOCKS // dil + 1
    return pl.pallas_call(
        functools.partial(_attn_kernel, dil=dil, n_back=n_back),
        grid=(batch, slabs),
        in_specs=[cur(0), cur(1), prev(1), cur(2), prev(2)],
        out_specs=[pl.BlockSpec((ATTN_SLAB, ATTN_GROUP_WIDTH), lambda b, s: (b * slabs + s, 0)),
                   pl.BlockSpec((ATTN_SLAB, LANES), lambda b, s: (b * slabs + s, 0))],
        out_shape=[jax.ShapeDtypeStruct((batch * seq, ATTN_GROUP_WIDTH), BF16),
                   jax.ShapeDtypeStruct((batch * seq, LANES), F32)],
        scratch_shapes=[
            pltpu.VMEM((ATTN_LANE_CHUNKS, prev_rows + ATTN_SLAB, LANES), F32),
            pltpu.VMEM((ATTN_SLAB_BLOCKS, ATTN_BLOCK, ATTN_GROUP_WIDTH), BF16),
            pltpu.VMEM((dil * run, ATTN_BLOCK, ATTN_GROUP_WIDTH), BF16),
            pltpu.VMEM((dil * run, ATTN_BLOCK, ATTN_GROUP_WIDTH), BF16),
        ],
        compiler_params=pltpu.CompilerParams(
            dimension_semantics=("arbitrary", "arbitrary"), vmem_limit_bytes=VMEM_LIMIT),
        name=f"attn_g{gi}",
    )(qkv, qkv, qkv, qkv, qkv)


def _expand_heads(w, hexp):
    return _dot_split(_split_bf16(w, 2), hexp)


def _merge_kernel(x_ref, yssd_ref, o0_ref, o1_ref, o2_ref, l0_ref, l1_ref, l2_ref, gssd_ref, gattn_ref,
                  wssd_ref, wattn_ref, wout_ref, hexp_ref, g2_ref, x1_ref, h2_ref):
    l0, l1, l2 = l0_ref[...], l1_ref[...], l2_ref[...]
    mx = jnp.maximum(jnp.maximum(l0, l1), l2)
    e0, e1, e2 = jnp.exp(l0 - mx), jnp.exp(l1 - mx), jnp.exp(l2 - mx)
    inv = 1.0 / (e0 + e1 + e2)
    y_attn = jnp.zeros(o0_ref.shape, F32)
    for e, o_ref in ((e0, o0_ref), (e1, o1_ref), (e2, o2_ref)):
        y_attn = y_attn + _expand_heads(e * inv, hexp_ref[...]) * o_ref[...].astype(F32)
    p_ssd = jnp.dot(yssd_ref[...], wssd_ref[...], preferred_element_type=F32)
    p_attn = jnp.dot(y_attn.astype(BF16), wattn_ref[...], preferred_element_type=F32)
    merged = (_sigmoid(gssd_ref[...].astype(F32)) * p_ssd
              + _sigmoid(gattn_ref[...].astype(F32)) * p_attn)
    x1 = x_ref[...] + jnp.dot(merged.astype(BF16), wout_ref[...], preferred_element_type=F32)
    x1_ref[...] = x1
    h2_ref[...] = _rms_rows(x1, g2_ref[...]).astype(h2_ref.dtype)


def _merge(x2, y_ssd, outs, lses, proj, w_ssd, w_attn, w_out, head_expand, g2, tm):
    m = x2.shape[0]
    rows = lambda width, cb=0: pl.BlockSpec((tm, width), lambda i: (i, cb))
    const = lambda shape: pl.BlockSpec(shape, lambda i: (0, 0))
    return pl.pallas_call(
        _merge_kernel,
        grid=(m // tm,),
        in_specs=[rows(D_MODEL), rows(D_INNER),
                  rows(ATTN_GROUP_WIDTH), rows(ATTN_GROUP_WIDTH), rows(ATTN_GROUP_WIDTH),
                  rows(LANES), rows(LANES), rows(LANES),
                  rows(D_MODEL, COL_GSSD // D_MODEL), rows(D_MODEL, COL_GATTN // D_MODEL),
                  const((D_INNER, D_MODEL)), const((ATTN_GROUP_WIDTH, D_MODEL)), const((D_MODEL, D_MODEL)),
                  const((LANES, ATTN_GROUP_WIDTH)), const((1, D_MODEL))],
        out_specs=[rows(D_MODEL), rows(D_MODEL)],
        out_shape=[jax.ShapeDtypeStruct((m, D_MODEL), F32), jax.ShapeDtypeStruct((m, D_MODEL), BF16)],
        compiler_params=pltpu.CompilerParams(
            dimension_semantics=("arbitrary",), vmem_limit_bytes=VMEM_LIMIT),
        name="merge",
    )(x2, y_ssd, outs[0], outs[1], outs[2], lses[0], lses[1], lses[2], proj, proj,
      w_ssd, w_attn, w_out, head_expand, g2)


def _ffn_kernel(h_ref, x1_ref, wg_ref, wv_ref, cwg_ref, cwv_ref, cbg_ref, cbv_ref, wd_ref, out_ref,
                ug_scr, uv_scr, halo_g, halo_v, acc_scr, *, tiles_per_seq):
    i = pl.program_id(0)
    f = pl.program_id(1)
    tm = h_ref.shape[0]
    first = (i % tiles_per_seq) == 0

    def conv(u_scr, halo, w_ref, b_ref, wmat_ref):
        u_scr[HALO:HALO + tm, :] = jnp.dot(h_ref[...], wmat_ref[...], preferred_element_type=F32)
        u_scr[0:HALO, :] = jnp.where(first, 0.0, halo[f])
        acc = jnp.broadcast_to(b_ref[...], (tm, u_scr.shape[1]))
        for t in range(FFN_CONV):
            acc = acc + u_scr[pl.ds(HALO - (FFN_CONV - 1) + t, tm), :] * w_ref[t:t + 1, :]
        halo[f] = u_scr[tm:tm + HALO, :]
        return acc

    gate = conv(ug_scr, halo_g, cwg_ref, cbg_ref, wg_ref)
    val = conv(uv_scr, halo_v, cwv_ref, cbv_ref, wv_ref)
    part = jnp.dot((_silu(gate) * val).astype(BF16), wd_ref[...], preferred_element_type=F32)

    @pl.when(f == 0)
    def _():
        acc_scr[...] = x1_ref[...] + part

    @pl.when(f > 0)
    def _():
        acc_scr[...] += part

    @pl.when(f == pl.num_programs(1) - 1)
    def _():
        out_ref[...] = acc_scr[...]


def _ffn(h2, x1, w_up, conv_w, conv_b, w_down, tm, tf, seq):
    m = h2.shape[0]
    nf = D_FF // tf
    return pl.pallas_call(
        functools.partial(_ffn_kernel, tiles_per_seq=seq // tm),
        grid=(m // tm, nf),
        in_specs=[
            pl.BlockSpec((tm, D_MODEL), lambda i, f: (i, 0)),
            pl.BlockSpec((tm, D_MODEL), lambda i, f: (i, 0)),
            pl.BlockSpec((D_MODEL, tf), lambda i, f: (0, f)),
            pl.BlockSpec((D_MODEL, tf), lambda i, f: (0, nf + f)),
            pl.BlockSpec((FFN_CONV, tf), lambda i, f: (0, f)),
            pl.BlockSpec((FFN_CONV, tf), lambda i, f: (0, nf + f)),
            pl.BlockSpec((1, tf), lambda i, f: (0, f)),
            pl.BlockSpec((1, tf), lambda i, f: (0, nf + f)),
            pl.BlockSpec((tf, D_MODEL), lambda i, f: (f, 0)),
        ],
        out_specs=pl.BlockSpec((tm, D_MODEL), lambda i, f: (i, 0)),
        out_shape=jax.ShapeDtypeStruct((m, D_MODEL), F32),
        scratch_shapes=[
            pltpu.VMEM((tm + HALO, tf), F32),
            pltpu.VMEM((tm + HALO, tf), F32),
            pltpu.VMEM((nf, HALO, tf), F32),
            pltpu.VMEM((nf, HALO, tf), F32),
            pltpu.VMEM((tm, D_MODEL), F32),
        ],
        compiler_params=pltpu.CompilerParams(
            dimension_semantics=("arbitrary", "arbitrary"), vmem_limit_bytes=VMEM_LIMIT),
        name="ffn",
    )(h2, x1, w_up, w_up, conv_w, conv_w, conv_b, conv_b, w_down)


def _pad_lanes(v):
    return jnp.pad(v.astype(F32), (0, LANES - v.shape[0])).reshape(1, LANES)


def kernel(x, norm1_g, w_in, ssd_conv_w, ssd_conv_b, dt_bias, a_log, d_skip, ssd_norm_g, w_ssd_proj,
           q_norm_g, k_norm_g, w_attn_proj, w_out, norm2_g, w_up, ffn_conv_w, ffn_conv_b, w_down):
    batch, seq, _ = x.shape
    assert norm1_g.shape[0] == 1, "single-layer block"
    x2 = x.reshape(batch * seq, D_MODEL)
    g1 = norm1_g[0].reshape(1, D_MODEL)

    w = w_in[0]
    o_xbc = D_INNER
    o_dt = o_xbc + SSD_CONV_DIM
    o_q = o_dt + SSD_HEADS
    o_g = o_q + 3 * ATTN_WIDTH
    w_main = jnp.concatenate([w[:, o_xbc:o_dt], w[:, :o_xbc], w[:, o_g:]], axis=1).astype(BF16)
    w_dt = jnp.pad(w[:, o_dt:o_q], ((0, 0), (0, LANES - SSD_HEADS))).astype(BF16)
    w_qkv = w[:, o_q:o_g].astype(BF16)

    proj, dt_raw = _in_proj_main(x2, g1, w_main, w_dt, tm=1024, tn=2048)

    heads_per_tile = MXU_WIDTH // ATTN_HEAD_DIM
    qk_gains = jnp.stack([jnp.tile(q_norm_g[0] * (ATTN_HEAD_DIM ** -0.5), heads_per_tile),
                          jnp.tile(k_norm_g[0], heads_per_tile)]).astype(F32)
    lane_head = jnp.arange(MXU_WIDTH) // ATTN_HEAD_DIM
    seg = (lane_head[:, None] == lane_head[None, :]).astype(BF16)
    qkv = _in_proj_qkv(x2, g1, w_qkv, qk_gains, seg, tm=1024)

    tri = jnp.tril(jnp.ones((SSD_CHUNK, SSD_CHUNK), BF16))
    head_of_col = jnp.arange(D_INNER) // SSD_HEAD_DIM
    expand = (jnp.arange(LANES)[:, None] == head_of_col[None, :]).astype(BF16)
    y_ssd = _ssd(proj, dt_raw, ssd_conv_w[0], ssd_conv_b[0].reshape(1, -1), _pad_lanes(dt_bias[0]),
                 _pad_lanes(a_log[0]), jnp.repeat(d_skip[0], SSD_HEAD_DIM).reshape(1, D_INNER),
                 ssd_norm_g[0].reshape(1, D_INNER), tri, expand, batch, seq)

    outs, lses = zip(*[_attn_group(qkv, gi, batch, seq) for gi in range(ATTN_GROUPS)])

    attn_head_of_col = jnp.arange(ATTN_GROUP_WIDTH) // ATTN_HEAD_DIM
    head_expand = (jnp.arange(LANES)[:, None] == attn_head_of_col[None, :]).astype(BF16)
    x1, h2 = _merge(x2, y_ssd, outs, lses, proj, w_ssd_proj[0].astype(BF16), w_attn_proj[0].astype(BF16),
                    w_out[0].astype(BF16), head_expand, norm2_g[0].reshape(1, D_MODEL), tm=512)

    out = _ffn(h2, x1, w_up[0].astype(BF16), ffn_conv_w[0], ffn_conv_b[0].reshape(1, -1),
               w_down[0].astype(BF16), tm=512, tf=1408, seq=seq)
    return out.reshape(batch, seq, D_MODEL)
```

```python
import functools

import jax
import jax.numpy as jnp
from jax import lax
from jax.experimental import pallas as pl
from jax.experimental.pallas import tpu as pltpu

F32 = jnp.float32
BF16 = jnp.bfloat16

D_MODEL = 1024
D_INNER = 2048
SSD_HEADS = 32
SSD_HEAD_DIM = 64
SSD_GROUPS = 8
SSD_HEADS_PER_GROUP = SSD_HEADS // SSD_GROUPS
D_STATE = 128
SSD_CONV = 4
SSD_CHUNK = 128
SSD_CONV_DIM = D_INNER + 2 * SSD_GROUPS * D_STATE
SSD_GROUP_WIDTH = D_INNER // SSD_GROUPS
ATTN_WINDOWS = (128, 512, 2048)
ATTN_DILATIONS = (1, 4, 16)
ATTN_GROUPS = 3
ATTN_HEADS = 8
ATTN_HEAD_DIM = 64
ATTN_BLOCK = 128
ATTN_GROUP_WIDTH = ATTN_HEADS * ATTN_HEAD_DIM
ATTN_WIDTH = ATTN_GROUPS * ATTN_GROUP_WIDTH
D_FF = 2816
FFN_CONV = 3
EPS = 1e-6
LANES = 128
MXU_WIDTH = 256
HALO = 8
ATTN_SLAB = ATTN_BLOCK * max(ATTN_DILATIONS)
ATTN_SLAB_BLOCKS = ATTN_SLAB // ATTN_BLOCK
ATTN_LANE_CHUNKS = ATTN_GROUP_WIDTH // LANES
ATTN_UNROLL = 4

COL_XBC = 0
COL_Z = COL_XBC + SSD_CONV_DIM
COL_GSSD = COL_Z + D_INNER
COL_GATTN = COL_GSSD + D_MODEL
PROJ_WIDTH = COL_GATTN + D_MODEL

VMEM_LIMIT = 56 * 1024 * 1024


def _sigmoid(v):
    return 0.5 * jnp.tanh(0.5 * v) + 0.5


def _silu_of_half(hv):
    return hv + hv * jnp.tanh(hv)


def _split_bf16(v, terms):
    parts = []
    for _ in range(terms):
        p = v.astype(BF16)
        parts.append(p)
        v = v - p.astype(F32)
    return parts


def _dot_split(parts, rhs, lhs_side=True):
    acc = None
    for p in parts:
        t = (jnp.dot(p, rhs, preferred_element_type=F32) if lhs_side
             else jnp.dot(rhs, p, preferred_element_type=F32))
        acc = t if acc is None else acc + t
    return acc


def _rms_rows(x, gain):
    ms = jnp.mean(x * x, axis=-1, keepdims=True)
    return x * lax.rsqrt(ms + EPS) * gain


def _in_proj_main_kernel(x_ref, g_ref, w_ref, wdt_ref, cw_ref, cb_ref, proj_ref, dt_ref,
                         h_scr, xp_scr, halo_scr, *, tiles_per_seq):
    i = pl.program_id(0)
    j = pl.program_id(1)
    tm, tn = proj_ref.shape
    conv_tiles = SSD_CONV_DIM // tn
    z_tiles = D_INNER // tn

    @pl.when(j == 0)
    def _():
        h = _rms_rows(x_ref[...], g_ref[...]).astype(BF16)
        h_scr[...] = h
        dt_ref[...] = jnp.dot(h, wdt_ref[...], preferred_element_type=F32)

    @pl.when(jnp.logical_and(i == 0, j == 0))
    def _():
        halo_scr[...] = jnp.zeros(halo_scr.shape, F32)

    @pl.when(j < conv_tiles)
    def _():
        r = jnp.dot(h_scr[...], w_ref[...], preferred_element_type=F32)
        first = (i % tiles_per_seq) == 0
        for c in range(tn // LANES):
            cs = slice(c * LANES, (c + 1) * LANES)
            xp_scr[c, 0:HALO, :] = jnp.where(first, 0.0, halo_scr[j, c])
            xp_scr[c, HALO:HALO + tm, :] = r[:, cs]
            acc = 0.5 * cb_ref[:, cs]
            for t in range(SSD_CONV):
                tap = xp_scr[c, pl.ds(HALO - (SSD_CONV - 1) + t, tm, stride=1), :]
                acc = acc + tap * (0.5 * cw_ref[t:t + 1, cs])
            proj_ref[:, cs] = _silu_of_half(acc).astype(proj_ref.dtype)
            halo_scr[j, c] = xp_scr[c, tm:tm + HALO, :]

    @pl.when(jnp.logical_and(j >= conv_tiles, j < conv_tiles + z_tiles))
    def _():
        r = jnp.dot(h_scr[...], w_ref[...], preferred_element_type=F32)
        proj_ref[...] = _silu_of_half(0.5 * r).astype(proj_ref.dtype)

    @pl.when(j >= conv_tiles + z_tiles)
    def _():
        proj_ref[...] = jnp.dot(h_scr[...], w_ref[...], preferred_element_type=F32).astype(proj_ref.dtype)


def _in_proj_main(x2, g1, w_main, w_dt, conv_w, conv_b, tm, tn, seq):
    m = x2.shape[0]
    assert SSD_CONV_DIM % tn == 0 and D_INNER % tn == 0 and seq % tm == 0
    conv_tiles = SSD_CONV_DIM // tn
    conv_col = lambda i, j: (0, jnp.minimum(j, conv_tiles - 1))
    return pl.pallas_call(
        functools.partial(_in_proj_main_kernel, tiles_per_seq=seq // tm),
        grid=(m // tm, PROJ_WIDTH // tn),
        in_specs=[
            pl.BlockSpec((tm, D_MODEL), lambda i, j: (i, 0)),
            pl.BlockSpec((1, D_MODEL), lambda i, j: (0, 0)),
            pl.BlockSpec((D_MODEL, tn), lambda i, j: (0, j)),
            pl.BlockSpec((D_MODEL, LANES), lambda i, j: (0, 0)),
            pl.BlockSpec((SSD_CONV, tn), conv_col),
            pl.BlockSpec((1, tn), conv_col),
        ],
        out_specs=[
            pl.BlockSpec((tm, tn), lambda i, j: (i, j)),
            pl.BlockSpec((tm, LANES), lambda i, j: (i, 0)),
        ],
        out_shape=[
            jax.ShapeDtypeStruct((m, PROJ_WIDTH), BF16),
            jax.ShapeDtypeStruct((m, LANES), F32),
        ],
        scratch_shapes=[
            pltpu.VMEM((tm, D_MODEL), BF16),
            pltpu.VMEM((tn // LANES, HALO + tm, LANES), F32),
            pltpu.VMEM((conv_tiles, tn // LANES, HALO, LANES), F32),
        ],
        compiler_params=pltpu.CompilerParams(
            dimension_semantics=("arbitrary", "arbitrary"), vmem_limit_bytes=VMEM_LIMIT),
        name="in_proj_main",
    )(x2, g1, w_main, w_dt, conv_w, conv_b)


def _in_proj_qkv_kernel(x_ref, g_ref, w_ref, gain_ref, seg_ref, out_ref, h_scr, t_scr):
    j = pl.program_id(1)
    seg_w = seg_ref.shape[0]

    @pl.when(j == 0)
    def _():
        h_scr[...] = _rms_rows(x_ref[...], g_ref[...]).astype(BF16)

    @pl.when(j < 2)
    def _():
        t_scr[...] = jnp.dot(h_scr[...], w_ref[...], preferred_element_type=F32)
        gain = gain_ref[pl.ds(j, 1), :]
        for c in range(ATTN_WIDTH // seg_w):
            cs = slice(c * seg_w, (c + 1) * seg_w)
            t = t_scr[:, cs]
            ss = jnp.dot((t * t).astype(BF16), seg_ref[...], preferred_element_type=F32)
            out_ref[:, cs] = (t * lax.rsqrt(ss * (1.0 / ATTN_HEAD_DIM) + EPS) * gain).astype(out_ref.dtype)

    @pl.when(j == 2)
    def _():
        out_ref[...] = jnp.dot(h_scr[...], w_ref[...], preferred_element_type=F32).astype(out_ref.dtype)


def _in_proj_qkv(x2, g1, w_qkv, gains, seg, tm):
    m = x2.shape[0]
    return pl.pallas_call(
        _in_proj_qkv_kernel,
        grid=(m // tm, 3),
        in_specs=[
            pl.BlockSpec((tm, D_MODEL), lambda i, j: (i, 0)),
            pl.BlockSpec((1, D_MODEL), lambda i, j: (0, 0)),
            pl.BlockSpec((D_MODEL, ATTN_WIDTH), lambda i, j: (0, j)),
            pl.BlockSpec(gains.shape, lambda i, j: (0, 0)),
            pl.BlockSpec(seg.shape, lambda i, j: (0, 0)),
        ],
        out_specs=pl.BlockSpec((tm, ATTN_WIDTH), lambda i, j: (i, j)),
        out_shape=jax.ShapeDtypeStruct((m, 3 * ATTN_WIDTH), BF16),
        scratch_shapes=[pltpu.VMEM((tm, D_MODEL), BF16), pltpu.VMEM((tm, ATTN_WIDTH), F32)],
        compiler_params=pltpu.CompilerParams(
            dimension_semantics=("arbitrary", "arbitrary"), vmem_limit_bytes=VMEM_LIMIT),
        name="in_proj_qkv",
    )(x2, g1, w_qkv, gains, seg)


def _ssd_chunk(act, gz, dt_raw, dtb, a_log, dskip, ng, tri, expand2, y_ref, rows, state_scr):
    q = SSD_CHUNK
    dt = jax.nn.softplus(dt_raw + dtb)
    a = dt * (-jnp.exp(a_log))
    a_cs = _dot_split(_split_bf16(a, 3), tri, lhs_side=False)
    a_cs_t_adj = a_cs.T - jnp.log(dt.T)
    e_in = jnp.exp(a_cs)
    w_out = jnp.exp(a_cs[q - 1:q, :] - a_cs) * dt
    lhs = jnp.concatenate([jnp.concatenate(_split_bf16(e_in, 2), axis=1),
                           jnp.concatenate(_split_bf16(w_out, 2), axis=1)], axis=0)
    both = jnp.dot(lhs, expand2, preferred_element_type=F32)
    e_exp, w_exp = both[:q], both[q:]

    li = lax.broadcasted_iota(jnp.int32, (q, q), 0)
    si = lax.broadcasted_iota(jnp.int32, (q, q), 1)
    causal = li >= si
    head_of_lane = lax.broadcasted_iota(jnp.int32, (q, SSD_GROUP_WIDTH), 1) // SSD_HEAD_DIM

    for g in range(SSD_GROUPS):
        gs = slice(g * SSD_GROUP_WIDTH, (g + 1) * SSD_GROUP_WIDTH)
        xs_b = act[:, gs]
        xs = xs_b.astype(F32)
        b_g = act[:, D_INNER + g * D_STATE:D_INNER + (g + 1) * D_STATE]
        c_g = act[:, D_INNER + (SSD_GROUPS + g) * D_STATE:D_INNER + (SSD_GROUPS + g + 1) * D_STATE]
        cb = lax.dot_general(c_g, b_g, (((1,), (1,)), ((), ())), preferred_element_type=F32)
        st = state_scr[g]
        y_off = jnp.dot(c_g, st.astype(BF16), preferred_element_type=F32) * e_exp[:, gs]
        mms, blocks = [], []
        for k in range(SSD_HEADS_PER_GROUP):
            h = g * SSD_HEADS_PER_GROUP + k
            seg = a_cs[:, h:h + 1] - a_cs_t_adj[h:h + 1, :]
            mms.append((cb * jnp.exp(jnp.where(causal, seg, -jnp.inf))).astype(BF16))
            blocks.append(jnp.where(head_of_lane == k, xs_b, jnp.zeros_like(xs_b)))
        y_diag = jnp.dot(jnp.concatenate(mms, axis=1), jnp.concatenate(blocks, axis=0),
                         preferred_element_type=F32)
        y = y_diag + y_off + dskip[:, gs] * xs
        y = y * gz[:, gs].astype(F32)
        ms = jnp.mean(y * y, axis=-1, keepdims=True)
        y_ref[rows, gs] = (y * lax.rsqrt(ms + EPS) * ng[:, gs]).astype(y_ref.dtype)
        xw = (xs * w_exp[:, gs]).astype(BF16)
        upd = lax.dot_general(b_g, xw, (((0,), (0,)), ((), ())), preferred_element_type=F32)
        state_scr[g] = st * e_exp[q - 1:q, gs] + upd


def _ssd_kernel(act_ref, gz_ref, dt_ref, dtb_ref, alog_ref, dskip_ref, ng_ref, tri_ref, expand2_ref,
                y_ref, state_scr):
    @pl.when(pl.program_id(1) == 0)
    def _():
        state_scr[...] = jnp.zeros(state_scr.shape, F32)

    for ci in range(act_ref.shape[0] // SSD_CHUNK):
        rows = slice(ci * SSD_CHUNK, (ci + 1) * SSD_CHUNK)
        _ssd_chunk(act_ref[rows, :], gz_ref[rows, :], dt_ref[rows, :], dtb_ref[...], alog_ref[...],
                   dskip_ref[...], ng_ref[...], tri_ref[...], expand2_ref[...], y_ref, rows, state_scr)


def _ssd(proj, dt_raw, dt_bias, a_log, d_skip_exp, norm_g, tri, expand2, batch, seq, chunks_per_step):
    rows = SSD_CHUNK * chunks_per_step
    steps = seq // rows
    row = lambda b, c: b * steps + c
    const = lambda b, c: (0, 0)
    return pl.pallas_call(
        _ssd_kernel,
        grid=(batch, steps),
        in_specs=[
            pl.BlockSpec((rows, SSD_CONV_DIM), lambda b, c: (row(b, c), COL_XBC // SSD_CONV_DIM)),
            pl.BlockSpec((rows, D_INNER), lambda b, c: (row(b, c), COL_Z // D_INNER)),
            pl.BlockSpec((rows, LANES), lambda b, c: (row(b, c), 0)),
            pl.BlockSpec((1, LANES), const),
            pl.BlockSpec((1, LANES), const),
            pl.BlockSpec((1, D_INNER), const),
            pl.BlockSpec((1, D_INNER), const),
            pl.BlockSpec((SSD_CHUNK, SSD_CHUNK), const),
            pl.BlockSpec((2 * LANES, D_INNER), const),
        ],
        out_specs=pl.BlockSpec((rows, D_INNER), lambda b, c: (row(b, c), 0)),
        out_shape=jax.ShapeDtypeStruct((batch * seq, D_INNER), BF16),
        scratch_shapes=[pltpu.VMEM((SSD_GROUPS, D_STATE, SSD_GROUP_WIDTH), F32)],
        compiler_params=pltpu.CompilerParams(
            dimension_semantics=("arbitrary", "arbitrary"), vmem_limit_bytes=VMEM_LIMIT),
        name="ssd",
    )(proj, proj, dt_raw, dt_bias, a_log, d_skip_exp, norm_g, tri, expand2)


def _attn_kernel(q_ref, k_ref, v_ref, o_ref, lse_ref, stage, q_de, k_de, v_de, *, dil, n_back):
    blk = ATTN_BLOCK
    per_res = ATTN_SLAB_BLOCKS // dil
    run = per_res + 1
    first_slab = pl.program_id(1) == 0

    def keep_previous(dst):
        @pl.when(first_slab)
        def _():
            def body(res, carry):
                dst[res * run] = jnp.zeros(dst.shape[1:], BF16)
                return carry
            lax.fori_loop(0, dil, body, 0)

        @pl.when(jnp.logical_not(first_slab))
        def _():
            def body(res, carry):
                dst[res * run] = dst[res * run + per_res]
                return carry
            lax.fori_loop(0, dil, body, 0)

    def deinterleave(src_ref, dst, skip):
        for c in range(ATTN_LANE_CHUNKS):
            stage[c] = src_ref[:, c * LANES:(c + 1) * LANES].astype(F32)

        def body(bi, carry):
            res = bi // per_res
            start = res + dil * blk * (bi % per_res)
            for c in range(ATTN_LANE_CHUNKS):
                dst[bi + skip * (res + 1), c] = stage[c, pl.ds(start, blk, stride=dil), :].astype(BF16)
            return carry

        lax.fori_loop(0, ATTN_SLAB_BLOCKS, body, 0)

    keep_previous(k_de)
    keep_previous(v_de)
    deinterleave(k_ref, k_de, 1)
    deinterleave(v_ref, v_de, 1)
    deinterleave(q_ref, q_de, 0)

    lane = lax.broadcasted_iota(jnp.int32, (blk, LANES), 1)
    even = lane < ATTN_HEAD_DIM
    qi = lax.broadcasted_iota(jnp.int32, (blk, 2 * blk), 0)
    ki = lax.broadcasted_iota(jnp.int32, (blk, 2 * blk), 1)
    dist = blk + qi - ki
    band = (dist >= 0) & (dist <= n_back)

    def attend(bi, carry):
        res = bi // per_res
        j = bi % per_res
        key_lo = jnp.where(jnp.logical_and(first_slab, j == 0), blk, 0)
        mask = band & (ki >= key_lo)
        kv = res * run + j
        start = res + dil * blk * j
        lse_tile = jnp.zeros((blk, LANES), F32)
        for hp in range(ATTN_LANE_CHUNKS):
            q_pair = q_de[bi, hp]
            k2 = jnp.concatenate([k_de[kv, hp], k_de[kv + 1, hp]], axis=0)
            v2 = jnp.concatenate([v_de[kv, hp], v_de[kv + 1, hp]], axis=0)
            halves = []
            for e, sel in enumerate((even, jnp.logical_not(even))):
                qm = jnp.where(sel, q_pair, jnp.zeros_like(q_pair))
                s = lax.dot_general(qm, k2, (((1,), (1,)), ((), ())), preferred_element_type=F32)
                s = jnp.where(mask, s, -jnp.inf)
                m = jnp.max(s, axis=-1, keepdims=True)
                p = jnp.exp(s - m)
                den = jnp.sum(p, axis=-1, keepdims=True)
                halves.append(jnp.dot(p.astype(BF16), v2, preferred_element_type=F32) * (1.0 / den))
                lse_tile = jnp.where(lane == 2 * hp + e, m + jnp.log(den), lse_tile)
            stage[hp, pl.ds(start, blk, stride=dil), :] = jnp.where(even, halves[0], halves[1])
        lse_ref[pl.ds(start, blk, stride=dil), :] = lse_tile
        return carry

    lax.fori_loop(0, ATTN_SLAB_BLOCKS, attend, 0, unroll=ATTN_UNROLL)
    for c in range(ATTN_LANE_CHUNKS):
        o_ref[:, c * LANES:(c + 1) * LANES] = stage[c].astype(o_ref.dtype)


def _attn_group(qkv, gi, batch, seq):
    dil = ATTN_DILATIONS[gi]
    n_back = ATTN_WINDOWS[gi] // dil
    assert n_back <= ATTN_BLOCK and ATTN_SLAB_BLOCKS % dil == 0 and seq % ATTN_SLAB == 0
    slabs = seq // ATTN_SLAB

    def cur(which):
        return pl.BlockSpec((ATTN_SLAB, ATTN_GROUP_WIDTH), lambda b, s: (b * slabs + s, which * ATTN_GROUPS + gi))

    run = ATTN_SLAB_BLOCKS // dil + 1
    de_block = (ATTN_LANE_CHUNKS, ATTN_BLOCK, LANES)
    return pl.pallas_call(
        functools.partial(_attn_kernel, dil=dil, n_back=n_back),
        grid=(batch, slabs),
        in_specs=[cur(0), cur(1), cur(2)],
        out_specs=[pl.BlockSpec((ATTN_SLAB, ATTN_GROUP_WIDTH), lambda b, s: (b * slabs + s, 0)),
                   pl.BlockSpec((ATTN_SLAB, LANES), lambda b, s: (b * slabs + s, 0))],
        out_shape=[jax.ShapeDtypeStruct((batch * seq, ATTN_GROUP_WIDTH), BF16),
                   jax.ShapeDtypeStruct((batch * seq, LANES), F32)],
        scratch_shapes=[
            pltpu.VMEM((ATTN_LANE_CHUNKS, ATTN_SLAB, LANES), F32),
            pltpu.VMEM((ATTN_SLAB_BLOCKS,) + de_block, BF16),
            pltpu.VMEM((dil * run,) + de_block, BF16),
            pltpu.VMEM((dil * run,) + de_block, BF16),
        ],
        compiler_params=pltpu.CompilerParams(
            dimension_semantics=("arbitrary", "arbitrary"), vmem_limit_bytes=VMEM_LIMIT),
        name=f"attn_g{gi}",
    )(qkv, qkv, qkv)


def _expand_heads(w, hexp):
    return _dot_split(_split_bf16(w, 2), hexp)


def _merge_kernel(x_ref, yssd_ref, o0_ref, o1_ref, o2_ref, l0_ref, l1_ref, l2_ref, gssd_ref, gattn_ref,
                  wssd_ref, wattn_ref, wout_ref, hexp_ref, g2_ref, x1_ref, h2_ref):
    l0, l1, l2 = l0_ref[...], l1_ref[...], l2_ref[...]
    mx = jnp.maximum(jnp.maximum(l0, l1), l2)
    e0, e1, e2 = jnp.exp(l0 - mx), jnp.exp(l1 - mx), jnp.exp(l2 - mx)
    inv = 1.0 / (e0 + e1 + e2)
    y_attn = jnp.zeros(o0_ref.shape, F32)
    for e, o_ref in ((e0, o0_ref), (e1, o1_ref), (e2, o2_ref)):
        y_attn = y_attn + _expand_heads(e * inv, hexp_ref[...]) * o_ref[...].astype(F32)
    p_ssd = jnp.dot(yssd_ref[...], wssd_ref[...], preferred_element_type=F32)
    p_attn = jnp.dot(y_attn.astype(BF16), wattn_ref[...], preferred_element_type=F32)
    merged = (_sigmoid(gssd_ref[...].astype(F32)) * p_ssd
              + _sigmoid(gattn_ref[...].astype(F32)) * p_attn)
    x1 = x_ref[...] + jnp.dot(merged.astype(BF16), wout_ref[...], preferred_element_type=F32)
    x1_ref[...] = x1
    h2_ref[...] = _rms_rows(x1, g2_ref[...]).astype(h2_ref.dtype)


def _merge(x2, y_ssd, outs, lses, proj, w_ssd, w_attn, w_out, head_expand, g2, tm):
    m = x2.shape[0]
    rows = lambda width, cb=0: pl.BlockSpec((tm, width), lambda i: (i, cb))
    const = lambda shape: pl.BlockSpec(shape, lambda i: (0, 0))
    return pl.pallas_call(
        _merge_kernel,
        grid=(m // tm,),
        in_specs=[rows(D_MODEL), rows(D_INNER),
                  rows(ATTN_GROUP_WIDTH), rows(ATTN_GROUP_WIDTH), rows(ATTN_GROUP_WIDTH),
                  rows(LANES), rows(LANES), rows(LANES),
                  rows(D_MODEL, COL_GSSD // D_MODEL), rows(D_MODEL, COL_GATTN // D_MODEL),
                  const((D_INNER, D_MODEL)), const((ATTN_GROUP_WIDTH, D_MODEL)), const((D_MODEL, D_MODEL)),
                  const((LANES, ATTN_GROUP_WIDTH)), const((1, D_MODEL))],
        out_specs=[rows(D_MODEL), rows(D_MODEL)],
        out_shape=[jax.ShapeDtypeStruct((m, D_MODEL), F32), jax.ShapeDtypeStruct((m, D_MODEL), BF16)],
        compiler_params=pltpu.CompilerParams(
            dimension_semantics=("arbitrary",), vmem_limit_bytes=VMEM_LIMIT),
        name="merge",
    )(x2, y_ssd, outs[0], outs[1], outs[2], lses[0], lses[1], lses[2], proj, proj,
      w_ssd, w_attn, w_out, head_expand, g2)


def _ffn_kernel(h_ref, x1_ref, wg_ref, wv_ref, cwg_ref, cwv_ref, cbg_ref, cbv_ref, wd_ref, out_ref,
                ug_scr, uv_scr, halo_g, halo_v, act_scr, *, tiles_per_seq):
    i = pl.program_id(0)
    tm = h_ref.shape[0]
    tf = wg_ref.shape[1]
    first = (i % tiles_per_seq) == 0

    @pl.when(i == 0)
    def _():
        halo_g[...] = jnp.zeros(halo_g.shape, F32)
        halo_v[...] = jnp.zeros(halo_v.shape, F32)

    def conv(u, u_scr, halo, w_ref, b_ref, c, scale):
        cs = slice(c * LANES, (c + 1) * LANES)
        u_scr[c, 0:HALO, :] = jnp.where(first, 0.0, halo[c])
        u_scr[c, HALO:HALO + tm, :] = u[:, cs]
        acc = scale * b_ref[:, cs]
        for t in range(FFN_CONV):
            tap = u_scr[c, pl.ds(HALO - (FFN_CONV - 1) + t, tm, stride=1), :]
            acc = acc + tap * (scale * w_ref[t:t + 1, cs])
        halo[c] = u_scr[c, tm:tm + HALO, :]
        return acc

    ug = jnp.dot(h_ref[...], wg_ref[...], preferred_element_type=F32)
    uv = jnp.dot(h_ref[...], wv_ref[...], preferred_element_type=F32)
    for c in range(tf // LANES):
        gate = conv(ug, ug_scr, halo_g, cwg_ref, cbg_ref, c, 0.5)
        val = conv(uv, uv_scr, halo_v, cwv_ref, cbv_ref, c, 1.0)
        act_scr[:, c * LANES:(c + 1) * LANES] = (_silu_of_half(gate) * val).astype(BF16)
    out_ref[...] = x1_ref[...] + jnp.dot(act_scr[...], wd_ref[...], preferred_element_type=F32)


def _ffn(h2, x1, w_up, conv_w, conv_b, w_down, tm, seq):
    m = h2.shape[0]
    tf = D_FF
    once = dict(pipeline_mode=pl.Buffered(1))
    gate_half = lambda i: (0, 0)
    value_half = lambda i: (0, 1)
    return pl.pallas_call(
        functools.partial(_ffn_kernel, tiles_per_seq=seq // tm),
        grid=(m // tm,),
        in_specs=[
            pl.BlockSpec((tm, D_MODEL), lambda i: (i, 0)),
            pl.BlockSpec((tm, D_MODEL), lambda i: (i, 0)),
            pl.BlockSpec((D_MODEL, tf), gate_half, **once),
            pl.BlockSpec((D_MODEL, tf), value_half, **once),
            pl.BlockSpec((FFN_CONV, tf), gate_half, **once),
            pl.BlockSpec((FFN_CONV, tf), value_half, **once),
            pl.BlockSpec((1, tf), gate_half, **once),
            pl.BlockSpec((1, tf), value_half, **once),
            pl.BlockSpec((tf, D_MODEL), lambda i: (0, 0), **once),
        ],
        out_specs=pl.BlockSpec((tm, D_MODEL), lambda i: (i, 0)),
        out_shape=jax.ShapeDtypeStruct((m, D_MODEL), F32),
        scratch_shapes=[
            pltpu.VMEM((tf // LANES, tm + HALO, LANES), F32),
            pltpu.VMEM((tf // LANES, tm + HALO, LANES), F32),
            pltpu.VMEM((tf // LANES, HALO, LANES), F32),
            pltpu.VMEM((tf // LANES, HALO, LANES), F32),
            pltpu.VMEM((tm, tf), BF16),
        ],
        compiler_params=pltpu.CompilerParams(
            dimension_semantics=("arbitrary",), vmem_limit_bytes=VMEM_LIMIT),
        name="ffn",
    )(h2, x1, w_up, w_up, conv_w, conv_w, conv_b, conv_b, w_down)


def _pad_lanes(v):
    return jnp.pad(v.astype(F32), (0, LANES - v.shape[0])).reshape(1, LANES)


def kernel(x, norm1_g, w_in, ssd_conv_w, ssd_conv_b, dt_bias, a_log, d_skip, ssd_norm_g, w_ssd_proj,
           q_norm_g, k_norm_g, w_attn_proj, w_out, norm2_g, w_up, ffn_conv_w, ffn_conv_b, w_down):
    batch, seq, _ = x.shape
    assert norm1_g.shape[0] == 1, "single-layer block"
    x2 = x.reshape(batch * seq, D_MODEL)
    g1 = norm1_g[0].reshape(1, D_MODEL)

    w = w_in[0]
    o_xbc = D_INNER
    o_dt = o_xbc + SSD_CONV_DIM
    o_q = o_dt + SSD_HEADS
    o_g = o_q + 3 * ATTN_WIDTH
    w_main = jnp.concatenate([w[:, o_xbc:o_dt], w[:, :o_xbc], w[:, o_g:]], axis=1).astype(BF16)
    w_dt = jnp.pad(w[:, o_dt:o_q], ((0, 0), (0, LANES - SSD_HEADS))).astype(BF16)
    w_qkv = w[:, o_q:o_g].astype(BF16)

    proj, dt_raw = _in_proj_main(x2, g1, w_main, w_dt, ssd_conv_w[0], ssd_conv_b[0].reshape(1, -1),
                                 tm=1024, tn=2048, seq=seq)

    heads_per_tile = MXU_WIDTH // ATTN_HEAD_DIM
    qk_gains = jnp.stack([jnp.tile(q_norm_g[0] * (ATTN_HEAD_DIM ** -0.5), heads_per_tile),
                          jnp.tile(k_norm_g[0], heads_per_tile)]).astype(F32)
    lane_head = jnp.arange(MXU_WIDTH) // ATTN_HEAD_DIM
    seg = (lane_head[:, None] == lane_head[None, :]).astype(BF16)
    qkv = _in_proj_qkv(x2, g1, w_qkv, qk_gains, seg, tm=1024)

    tri = jnp.tril(jnp.ones((SSD_CHUNK, SSD_CHUNK), BF16))
    head_of_col = jnp.arange(D_INNER) // SSD_HEAD_DIM
    expand = (jnp.arange(LANES)[:, None] == head_of_col[None, :]).astype(BF16)
    y_ssd = _ssd(proj, dt_raw, _pad_lanes(dt_bias[0]), _pad_lanes(a_log[0]),
                 jnp.repeat(d_skip[0], SSD_HEAD_DIM).reshape(1, D_INNER),
                 ssd_norm_g[0].reshape(1, D_INNER), tri, jnp.concatenate([expand, expand], axis=0),
                 batch, seq, chunks_per_step=2)

    outs, lses = zip(*[_attn_group(qkv, gi, batch, seq) for gi in range(ATTN_GROUPS)])

    attn_head_of_col = jnp.arange(ATTN_GROUP_WIDTH) // ATTN_HEAD_DIM
    head_expand = (jnp.arange(LANES)[:, None] == attn_head_of_col[None, :]).astype(BF16)
    x1, h2 = _merge(x2, y_ssd, outs, lses, proj, w_ssd_proj[0].astype(BF16), w_attn_proj[0].astype(BF16),
                    w_out[0].astype(BF16), head_expand, norm2_g[0].reshape(1, D_MODEL), tm=512)

    out = _ffn(h2, x1, w_up[0].astype(BF16), ffn_conv_w[0], ffn_conv_b[0].reshape(1, -1),
               w_down[0].astype(BF16), tm=512, seq=seq)
    return out.reshape(batch, seq, D_MODEL)
```

```python
import functools

import jax
import jax.numpy as jnp
from jax import lax
from jax.experimental import pallas as pl
from jax.experimental.pallas import tpu as pltpu

F32 = jnp.float32
BF16 = jnp.bfloat16

D_MODEL = 1024
D_INNER = 2048
SSD_HEADS = 32
SSD_HEAD_DIM = 64
SSD_GROUPS = 8
SSD_HEADS_PER_GROUP = SSD_HEADS // SSD_GROUPS
D_STATE = 128
SSD_CONV = 4
SSD_CHUNK = 128
SSD_CONV_DIM = D_INNER + 2 * SSD_GROUPS * D_STATE
SSD_GROUP_WIDTH = D_INNER // SSD_GROUPS
ATTN_WINDOWS = (128, 512, 2048)
ATTN_DILATIONS = (1, 4, 16)
ATTN_GROUPS = 3
ATTN_HEADS = 8
ATTN_HEAD_DIM = 64
ATTN_BLOCK = 128
ATTN_GROUP_WIDTH = ATTN_HEADS * ATTN_HEAD_DIM
ATTN_WIDTH = ATTN_GROUPS * ATTN_GROUP_WIDTH
D_FF = 2816
FFN_CONV = 3
EPS = 1e-6
LANES = 128
MXU_WIDTH = 256
HALO = 8
ATTN_SLAB = ATTN_BLOCK * max(ATTN_DILATIONS)
ATTN_SLAB_BLOCKS = ATTN_SLAB // ATTN_BLOCK
ATTN_LANE_CHUNKS = ATTN_GROUP_WIDTH // LANES
ATTN_UNROLL = 8

COL_XBC = 0
COL_Z = COL_XBC + SSD_CONV_DIM
COL_GSSD = COL_Z + D_INNER
COL_GATTN = COL_GSSD + D_MODEL
PROJ_WIDTH = COL_GATTN + D_MODEL

VMEM_LIMIT = 56 * 1024 * 1024


def _sigmoid(v):
    return 0.5 * jnp.tanh(0.5 * v) + 0.5


def _silu_of_half(hv):
    return hv + hv * jnp.tanh(hv)


def _split_bf16(v, terms):
    parts = []
    for _ in range(terms):
        p = v.astype(BF16)
        parts.append(p)
        v = v - p.astype(F32)
    return parts


def _dot_split(parts, rhs, lhs_side=True):
    acc = None
    for p in parts:
        t = (jnp.dot(p, rhs, preferred_element_type=F32) if lhs_side
             else jnp.dot(rhs, p, preferred_element_type=F32))
        acc = t if acc is None else acc + t
    return acc


def _rms_rows(x, gain):
    ms = jnp.mean(x * x, axis=-1, keepdims=True)
    return x * lax.rsqrt(ms + EPS) * gain


def _in_proj_main_kernel(x_ref, g_ref, w_ref, wdt_ref, cw_ref, cb_ref, proj_ref, dt_ref,
                         h_scr, xp_scr, halo_scr, *, tiles_per_seq):
    i = pl.program_id(0)
    j = pl.program_id(1)
    tm, tn = proj_ref.shape
    conv_tiles = SSD_CONV_DIM // tn
    z_tiles = D_INNER // tn

    @pl.when(j == 0)
    def _():
        h = _rms_rows(x_ref[...], g_ref[...]).astype(BF16)
        h_scr[...] = h
        dt_ref[...] = jnp.dot(h, wdt_ref[...], preferred_element_type=F32)

    @pl.when(jnp.logical_and(i == 0, j == 0))
    def _():
        halo_scr[...] = jnp.zeros(halo_scr.shape, F32)

    @pl.when(j < conv_tiles)
    def _():
        r = jnp.dot(h_scr[...], w_ref[...], preferred_element_type=F32)
        first = (i % tiles_per_seq) == 0
        for c in range(tn // LANES):
            cs = slice(c * LANES, (c + 1) * LANES)
            xp_scr[c, 0:HALO, :] = jnp.where(first, 0.0, halo_scr[j, c])
            xp_scr[c, HALO:HALO + tm, :] = r[:, cs]
            acc = 0.5 * cb_ref[:, cs]
            for t in range(SSD_CONV):
                tap = xp_scr[c, pl.ds(HALO - (SSD_CONV - 1) + t, tm, stride=1), :]
                acc = acc + tap * (0.5 * cw_ref[t:t + 1, cs])
            proj_ref[:, cs] = _silu_of_half(acc).astype(proj_ref.dtype)
            halo_scr[j, c] = xp_scr[c, tm:tm + HALO, :]

    @pl.when(jnp.logical_and(j >= conv_tiles, j < conv_tiles + z_tiles))
    def _():
        r = jnp.dot(h_scr[...], w_ref[...], preferred_element_type=F32)
        proj_ref[...] = _silu_of_half(0.5 * r).astype(proj_ref.dtype)

    @pl.when(j >= conv_tiles + z_tiles)
    def _():
        proj_ref[...] = jnp.dot(h_scr[...], w_ref[...], preferred_element_type=F32).astype(proj_ref.dtype)


def _in_proj_main(x2, g1, w_main, w_dt, conv_w, conv_b, tm, tn, seq):
    m = x2.shape[0]
    assert SSD_CONV_DIM % tn == 0 and D_INNER % tn == 0 and seq % tm == 0
    conv_tiles = SSD_CONV_DIM // tn
    conv_col = lambda i, j: (0, jnp.minimum(j, conv_tiles - 1))
    return pl.pallas_call(
        functools.partial(_in_proj_main_kernel, tiles_per_seq=seq // tm),
        grid=(m // tm, PROJ_WIDTH // tn),
        in_specs=[
            pl.BlockSpec((tm, D_MODEL), lambda i, j: (i, 0)),
            pl.BlockSpec((1, D_MODEL), lambda i, j: (0, 0)),
            pl.BlockSpec((D_MODEL, tn), lambda i, j: (0, j)),
            pl.BlockSpec((D_MODEL, LANES), lambda i, j: (0, 0)),
            pl.BlockSpec((SSD_CONV, tn), conv_col),
            pl.BlockSpec((1, tn), conv_col),
        ],
        out_specs=[
            pl.BlockSpec((tm, tn), lambda i, j: (i, j)),
            pl.BlockSpec((tm, LANES), lambda i, j: (i, 0)),
        ],
        out_shape=[
            jax.ShapeDtypeStruct((m, PROJ_WIDTH), BF16),
            jax.ShapeDtypeStruct((m, LANES), F32),
        ],
        scratch_shapes=[
            pltpu.VMEM((tm, D_MODEL), BF16),
            pltpu.VMEM((tn // LANES, HALO + tm, LANES), F32),
            pltpu.VMEM((conv_tiles, tn // LANES, HALO, LANES), F32),
        ],
        compiler_params=pltpu.CompilerParams(
            dimension_semantics=("arbitrary", "arbitrary"), vmem_limit_bytes=VMEM_LIMIT),
        name="in_proj_main",
    )(x2, g1, w_main, w_dt, conv_w, conv_b)


def _in_proj_qkv_kernel(x_ref, g_ref, w_ref, gain_ref, seg_ref, out_ref, h_scr, t_scr):
    j = pl.program_id(1)
    seg_w = seg_ref.shape[0]

    @pl.when(j == 0)
    def _():
        h_scr[...] = _rms_rows(x_ref[...], g_ref[...]).astype(BF16)

    @pl.when(j < 2)
    def _():
        t_scr[...] = jnp.dot(h_scr[...], w_ref[...], preferred_element_type=F32)
        gain = gain_ref[pl.ds(j, 1), :]
        for c in range(ATTN_WIDTH // seg_w):
            cs = slice(c * seg_w, (c + 1) * seg_w)
            t = t_scr[:, cs]
            ss = jnp.dot((t * t).astype(BF16), seg_ref[...], preferred_element_type=F32)
            out_ref[:, cs] = (t * lax.rsqrt(ss * (1.0 / ATTN_HEAD_DIM) + EPS) * gain).astype(out_ref.dtype)

    @pl.when(j == 2)
    def _():
        out_ref[...] = jnp.dot(h_scr[...], w_ref[...], preferred_element_type=F32).astype(out_ref.dtype)


def _in_proj_qkv(x2, g1, w_qkv, gains, seg, tm):
    m = x2.shape[0]
    return pl.pallas_call(
        _in_proj_qkv_kernel,
        grid=(m // tm, 3),
        in_specs=[
            pl.BlockSpec((tm, D_MODEL), lambda i, j: (i, 0)),
            pl.BlockSpec((1, D_MODEL), lambda i, j: (0, 0)),
            pl.BlockSpec((D_MODEL, ATTN_WIDTH), lambda i, j: (0, j)),
            pl.BlockSpec(gains.shape, lambda i, j: (0, 0)),
            pl.BlockSpec(seg.shape, lambda i, j: (0, 0)),
        ],
        out_specs=pl.BlockSpec((tm, ATTN_WIDTH), lambda i, j: (i, j)),
        out_shape=jax.ShapeDtypeStruct((m, 3 * ATTN_WIDTH), BF16),
        scratch_shapes=[pltpu.VMEM((tm, D_MODEL), BF16), pltpu.VMEM((tm, ATTN_WIDTH), F32)],
        compiler_params=pltpu.CompilerParams(
            dimension_semantics=("arbitrary", "arbitrary"), vmem_limit_bytes=VMEM_LIMIT),
        name="in_proj_qkv",
    )(x2, g1, w_qkv, gains, seg)


def _ssd_chunk(act, gz, dt_raw, dtb, a_log, dskip, ng, tri, expand2, y_ref, rows, state_scr):
    q = SSD_CHUNK
    dt = jax.nn.softplus(dt_raw + dtb)
    a = dt * (-jnp.exp(a_log))
    a_cs = _dot_split(_split_bf16(a, 3), tri, lhs_side=False)
    a_cs_t_adj = a_cs.T - jnp.log(dt.T)
    e_in = jnp.exp(a_cs)
    w_out = jnp.exp(a_cs[q - 1:q, :] - a_cs) * dt
    lhs = jnp.concatenate([jnp.concatenate(_split_bf16(e_in, 2), axis=1),
                           jnp.concatenate(_split_bf16(w_out, 2), axis=1)], axis=0)
    both = jnp.dot(lhs, expand2, preferred_element_type=F32)
    e_exp, w_exp = both[:q], both[q:]

    li = lax.broadcasted_iota(jnp.int32, (q, q), 0)
    si = lax.broadcasted_iota(jnp.int32, (q, q), 1)
    causal = li >= si
    head_of_lane = lax.broadcasted_iota(jnp.int32, (q, SSD_GROUP_WIDTH), 1) // SSD_HEAD_DIM

    for g in range(SSD_GROUPS):
        gs = slice(g * SSD_GROUP_WIDTH, (g + 1) * SSD_GROUP_WIDTH)
        xs_b = act[:, gs]
        xs = xs_b.astype(F32)
        b_g = act[:, D_INNER + g * D_STATE:D_INNER + (g + 1) * D_STATE]
        c_g = act[:, D_INNER + (SSD_GROUPS + g) * D_STATE:D_INNER + (SSD_GROUPS + g + 1) * D_STATE]
        cb = lax.dot_general(c_g, b_g, (((1,), (1,)), ((), ())), preferred_element_type=F32)
        st = state_scr[g]
        y_off = jnp.dot(c_g, st.astype(BF16), preferred_element_type=F32) * e_exp[:, gs]
        mms, blocks = [], []
        for k in range(SSD_HEADS_PER_GROUP):
            h = g * SSD_HEADS_PER_GROUP + k
            seg = a_cs[:, h:h + 1] - a_cs_t_adj[h:h + 1, :]
            mms.append((cb * jnp.exp(jnp.where(causal, seg, -jnp.inf))).astype(BF16))
            blocks.append(jnp.where(head_of_lane == k, xs_b, jnp.zeros_like(xs_b)))
        y_diag = jnp.dot(jnp.concatenate(mms, axis=1), jnp.concatenate(blocks, axis=0),
                         preferred_element_type=F32)
        y = y_diag + y_off + dskip[:, gs] * xs
        y = y * gz[:, gs].astype(F32)
        ms = jnp.mean(y * y, axis=-1, keepdims=True)
        y_ref[rows, gs] = (y * lax.rsqrt(ms + EPS) * ng[:, gs]).astype(y_ref.dtype)
        xw = (xs * w_exp[:, gs]).astype(BF16)
        upd = lax.dot_general(b_g, xw, (((0,), (0,)), ((), ())), preferred_element_type=F32)
        state_scr[g] = st * e_exp[q - 1:q, gs] + upd


def _ssd_kernel(act_ref, gz_ref, dt_ref, dtb_ref, alog_ref, dskip_ref, ng_ref, tri_ref, expand2_ref,
                y_ref, state_scr):
    @pl.when(pl.program_id(1) == 0)
    def _():
        state_scr[...] = jnp.zeros(state_scr.shape, F32)

    for ci in range(act_ref.shape[0] // SSD_CHUNK):
        rows = slice(ci * SSD_CHUNK, (ci + 1) * SSD_CHUNK)
        _ssd_chunk(act_ref[rows, :], gz_ref[rows, :], dt_ref[rows, :], dtb_ref[...], alog_ref[...],
                   dskip_ref[...], ng_ref[...], tri_ref[...], expand2_ref[...], y_ref, rows, state_scr)


def _ssd(proj, dt_raw, dt_bias, a_log, d_skip_exp, norm_g, tri, expand2, batch, seq, chunks_per_step):
    rows = SSD_CHUNK * chunks_per_step
    steps = seq // rows
    row = lambda b, c: b * steps + c
    const = lambda b, c: (0, 0)
    return pl.pallas_call(
        _ssd_kernel,
        grid=(batch, steps),
        in_specs=[
            pl.BlockSpec((rows, SSD_CONV_DIM), lambda b, c: (row(b, c), COL_XBC // SSD_CONV_DIM)),
            pl.BlockSpec((rows, D_INNER), lambda b, c: (row(b, c), COL_Z // D_INNER)),
            pl.BlockSpec((rows, LANES), lambda b, c: (row(b, c), 0)),
            pl.BlockSpec((1, LANES), const),
            pl.BlockSpec((1, LANES), const),
            pl.BlockSpec((1, D_INNER), const),
            pl.BlockSpec((1, D_INNER), const),
            pl.BlockSpec((SSD_CHUNK, SSD_CHUNK), const),
            pl.BlockSpec((2 * LANES, D_INNER), const),
        ],
        out_specs=pl.BlockSpec((rows, D_INNER), lambda b, c: (row(b, c), 0)),
        out_shape=jax.ShapeDtypeStruct((batch * seq, D_INNER), BF16),
        scratch_shapes=[pltpu.VMEM((SSD_GROUPS, D_STATE, SSD_GROUP_WIDTH), F32)],
        compiler_params=pltpu.CompilerParams(
            dimension_semantics=("arbitrary", "arbitrary"), vmem_limit_bytes=VMEM_LIMIT),
        name="ssd",
    )(proj, proj, dt_raw, dt_bias, a_log, d_skip_exp, norm_g, tri, expand2)


def _attn_kernel(q_ref, k_ref, v_ref, o_ref, lse_ref, stage, q_de, k_de, v_de, *, dil, n_back):
    blk = ATTN_BLOCK
    per_res = ATTN_SLAB_BLOCKS // dil
    run = per_res + 1
    first_slab = pl.program_id(1) == 0

    def keep_previous(dst):
        @pl.when(first_slab)
        def _():
            def body(res, carry):
                dst[res * run] = jnp.zeros(dst.shape[1:], BF16)
                return carry
            lax.fori_loop(0, dil, body, 0)

        @pl.when(jnp.logical_not(first_slab))
        def _():
            def body(res, carry):
                dst[res * run] = dst[res * run + per_res]
                return carry
            lax.fori_loop(0, dil, body, 0)

    def deinterleave(src_ref, dst, skip):
        for c in range(ATTN_LANE_CHUNKS):
            stage[c] = src_ref[:, c * LANES:(c + 1) * LANES].astype(F32)

        def body(bi, carry):
            res = bi // per_res
            start = res + dil * blk * (bi % per_res)
            for c in range(ATTN_LANE_CHUNKS):
                dst[bi + skip * (res + 1), c] = stage[c, pl.ds(start, blk, stride=dil), :].astype(BF16)
            return carry

        lax.fori_loop(0, ATTN_SLAB_BLOCKS, body, 0)

    keep_previous(k_de)
    keep_previous(v_de)
    deinterleave(k_ref, k_de, 1)
    deinterleave(v_ref, v_de, 1)
    deinterleave(q_ref, q_de, 0)

    lane = lax.broadcasted_iota(jnp.int32, (blk, LANES), 1)
    even = lane < ATTN_HEAD_DIM
    qi = lax.broadcasted_iota(jnp.int32, (blk, 2 * blk), 0)
    ki = lax.broadcasted_iota(jnp.int32, (blk, 2 * blk), 1)
    dist = blk + qi - ki
    band = (dist >= 0) & (dist <= n_back)

    def attend(bi, carry):
        res = bi // per_res
        j = bi % per_res
        key_lo = jnp.where(jnp.logical_and(first_slab, j == 0), blk, 0)
        mask = band & (ki >= key_lo)
        kv = res * run + j
        start = res + dil * blk * j
        lse_tile = jnp.zeros((blk, LANES), F32)
        for hp in range(ATTN_LANE_CHUNKS):
            q_pair = q_de[bi, hp]
            k2 = jnp.concatenate([k_de[kv, hp], k_de[kv + 1, hp]], axis=0)
            v2 = jnp.concatenate([v_de[kv, hp], v_de[kv + 1, hp]], axis=0)
            v2 = jnp.concatenate([v2, jnp.ones_like(v2)], axis=1)
            halves = []
            for e, sel in enumerate((even, jnp.logical_not(even))):
                qm = jnp.where(sel, q_pair, jnp.zeros_like(q_pair))
                s = lax.dot_general(qm, k2, (((1,), (1,)), ((), ())), preferred_element_type=F32)
                s = jnp.where(mask, s, -jnp.inf)
                m = jnp.max(s, axis=-1, keepdims=True)
                p = jnp.exp(s - m)
                pv = jnp.dot(p.astype(BF16), v2, preferred_element_type=F32)
                den = pv[:, LANES:]
                halves.append(pv[:, :LANES] * (1.0 / den))
                lse_tile = jnp.where(lane == 2 * hp + e, m + jnp.log(den), lse_tile)
            stage[hp, pl.ds(start, blk, stride=dil), :] = jnp.where(even, halves[0], halves[1])
        lse_ref[pl.ds(start, blk, stride=dil), :] = lse_tile
        return carry

    lax.fori_loop(0, ATTN_SLAB_BLOCKS, attend, 0, unroll=ATTN_UNROLL)
    for c in range(ATTN_LANE_CHUNKS):
        o_ref[:, c * LANES:(c + 1) * LANES] = stage[c].astype(o_ref.dtype)


def _attn_group(qkv, gi, batch, seq):
    dil = ATTN_DILATIONS[gi]
    n_back = ATTN_WINDOWS[gi] // dil
    assert n_back <= ATTN_BLOCK and ATTN_SLAB_BLOCKS % dil == 0 and seq % ATTN_SLAB == 0
    slabs = seq // ATTN_SLAB

    def cur(which):
        return pl.BlockSpec((ATTN_SLAB, ATTN_GROUP_WIDTH), lambda b, s: (b * slabs + s, which * ATTN_GROUPS + gi))

    run = ATTN_SLAB_BLOCKS // dil + 1
    de_block = (ATTN_LANE_CHUNKS, ATTN_BLOCK, LANES)
    return pl.pallas_call(
        functools.partial(_attn_kernel, dil=dil, n_back=n_back),
        grid=(batch, slabs),
        in_specs=[cur(0), cur(1), cur(2)],
        out_specs=[pl.BlockSpec((ATTN_SLAB, ATTN_GROUP_WIDTH), lambda b, s: (b * slabs + s, 0)),
                   pl.BlockSpec((ATTN_SLAB, LANES), lambda b, s: (b * slabs + s, 0))],
        out_shape=[jax.ShapeDtypeStruct((batch * seq, ATTN_GROUP_WIDTH), BF16),
                   jax.ShapeDtypeStruct((batch * seq, LANES), F32)],
        scratch_shapes=[
            pltpu.VMEM((ATTN_LANE_CHUNKS, ATTN_SLAB, LANES), F32),
            pltpu.VMEM((ATTN_SLAB_BLOCKS,) + de_block, BF16),
            pltpu.VMEM((dil * run,) + de_block, BF16),
            pltpu.VMEM((dil * run,) + de_block, BF16),
        ],
        compiler_params=pltpu.CompilerParams(
            dimension_semantics=("arbitrary", "arbitrary"), vmem_limit_bytes=VMEM_LIMIT),
        name=f"attn_g{gi}",
    )(qkv, qkv, qkv)


def _expand_heads(w, hexp):
    return _dot_split(_split_bf16(w, 2), hexp)


def _merge_kernel(x_ref, yssd_ref, o0_ref, o1_ref, o2_ref, l0_ref, l1_ref, l2_ref, gssd_ref, gattn_ref,
                  wssd_ref, wattn_ref, wout_ref, hexp_ref, g2_ref, x1_ref, h2_ref):
    l0, l1, l2 = l0_ref[...], l1_ref[...], l2_ref[...]
    mx = jnp.maximum(jnp.maximum(l0, l1), l2)
    e0, e1, e2 = jnp.exp(l0 - mx), jnp.exp(l1 - mx), jnp.exp(l2 - mx)
    inv = 1.0 / (e0 + e1 + e2)
    y_attn = jnp.zeros(o0_ref.shape, F32)
    for e, o_ref in ((e0, o0_ref), (e1, o1_ref), (e2, o2_ref)):
        y_attn = y_attn + _expand_heads(e * inv, hexp_ref[...]) * o_ref[...].astype(F32)
    p_ssd = jnp.dot(yssd_ref[...], wssd_ref[...], preferred_element_type=F32)
    p_attn = jnp.dot(y_attn.astype(BF16), wattn_ref[...], preferred_element_type=F32)
    merged = (_sigmoid(gssd_ref[...].astype(F32)) * p_ssd
              + _sigmoid(gattn_ref[...].astype(F32)) * p_attn)
    x1 = x_ref[...] + jnp.dot(merged.astype(BF16), wout_ref[...], preferred_element_type=F32)
    x1_ref[...] = x1
    h2_ref[...] = _rms_rows(x1, g2_ref[...]).astype(h2_ref.dtype)


def _merge(x2, y_ssd, outs, lses, proj, w_ssd, w_attn, w_out, head_expand, g2, tm):
    m = x2.shape[0]
    rows = lambda width, cb=0: pl.BlockSpec((tm, width), lambda i: (i, cb))
    const = lambda shape: pl.BlockSpec(shape, lambda i: (0, 0))
    return pl.pallas_call(
        _merge_kernel,
        grid=(m // tm,),
        in_specs=[rows(D_MODEL), rows(D_INNER),
                  rows(ATTN_GROUP_WIDTH), rows(ATTN_GROUP_WIDTH), rows(ATTN_GROUP_WIDTH),
                  rows(LANES), rows(LANES), rows(LANES),
                  rows(D_MODEL, COL_GSSD // D_MODEL), rows(D_MODEL, COL_GATTN // D_MODEL),
                  const((D_INNER, D_MODEL)), const((ATTN_GROUP_WIDTH, D_MODEL)), const((D_MODEL, D_MODEL)),
                  const((LANES, ATTN_GROUP_WIDTH)), const((1, D_MODEL))],
        out_specs=[rows(D_MODEL), rows(D_MODEL)],
        out_shape=[jax.ShapeDtypeStruct((m, D_MODEL), F32), jax.ShapeDtypeStruct((m, D_MODEL), BF16)],
        compiler_params=pltpu.CompilerParams(
            dimension_semantics=("arbitrary",), vmem_limit_bytes=VMEM_LIMIT),
        name="merge",
    )(x2, y_ssd, outs[0], outs[1], outs[2], lses[0], lses[1], lses[2], proj, proj,
      w_ssd, w_attn, w_out, head_expand, g2)


def _ffn_kernel(h_ref, x1_ref, wg_ref, wv_ref, cwg_ref, cwv_ref, cbg_ref, cbv_ref, wd_ref, out_ref,
                ug_scr, uv_scr, halo_g, halo_v, act_scr, *, tiles_per_seq):
    i = pl.program_id(0)
    tm = h_ref.shape[0]
    tf = wg_ref.shape[1]
    first = (i % tiles_per_seq) == 0

    @pl.when(i == 0)
    def _():
        halo_g[...] = jnp.zeros(halo_g.shape, F32)
        halo_v[...] = jnp.zeros(halo_v.shape, F32)

    def conv(u, u_scr, halo, w_ref, b_ref, c, scale):
        cs = slice(c * LANES, (c + 1) * LANES)
        u_scr[c, 0:HALO, :] = jnp.where(first, 0.0, halo[c])
        u_scr[c, HALO:HALO + tm, :] = u[:, cs]
        acc = scale * b_ref[:, cs]
        for t in range(FFN_CONV):
            tap = u_scr[c, pl.ds(HALO - (FFN_CONV - 1) + t, tm, stride=1), :]
            acc = acc + tap * (scale * w_ref[t:t + 1, cs])
        halo[c] = u_scr[c, tm:tm + HALO, :]
        return acc

    ug = jnp.dot(h_ref[...], wg_ref[...], preferred_element_type=F32)
    uv = jnp.dot(h_ref[...], wv_ref[...], preferred_element_type=F32)
    for c in range(tf // LANES):
        gate = conv(ug, ug_scr, halo_g, cwg_ref, cbg_ref, c, 0.5)
        val = conv(uv, uv_scr, halo_v, cwv_ref, cbv_ref, c, 1.0)
        act_scr[:, c * LANES:(c + 1) * LANES] = (_silu_of_half(gate) * val).astype(BF16)
    out_ref[...] = x1_ref[...] + jnp.dot(act_scr[...], wd_ref[...], preferred_element_type=F32)


def _ffn(h2, x1, w_up, conv_w, conv_b, w_down, tm, seq):
    m = h2.shape[0]
    tf = D_FF
    once = dict(pipeline_mode=pl.Buffered(1))
    gate_half = lambda i: (0, 0)
    value_half = lambda i: (0, 1)
    return pl.pallas_call(
        functools.partial(_ffn_kernel, tiles_per_seq=seq // tm),
        grid=(m // tm,),
        in_specs=[
            pl.BlockSpec((tm, D_MODEL), lambda i: (i, 0)),
            pl.BlockSpec((tm, D_MODEL), lambda i: (i, 0)),
            pl.BlockSpec((D_MODEL, tf), gate_half, **once),
            pl.BlockSpec((D_MODEL, tf), value_half, **once),
            pl.BlockSpec((FFN_CONV, tf), gate_half, **once),
            pl.BlockSpec((FFN_CONV, tf), value_half, **once),
            pl.BlockSpec((1, tf), gate_half, **once),
            pl.BlockSpec((1, tf), value_half, **once),
            pl.BlockSpec((tf, D_MODEL), lambda i: (0, 0), **once),
        ],
        out_specs=pl.BlockSpec((tm, D_MODEL), lambda i: (i, 0)),
        out_shape=jax.ShapeDtypeStruct((m, D_MODEL), F32),
        scratch_shapes=[
            pltpu.VMEM((tf // LANES, tm + HALO, LANES), F32),
            pltpu.VMEM((tf // LANES, tm + HALO, LANES), F32),
            pltpu.VMEM((tf // LANES, HALO, LANES), F32),
            pltpu.VMEM((tf // LANES, HALO, LANES), F32),
            pltpu.VMEM((tm, tf), BF16),
        ],
        compiler_params=pltpu.CompilerParams(
            dimension_semantics=("arbitrary",), vmem_limit_bytes=VMEM_LIMIT),
        name="ffn",
    )(h2, x1, w_up, w_up, conv_w, conv_w, conv_b, conv_b, w_down)


def _pad_lanes(v):
    return jnp.pad(v.astype(F32), (0, LANES - v.shape[0])).reshape(1, LANES)


def kernel(x, norm1_g, w_in, ssd_conv_w, ssd_conv_b, dt_bias, a_log, d_skip, ssd_norm_g, w_ssd_proj,
           q_norm_g, k_norm_g, w_attn_proj, w_out, norm2_g, w_up, ffn_conv_w, ffn_conv_b, w_down):
    batch, seq, _ = x.shape
    assert norm1_g.shape[0] == 1, "single-layer block"
    x2 = x.reshape(batch * seq, D_MODEL)
    g1 = norm1_g[0].reshape(1, D_MODEL)

    w = w_in[0]
    o_xbc = D_INNER
    o_dt = o_xbc + SSD_CONV_DIM
    o_q = o_dt + SSD_HEADS
    o_g = o_q + 3 * ATTN_WIDTH
    w_main = jnp.concatenate([w[:, o_xbc:o_dt], w[:, :o_xbc], w[:, o_g:]], axis=1).astype(BF16)
    w_dt = jnp.pad(w[:, o_dt:o_q], ((0, 0), (0, LANES - SSD_HEADS))).astype(BF16)
    w_qkv = w[:, o_q:o_g].astype(BF16)

    proj, dt_raw = _in_proj_main(x2, g1, w_main, w_dt, ssd_conv_w[0], ssd_conv_b[0].reshape(1, -1),
                                 tm=1024, tn=2048, seq=seq)

    heads_per_tile = MXU_WIDTH // ATTN_HEAD_DIM
    qk_gains = jnp.stack([jnp.tile(q_norm_g[0] * (ATTN_HEAD_DIM ** -0.5), heads_per_tile),
                          jnp.tile(k_norm_g[0], heads_per_tile)]).astype(F32)
    lane_head = jnp.arange(MXU_WIDTH) // ATTN_HEAD_DIM
    seg = (lane_head[:, None] == lane_head[None, :]).astype(BF16)
    qkv = _in_proj_qkv(x2, g1, w_qkv, qk_gains, seg, tm=1024)

    tri = jnp.tril(jnp.ones((SSD_CHUNK, SSD_CHUNK), BF16))
    head_of_col = jnp.arange(D_INNER) // SSD_HEAD_DIM
    expand = (jnp.arange(LANES)[:, None] == head_of_col[None, :]).astype(BF16)
    y_ssd = _ssd(proj, dt_raw, _pad_lanes(dt_bias[0]), _pad_lanes(a_log[0]),
                 jnp.repeat(d_skip[0], SSD_HEAD_DIM).reshape(1, D_INNER),
                 ssd_norm_g[0].reshape(1, D_INNER), tri, jnp.concatenate([expand, expand], axis=0),
                 batch, seq, chunks_per_step=4)

    outs, lses = zip(*[_attn_group(qkv, gi, batch, seq) for gi in range(ATTN_GROUPS)])

    attn_head_of_col = jnp.arange(ATTN_GROUP_WIDTH) // ATTN_HEAD_DIM
    head_expand = (jnp.arange(LANES)[:, None] == attn_head_of_col[None, :]).astype(BF16)
    x1, h2 = _merge(x2, y_ssd, outs, lses, proj, w_ssd_proj[0].astype(BF16), w_attn_proj[0].astype(BF16),
                    w_out[0].astype(BF16), head_expand, norm2_g[0].reshape(1, D_MODEL), tm=512)

    out = _ffn(h2, x1, w_up[0].astype(BF16), ffn_conv_w[0], ffn_conv_b[0].reshape(1, -1),
               w_down[0].astype(BF16), tm=512, seq=seq)
    return out.reshape(batch, seq, D_MODEL)
```

```python
import functools

import jax
import jax.numpy as jnp
from jax import lax
from jax.experimental import pallas as pl
from jax.experimental.pallas import tpu as pltpu

F32 = jnp.float32
BF16 = jnp.bfloat16

D_MODEL = 1024
D_INNER = 2048
SSD_HEADS = 32
SSD_HEAD_DIM = 64
SSD_GROUPS = 8
SSD_HEADS_PER_GROUP = SSD_HEADS // SSD_GROUPS
D_STATE = 128
SSD_CONV = 4
SSD_CHUNK = 128
SSD_CONV_DIM = D_INNER + 2 * SSD_GROUPS * D_STATE
SSD_GROUP_WIDTH = D_INNER // SSD_GROUPS
ATTN_WINDOWS = (128, 512, 2048)
ATTN_DILATIONS = (1, 4, 16)
ATTN_GROUPS = 3
ATTN_HEADS = 8
ATTN_HEAD_DIM = 64
ATTN_BLOCK = 128
ATTN_GROUP_WIDTH = ATTN_HEADS * ATTN_HEAD_DIM
ATTN_WIDTH = ATTN_GROUPS * ATTN_GROUP_WIDTH
D_FF = 2816
FFN_CONV = 3
EPS = 1e-6
LANES = 128
MXU_WIDTH = 256
HALO = 8
ATTN_SLAB = ATTN_BLOCK * max(ATTN_DILATIONS)
ATTN_SLAB_BLOCKS = ATTN_SLAB // ATTN_BLOCK
ATTN_LANE_CHUNKS = ATTN_GROUP_WIDTH // LANES
ATTN_UNROLL = 16
DEINTERLEAVE_FACTOR = 4

COL_XBC = 0
COL_Z = COL_XBC + SSD_CONV_DIM
COL_GSSD = COL_Z + D_INNER
COL_GATTN = COL_GSSD + D_MODEL
PROJ_WIDTH = COL_GATTN + D_MODEL

VMEM_LIMIT = 56 * 1024 * 1024


def _sigmoid(v):
    return 0.5 * jnp.tanh(0.5 * v) + 0.5


def _silu_of_half(hv):
    return hv + hv * jnp.tanh(hv)


def _split_bf16(v, terms):
    parts = []
    for _ in range(terms):
        p = v.astype(BF16)
        parts.append(p)
        v = v - p.astype(F32)
    return parts


def _dot_split(parts, rhs, lhs_side=True):
    acc = None
    for p in parts:
        t = (jnp.dot(p, rhs, preferred_element_type=F32) if lhs_side
             else jnp.dot(rhs, p, preferred_element_type=F32))
        acc = t if acc is None else acc + t
    return acc


def _rms_rows(x, gain):
    ms = jnp.mean(x * x, axis=-1, keepdims=True)
    return x * lax.rsqrt(ms + EPS) * gain


def _in_proj_main_kernel(x_ref, g_ref, w_ref, wdt_ref, cw_ref, cb_ref, proj_ref, dt_ref,
                         h_scr, xp_scr, halo_scr, *, tiles_per_seq):
    i = pl.program_id(0)
    j = pl.program_id(1)
    tm, tn = proj_ref.shape
    conv_tiles = SSD_CONV_DIM // tn
    z_tiles = D_INNER // tn

    @pl.when(j == 0)
    def _():
        h = _rms_rows(x_ref[...], g_ref[...]).astype(BF16)
        h_scr[...] = h
        dt_ref[...] = jnp.dot(h, wdt_ref[...], preferred_element_type=F32)

    @pl.when(jnp.logical_and(i == 0, j == 0))
    def _():
        halo_scr[...] = jnp.zeros(halo_scr.shape, F32)

    @pl.when(j < conv_tiles)
    def _():
        r = jnp.dot(h_scr[...], w_ref[...], preferred_element_type=F32)
        first = (i % tiles_per_seq) == 0
        for c in range(tn // LANES):
            cs = slice(c * LANES, (c + 1) * LANES)
            xp_scr[c, 0:HALO, :] = jnp.where(first, 0.0, halo_scr[j, c])
            xp_scr[c, HALO:HALO + tm, :] = r[:, cs]
            acc = 0.5 * cb_ref[:, cs]
            for t in range(SSD_CONV):
                tap = xp_scr[c, pl.ds(HALO - (SSD_CONV - 1) + t, tm, stride=1), :]
                acc = acc + tap * (0.5 * cw_ref[t:t + 1, cs])
            proj_ref[:, cs] = _silu_of_half(acc).astype(proj_ref.dtype)
            halo_scr[j, c] = xp_scr[c, tm:tm + HALO, :]

    @pl.when(jnp.logical_and(j >= conv_tiles, j < conv_tiles + z_tiles))
    def _():
        r = jnp.dot(h_scr[...], w_ref[...], preferred_element_type=F32)
        proj_ref[...] = _silu_of_half(0.5 * r).astype(proj_ref.dtype)

    @pl.when(j >= conv_tiles + z_tiles)
    def _():
        proj_ref[...] = jnp.dot(h_scr[...], w_ref[...], preferred_element_type=F32).astype(proj_ref.dtype)


def _in_proj_main(x2, g1, w_main, w_dt, conv_w, conv_b, tm, tn, seq):
    m = x2.shape[0]
    assert SSD_CONV_DIM % tn == 0 and D_INNER % tn == 0 and seq % tm == 0
    conv_tiles = SSD_CONV_DIM // tn
    conv_col = lambda i, j: (0, jnp.minimum(j, conv_tiles - 1))
    return pl.pallas_call(
        functools.partial(_in_proj_main_kernel, tiles_per_seq=seq // tm),
        grid=(m // tm, PROJ_WIDTH // tn),
        in_specs=[
            pl.BlockSpec((tm, D_MODEL), lambda i, j: (i, 0)),
            pl.BlockSpec((1, D_MODEL), lambda i, j: (0, 0)),
            pl.BlockSpec((D_MODEL, tn), lambda i, j: (0, j)),
            pl.BlockSpec((D_MODEL, LANES), lambda i, j: (0, 0)),
            pl.BlockSpec((SSD_CONV, tn), conv_col),
            pl.BlockSpec((1, tn), conv_col),
        ],
        out_specs=[
            pl.BlockSpec((tm, tn), lambda i, j: (i, j)),
            pl.BlockSpec((tm, LANES), lambda i, j: (i, 0)),
        ],
        out_shape=[
            jax.ShapeDtypeStruct((m, PROJ_WIDTH), BF16),
            jax.ShapeDtypeStruct((m, LANES), F32),
        ],
        scratch_shapes=[
            pltpu.VMEM((tm, D_MODEL), BF16),
            pltpu.VMEM((tn // LANES, HALO + tm, LANES), F32),
            pltpu.VMEM((conv_tiles, tn // LANES, HALO, LANES), F32),
        ],
        compiler_params=pltpu.CompilerParams(
            dimension_semantics=("arbitrary", "arbitrary"), vmem_limit_bytes=VMEM_LIMIT),
        name="in_proj_main",
    )(x2, g1, w_main, w_dt, conv_w, conv_b)


def _in_proj_qkv_kernel(x_ref, g_ref, w_ref, gain_ref, seg_ref, out_ref, h_scr, t_scr):
    j = pl.program_id(1)
    seg_w = seg_ref.shape[0]

    @pl.when(j == 0)
    def _():
        h_scr[...] = _rms_rows(x_ref[...], g_ref[...]).astype(BF16)

    @pl.when(j < 2)
    def _():
        t_scr[...] = jnp.dot(h_scr[...], w_ref[...], preferred_element_type=F32)
        gain = gain_ref[pl.ds(j, 1), :]
        for c in range(ATTN_WIDTH // seg_w):
            cs = slice(c * seg_w, (c + 1) * seg_w)
            t = t_scr[:, cs]
            ss = jnp.dot((t * t).astype(BF16), seg_ref[...], preferred_element_type=F32)
            out_ref[:, cs] = (t * lax.rsqrt(ss * (1.0 / ATTN_HEAD_DIM) + EPS) * gain).astype(out_ref.dtype)

    @pl.when(j == 2)
    def _():
        out_ref[...] = jnp.dot(h_scr[...], w_ref[...], preferred_element_type=F32).astype(out_ref.dtype)


def _in_proj_qkv(x2, g1, w_qkv, gains, seg, tm):
    m = x2.shape[0]
    return pl.pallas_call(
        _in_proj_qkv_kernel,
        grid=(m // tm, 3),
        in_specs=[
            pl.BlockSpec((tm, D_MODEL), lambda i, j: (i, 0)),
            pl.BlockSpec((1, D_MODEL), lambda i, j: (0, 0)),
            pl.BlockSpec((D_MODEL, ATTN_WIDTH), lambda i, j: (0, j)),
            pl.BlockSpec(gains.shape, lambda i, j: (0, 0)),
            pl.BlockSpec(seg.shape, lambda i, j: (0, 0)),
        ],
        out_specs=pl.BlockSpec((tm, ATTN_WIDTH), lambda i, j: (i, j)),
        out_shape=jax.ShapeDtypeStruct((m, 3 * ATTN_WIDTH), BF16),
        scratch_shapes=[pltpu.VMEM((tm, D_MODEL), BF16), pltpu.VMEM((tm, ATTN_WIDTH), F32)],
        compiler_params=pltpu.CompilerParams(
            dimension_semantics=("arbitrary", "arbitrary"), vmem_limit_bytes=VMEM_LIMIT),
        name="in_proj_qkv",
    )(x2, g1, w_qkv, gains, seg)


def _ssd_chunk(act, dt_raw, dtb, a_log, dskip, tri, expand2, y_ref, rows, state_scr):
    q = SSD_CHUNK
    dt = jax.nn.softplus(dt_raw + dtb)
    a = dt * (-jnp.exp(a_log))
    a_cs = _dot_split(_split_bf16(a, 3), tri, lhs_side=False)
    a_cs_t_adj = a_cs.T - jnp.log(dt.T)
    e_in = jnp.exp(a_cs)
    w_out = jnp.exp(a_cs[q - 1:q, :] - a_cs) * dt
    lhs = jnp.concatenate([jnp.concatenate(_split_bf16(e_in, 2), axis=1),
                           jnp.concatenate(_split_bf16(w_out, 2), axis=1)], axis=0)
    both = jnp.dot(lhs, expand2, preferred_element_type=F32)
    e_exp, w_exp = both[:q], both[q:]

    li = lax.broadcasted_iota(jnp.int32, (q, q), 0)
    si = lax.broadcasted_iota(jnp.int32, (q, q), 1)
    causal = li >= si
    head_of_lane = lax.broadcasted_iota(jnp.int32, (q, SSD_GROUP_WIDTH), 1) // SSD_HEAD_DIM

    for g in range(SSD_GROUPS):
        gs = slice(g * SSD_GROUP_WIDTH, (g + 1) * SSD_GROUP_WIDTH)
        xs_b = act[:, gs]
        xs = xs_b.astype(F32)
        b_g = act[:, D_INNER + g * D_STATE:D_INNER + (g + 1) * D_STATE]
        c_g = act[:, D_INNER + (SSD_GROUPS + g) * D_STATE:D_INNER + (SSD_GROUPS + g + 1) * D_STATE]
        cb = lax.dot_general(c_g, b_g, (((1,), (1,)), ((), ())), preferred_element_type=F32)
        st = state_scr[g]
        y_off = jnp.dot(c_g, st.astype(BF16), preferred_element_type=F32) * e_exp[:, gs]
        mms, blocks = [], []
        for k in range(SSD_HEADS_PER_GROUP):
            h = g * SSD_HEADS_PER_GROUP + k
            seg = a_cs[:, h:h + 1] - a_cs_t_adj[h:h + 1, :]
            mms.append((cb * jnp.exp(jnp.where(causal, seg, -jnp.inf))).astype(BF16))
            blocks.append(jnp.where(head_of_lane == k, xs_b, jnp.zeros_like(xs_b)))
        y_diag = jnp.dot(jnp.concatenate(mms, axis=1), jnp.concatenate(blocks, axis=0),
                         preferred_element_type=F32)
        y = y_diag + y_off + dskip[:, gs] * xs
        y_ref[rows, gs] = y.astype(y_ref.dtype)
        xw = (xs * w_exp[:, gs]).astype(BF16)
        upd = lax.dot_general(b_g, xw, (((0,), (0,)), ((), ())), preferred_element_type=F32)
        state_scr[g] = st * e_exp[q - 1:q, gs] + upd


def _ssd_kernel(act_ref, dt_ref, dtb_ref, alog_ref, dskip_ref, tri_ref, expand2_ref, y_ref, state_scr):
    @pl.when(pl.program_id(1) == 0)
    def _():
        state_scr[...] = jnp.zeros(state_scr.shape, F32)

    for ci in range(act_ref.shape[0] // SSD_CHUNK):
        rows = slice(ci * SSD_CHUNK, (ci + 1) * SSD_CHUNK)
        _ssd_chunk(act_ref[rows, :], dt_ref[rows, :], dtb_ref[...], alog_ref[...], dskip_ref[...],
                   tri_ref[...], expand2_ref[...], y_ref, rows, state_scr)


def _ssd(proj, dt_raw, dt_bias, a_log, d_skip_exp, tri, expand2, batch, seq, chunks_per_step):
    rows = SSD_CHUNK * chunks_per_step
    steps = seq // rows
    row = lambda b, c: b * steps + c
    const = lambda b, c: (0, 0)
    return pl.pallas_call(
        _ssd_kernel,
        grid=(batch, steps),
        in_specs=[
            pl.BlockSpec((rows, SSD_CONV_DIM), lambda b, c: (row(b, c), COL_XBC // SSD_CONV_DIM)),
            pl.BlockSpec((rows, LANES), lambda b, c: (row(b, c), 0)),
            pl.BlockSpec((1, LANES), const),
            pl.BlockSpec((1, LANES), const),
            pl.BlockSpec((1, D_INNER), const),
            pl.BlockSpec((SSD_CHUNK, SSD_CHUNK), const),
            pl.BlockSpec((2 * LANES, D_INNER), const),
        ],
        out_specs=pl.BlockSpec((rows, D_INNER), lambda b, c: (row(b, c), 0)),
        out_shape=jax.ShapeDtypeStruct((batch * seq, D_INNER), BF16),
        scratch_shapes=[pltpu.VMEM((SSD_GROUPS, D_STATE, SSD_GROUP_WIDTH), F32)],
        compiler_params=pltpu.CompilerParams(
            dimension_semantics=("arbitrary", "arbitrary"), vmem_limit_bytes=VMEM_LIMIT),
        name="ssd",
    )(proj, dt_raw, dt_bias, a_log, d_skip_exp, tri, expand2)


def _attn_kernel(q_ref, k_ref, v_ref, o_ref, lse_ref, stage, q_de, k_de, v_de, *maybe_mid, dil, n_back):
    blk = ATTN_BLOCK
    per_res = ATTN_SLAB_BLOCKS // dil
    run = per_res + 1
    first_slab = pl.program_id(1) == 0

    def keep_previous(dst):
        @pl.when(first_slab)
        def _():
            def body(res, carry):
                dst[res * run] = jnp.zeros(dst.shape[1:], BF16)
                return carry
            lax.fori_loop(0, dil, body, 0)

        @pl.when(jnp.logical_not(first_slab))
        def _():
            def body(res, carry):
                dst[res * run] = dst[res * run + per_res]
                return carry
            lax.fori_loop(0, dil, body, 0)

    def deinterleave(src_ref, dst, skip):
        if dil == 1:
            def copy(bi, carry):
                rows = pl.ds(pl.multiple_of(bi * blk, blk), blk)
                for c in range(ATTN_LANE_CHUNKS):
                    dst[bi + skip, c] = src_ref[rows, c * LANES:(c + 1) * LANES]
                return carry

            lax.fori_loop(0, ATTN_SLAB_BLOCKS, copy, 0)
            return

        for c in range(ATTN_LANE_CHUNKS):
            stage[c] = src_ref[:, c * LANES:(c + 1) * LANES].astype(F32)

        if maybe_mid:
            mid, = maybe_mid
            f = DEINTERLEAVE_FACTOR
            for c in range(ATTN_LANE_CHUNKS):
                for r in range(f):
                    mid[c * f + r] = stage[c, pl.ds(r, ATTN_SLAB // f, stride=f), :]

            def body(bi, carry):
                for c in range(ATTN_LANE_CHUNKS):
                    dst[bi + skip * (bi + 1), c] = (
                        mid[c * f + bi % f, pl.ds(bi // f, blk, stride=f), :].astype(BF16))
                return carry
        else:
            def body(bi, carry):
                res = bi // per_res
                start = res + dil * blk * (bi % per_res)
                for c in range(ATTN_LANE_CHUNKS):
                    dst[bi + skip * (res + 1), c] = stage[c, pl.ds(start, blk, stride=dil), :].astype(BF16)
                return carry

        lax.fori_loop(0, ATTN_SLAB_BLOCKS, body, 0)

    keep_previous(k_de)
    keep_previous(v_de)
    deinterleave(k_ref, k_de, 1)
    deinterleave(v_ref, v_de, 1)
    deinterleave(q_ref, q_de, 0)

    lane = lax.broadcasted_iota(jnp.int32, (blk, LANES), 1)
    even = lane < ATTN_HEAD_DIM
    qi = lax.broadcasted_iota(jnp.int32, (blk, 2 * blk), 0)
    ki = lax.broadcasted_iota(jnp.int32, (blk, 2 * blk), 1)
    dist = blk + qi - ki
    band = (dist >= 0) & (dist <= n_back)

    def attend(bi, carry):
        res = bi // per_res
        j = bi % per_res
        key_lo = jnp.where(jnp.logical_and(first_slab, j == 0), blk, 0)
        mask = band & (ki >= key_lo)
        kv = res * run + j
        start = res + dil * blk * j
        lse_tile = jnp.zeros((blk, LANES), F32)
        for hp in range(ATTN_LANE_CHUNKS):
            q_pair = q_de[bi, hp]
            k2 = jnp.concatenate([k_de[kv, hp], k_de[kv + 1, hp]], axis=0)
            v2 = jnp.concatenate([v_de[kv, hp], v_de[kv + 1, hp]], axis=0)
            v2 = jnp.concatenate([v2, jnp.ones_like(v2)], axis=1)
            halves = []
            for e, sel in enumerate((even, jnp.logical_not(even))):
                qm = jnp.where(sel, q_pair, jnp.zeros_like(q_pair))
                s = lax.dot_general(qm, k2, (((1,), (1,)), ((), ())), preferred_element_type=F32)
                s = jnp.where(mask, s, -jnp.inf)
                m = jnp.max(s, axis=-1, keepdims=True)
                p = jnp.exp(s - m)
                pv = jnp.dot(p.astype(BF16), v2, preferred_element_type=F32)
                den = pv[:, LANES:]
                halves.append(pv[:, :LANES] * (1.0 / den))
                lse_tile = jnp.where(lane == 2 * hp + e, m + jnp.log(den), lse_tile)
            stage[hp, pl.ds(start, blk, stride=dil), :] = jnp.where(even, halves[0], halves[1])
        lse_ref[pl.ds(start, blk, stride=dil), :] = lse_tile
        return carry

    lax.fori_loop(0, ATTN_SLAB_BLOCKS, attend, 0, unroll=ATTN_UNROLL)
    for c in range(ATTN_LANE_CHUNKS):
        o_ref[:, c * LANES:(c + 1) * LANES] = stage[c].astype(o_ref.dtype)


def _attn_group(qkv, gi, batch, seq):
    dil = ATTN_DILATIONS[gi]
    n_back = ATTN_WINDOWS[gi] // dil
    assert n_back <= ATTN_BLOCK and ATTN_SLAB_BLOCKS % dil == 0 and seq % ATTN_SLAB == 0
    slabs = seq // ATTN_SLAB

    def cur(which):
        return pl.BlockSpec((ATTN_SLAB, ATTN_GROUP_WIDTH), lambda b, s: (b * slabs + s, which * ATTN_GROUPS + gi))

    run = ATTN_SLAB_BLOCKS // dil + 1
    de_block = (ATTN_LANE_CHUNKS, ATTN_BLOCK, LANES)
    f = DEINTERLEAVE_FACTOR
    two_pass = dil == f * f and dil == ATTN_SLAB_BLOCKS
    two_pass_scratch = [pltpu.VMEM((ATTN_LANE_CHUNKS * f, ATTN_SLAB // f, LANES), F32)] if two_pass else []
    return pl.pallas_call(
        functools.partial(_attn_kernel, dil=dil, n_back=n_back),
        grid=(batch, slabs),
        in_specs=[cur(0), cur(1), cur(2)],
        out_specs=[pl.BlockSpec((ATTN_SLAB, ATTN_GROUP_WIDTH), lambda b, s: (b * slabs + s, 0)),
                   pl.BlockSpec((ATTN_SLAB, LANES), lambda b, s: (b * slabs + s, 0))],
        out_shape=[jax.ShapeDtypeStruct((batch * seq, ATTN_GROUP_WIDTH), BF16),
                   jax.ShapeDtypeStruct((batch * seq, LANES), F32)],
        scratch_shapes=[
            pltpu.VMEM((ATTN_LANE_CHUNKS, ATTN_SLAB, LANES), F32),
            pltpu.VMEM((ATTN_SLAB_BLOCKS,) + de_block, BF16),
            pltpu.VMEM((dil * run,) + de_block, BF16),
            pltpu.VMEM((dil * run,) + de_block, BF16),
        ] + two_pass_scratch,
        compiler_params=pltpu.CompilerParams(
            dimension_semantics=("arbitrary", "arbitrary"), vmem_limit_bytes=VMEM_LIMIT),
        name=f"attn_g{gi}",
    )(qkv, qkv, qkv)


def _expand_heads(w, hexp):
    return jnp.dot(jnp.concatenate(_split_bf16(w, 2), axis=1), hexp, preferred_element_type=F32)


def _merge_kernel(x_ref, yssd_ref, gz_ref, o0_ref, o1_ref, o2_ref, l0_ref, l1_ref, l2_ref, gssd_ref, gattn_ref,
                  ng_ref, wssd_ref, wattn_ref, wout_ref, hexp_ref, g2_ref, x1_ref, h2_ref):
    l0, l1, l2 = l0_ref[...], l1_ref[...], l2_ref[...]
    mx = jnp.maximum(jnp.maximum(l0, l1), l2)
    e0, e1, e2 = jnp.exp(l0 - mx), jnp.exp(l1 - mx), jnp.exp(l2 - mx)
    inv = 1.0 / (e0 + e1 + e2)
    y_attn = jnp.zeros(o0_ref.shape, F32)
    for e, o_ref in ((e0, o0_ref), (e1, o1_ref), (e2, o2_ref)):
        y_attn = y_attn + _expand_heads(e * inv, hexp_ref[...]) * o_ref[...].astype(F32)
    p_ssd = None
    for g in range(SSD_GROUPS):
        gs = slice(g * SSD_GROUP_WIDTH, (g + 1) * SSD_GROUP_WIDTH)
        y = yssd_ref[:, gs].astype(F32) * gz_ref[:, gs].astype(F32)
        yn = _rms_rows(y, ng_ref[:, gs]).astype(BF16)
        part = jnp.dot(yn, wssd_ref[gs, :], preferred_element_type=F32)
        p_ssd = part if p_ssd is None else p_ssd + part
    p_attn = jnp.dot(y_attn.astype(BF16), wattn_ref[...], preferred_element_type=F32)
    merged = (_sigmoid(gssd_ref[...].astype(F32)) * p_ssd
              + _sigmoid(gattn_ref[...].astype(F32)) * p_attn)
    x1 = x_ref[...] + jnp.dot(merged.astype(BF16), wout_ref[...], preferred_element_type=F32)
    x1_ref[...] = x1
    h2_ref[...] = _rms_rows(x1, g2_ref[...]).astype(h2_ref.dtype)


def _merge(x2, y_ssd, outs, lses, proj, ssd_norm_g, w_ssd, w_attn, w_out, head_expand, g2, tm):
    m = x2.shape[0]
    rows = lambda width, cb=0: pl.BlockSpec((tm, width), lambda i: (i, cb))
    const = lambda shape: pl.BlockSpec(shape, lambda i: (0, 0))
    return pl.pallas_call(
        _merge_kernel,
        grid=(m // tm,),
        in_specs=[rows(D_MODEL), rows(D_INNER), rows(D_INNER, COL_Z // D_INNER),
                  rows(ATTN_GROUP_WIDTH), rows(ATTN_GROUP_WIDTH), rows(ATTN_GROUP_WIDTH),
                  rows(LANES), rows(LANES), rows(LANES),
                  rows(D_MODEL, COL_GSSD // D_MODEL), rows(D_MODEL, COL_GATTN // D_MODEL),
                  const((1, D_INNER)), const((D_INNER, D_MODEL)), const((ATTN_GROUP_WIDTH, D_MODEL)),
                  const((D_MODEL, D_MODEL)),
                  const((2 * LANES, ATTN_GROUP_WIDTH)), const((1, D_MODEL))],
        out_specs=[rows(D_MODEL), rows(D_MODEL)],
        out_shape=[jax.ShapeDtypeStruct((m, D_MODEL), F32), jax.ShapeDtypeStruct((m, D_MODEL), BF16)],
        compiler_params=pltpu.CompilerParams(
            dimension_semantics=("arbitrary",), vmem_limit_bytes=VMEM_LIMIT),
        name="merge",
    )(x2, y_ssd, proj, outs[0], outs[1], outs[2], lses[0], lses[1], lses[2], proj, proj,
      ssd_norm_g, w_ssd, w_attn, w_out, head_expand, g2)


def _ffn_kernel(h_ref, x1_ref, wg_ref, wv_ref, cwg_ref, cwv_ref, cbg_ref, cbv_ref, wd_ref, out_ref,
                ug_scr, uv_scr, halo_g, halo_v, act_scr, *, tiles_per_seq):
    i = pl.program_id(0)
    tm = h_ref.shape[0]
    tf = wg_ref.shape[1]
    first = (i % tiles_per_seq) == 0

    @pl.when(i == 0)
    def _():
        halo_g[...] = jnp.zeros(halo_g.shape, F32)
        halo_v[...] = jnp.zeros(halo_v.shape, F32)

    def conv(u, u_scr, halo, w_ref, b_ref, c, scale):
        cs = slice(c * LANES, (c + 1) * LANES)
        u_scr[c, 0:HALO, :] = jnp.where(first, 0.0, halo[c])
        u_scr[c, HALO:HALO + tm, :] = u[:, cs]
        acc = scale * b_ref[:, cs]
        for t in range(FFN_CONV):
            tap = u_scr[c, pl.ds(HALO - (FFN_CONV - 1) + t, tm, stride=1), :]
            acc = acc + tap * (scale * w_ref[t:t + 1, cs])
        halo[c] = u_scr[c, tm:tm + HALO, :]
        return acc

    ug = jnp.dot(h_ref[...], wg_ref[...], preferred_element_type=F32)
    uv = jnp.dot(h_ref[...], wv_ref[...], preferred_element_type=F32)
    for c in range(tf // LANES):
        gate = conv(ug, ug_scr, halo_g, cwg_ref, cbg_ref, c, 0.5)
        val = conv(uv, uv_scr, halo_v, cwv_ref, cbv_ref, c, 1.0)
        act_scr[:, c * LANES:(c + 1) * LANES] = (_silu_of_half(gate) * val).astype(BF16)
    out_ref[...] = x1_ref[...] + jnp.dot(act_scr[...], wd_ref[...], preferred_element_type=F32)


def _ffn(h2, x1, w_up, conv_w, conv_b, w_down, tm, seq):
    m = h2.shape[0]
    tf = D_FF
    once = dict(pipeline_mode=pl.Buffered(1))
    gate_half = lambda i: (0, 0)
    value_half = lambda i: (0, 1)
    return pl.pallas_call(
        functools.partial(_ffn_kernel, tiles_per_seq=seq // tm),
        grid=(m // tm,),
        in_specs=[
            pl.BlockSpec((tm, D_MODEL), lambda i: (i, 0)),
            pl.BlockSpec((tm, D_MODEL), lambda i: (i, 0)),
            pl.BlockSpec((D_MODEL, tf), gate_half, **once),
            pl.BlockSpec((D_MODEL, tf), value_half, **once),
            pl.BlockSpec((FFN_CONV, tf), gate_half, **once),
            pl.BlockSpec((FFN_CONV, tf), value_half, **once),
            pl.BlockSpec((1, tf), gate_half, **once),
            pl.BlockSpec((1, tf), value_half, **once),
            pl.BlockSpec((tf, D_MODEL), lambda i: (0, 0), **once),
        ],
        out_specs=pl.BlockSpec((tm, D_MODEL), lambda i: (i, 0)),
        out_shape=jax.ShapeDtypeStruct((m, D_MODEL), F32),
        scratch_shapes=[
            pltpu.VMEM((tf // LANES, tm + HALO, LANES), F32),
            pltpu.VMEM((tf // LANES, tm + HALO, LANES), F32),
            pltpu.VMEM((tf // LANES, HALO, LANES), F32),
            pltpu.VMEM((tf // LANES, HALO, LANES), F32),
            pltpu.VMEM((tm, tf), BF16),
        ],
        compiler_params=pltpu.CompilerParams(
            dimension_semantics=("arbitrary",), vmem_limit_bytes=VMEM_LIMIT),
        name="ffn",
    )(h2, x1, w_up, w_up, conv_w, conv_w, conv_b, conv_b, w_down)


def _pad_lanes(v):
    return jnp.pad(v.astype(F32), (0, LANES - v.shape[0])).reshape(1, LANES)


def kernel(x, norm1_g, w_in, ssd_conv_w, ssd_conv_b, dt_bias, a_log, d_skip, ssd_norm_g, w_ssd_proj,
           q_norm_g, k_norm_g, w_attn_proj, w_out, norm2_g, w_up, ffn_conv_w, ffn_conv_b, w_down):
    batch, seq, _ = x.shape
    assert norm1_g.shape[0] == 1, "single-layer block"
    x2 = x.reshape(batch * seq, D_MODEL)
    g1 = norm1_g[0].reshape(1, D_MODEL)

    w = w_in[0]
    o_xbc = D_INNER
    o_dt = o_xbc + SSD_CONV_DIM
    o_q = o_dt + SSD_HEADS
    o_g = o_q + 3 * ATTN_WIDTH
    w_main = jnp.concatenate([w[:, o_xbc:o_dt], w[:, :o_xbc], w[:, o_g:]], axis=1).astype(BF16)
    w_dt = jnp.pad(w[:, o_dt:o_q], ((0, 0), (0, LANES - SSD_HEADS))).astype(BF16)
    w_qkv = w[:, o_q:o_g].astype(BF16)

    proj, dt_raw = _in_proj_main(x2, g1, w_main, w_dt, ssd_conv_w[0], ssd_conv_b[0].reshape(1, -1),
                                 tm=1024, tn=2048, seq=seq)

    heads_per_tile = MXU_WIDTH // ATTN_HEAD_DIM
    qk_gains = jnp.stack([jnp.tile(q_norm_g[0] * (ATTN_HEAD_DIM ** -0.5), heads_per_tile),
                          jnp.tile(k_norm_g[0], heads_per_tile)]).astype(F32)
    lane_head = jnp.arange(MXU_WIDTH) // ATTN_HEAD_DIM
    seg = (lane_head[:, None] == lane_head[None, :]).astype(BF16)
    qkv = _in_proj_qkv(x2, g1, w_qkv, qk_gains, seg, tm=1024)

    tri = jnp.tril(jnp.ones((SSD_CHUNK, SSD_CHUNK), BF16))
    head_of_col = jnp.arange(D_INNER) // SSD_HEAD_DIM
    expand = (jnp.arange(LANES)[:, None] == head_of_col[None, :]).astype(BF16)
    y_ssd = _ssd(proj, dt_raw, _pad_lanes(dt_bias[0]), _pad_lanes(a_log[0]),
                 jnp.repeat(d_skip[0], SSD_HEAD_DIM).reshape(1, D_INNER),
                 tri, jnp.concatenate([expand, expand], axis=0),
                 batch, seq, chunks_per_step=4)

    outs, lses = zip(*[_attn_group(qkv, gi, batch, seq) for gi in range(ATTN_GROUPS)])

    attn_head_of_col = jnp.arange(ATTN_GROUP_WIDTH) // ATTN_HEAD_DIM
    head_expand = (jnp.arange(2 * LANES)[:, None] % LANES == attn_head_of_col[None, :]).astype(BF16)
    x1, h2 = _merge(x2, y_ssd, outs, lses, proj, ssd_norm_g[0].reshape(1, D_INNER),
                    w_ssd_proj[0].astype(BF16), w_attn_proj[0].astype(BF16), w_out[0].astype(BF16),
                    head_expand, norm2_g[0].reshape(1, D_MODEL), tm=512)

    out = _ffn(h2, x1, w_up[0].astype(BF16), ffn_conv_w[0], ffn_conv_b[0].reshape(1, -1),
               w_down[0].astype(BF16), tm=512, seq=seq)
    return out.reshape(batch, seq, D_MODEL)
```

```python
import functools

import jax
import jax.numpy as jnp
from jax import lax
from jax.experimental import pallas as pl
from jax.experimental.pallas import tpu as pltpu

F32 = jnp.float32
BF16 = jnp.bfloat16

D_MODEL = 1024
D_INNER = 2048
SSD_HEADS = 32
SSD_HEAD_DIM = 64
SSD_GROUPS = 8
SSD_HEADS_PER_GROUP = SSD_HEADS // SSD_GROUPS
D_STATE = 128
SSD_CONV = 4
SSD_CHUNK = 128
SSD_CONV_DIM = D_INNER + 2 * SSD_GROUPS * D_STATE
SSD_GROUP_WIDTH = D_INNER // SSD_GROUPS
ATTN_WINDOWS = (128, 512, 2048)
ATTN_DILATIONS = (1, 4, 16)
ATTN_GROUPS = 3
ATTN_HEADS = 8
ATTN_HEAD_DIM = 64
ATTN_BLOCK = 128
ATTN_GROUP_WIDTH = ATTN_HEADS * ATTN_HEAD_DIM
ATTN_WIDTH = ATTN_GROUPS * ATTN_GROUP_WIDTH
D_FF = 2816
FFN_CONV = 3
FFN_CHUNK = D_FF // 2
EPS = 1e-6
LANES = 128
MXU_WIDTH = 256
HALO = 8
ATTN_SLAB = ATTN_BLOCK * max(ATTN_DILATIONS)
ATTN_SLAB_BLOCKS = ATTN_SLAB // ATTN_BLOCK
ATTN_LANE_CHUNKS = ATTN_GROUP_WIDTH // LANES
ATTN_UNROLL = 16
DEINTERLEAVE_FACTOR = 4

COL_XBC = 0
COL_Z = COL_XBC + SSD_CONV_DIM
COL_GSSD = COL_Z + D_INNER
COL_GATTN = COL_GSSD + D_MODEL
PROJ_WIDTH = COL_GATTN + D_MODEL

V7X_VMEM_BYTES = 64 * 1024 * 1024
VMEM_LIMIT = V7X_VMEM_BYTES * 7 // 8

TM_IN_PROJ = 1024
TN_IN_PROJ = 2048
TM_MERGE = 512
TM_FFN = 512
SSD_CHUNKS_PER_STEP = 4


def _sigmoid(v):
    return 0.5 * jnp.tanh(0.5 * v) + 0.5


def _silu_of_half(hv):
    return hv + hv * jnp.tanh(hv)


def _split_bf16(v, terms):
    parts = []
    for _ in range(terms):
        p = v.astype(BF16)
        parts.append(p)
        v = v - p.astype(F32)
    return parts


def _dot_split(parts, rhs, lhs_side=True):
    acc = None
    for p in parts:
        t = (jnp.dot(p, rhs, preferred_element_type=F32) if lhs_side
             else jnp.dot(rhs, p, preferred_element_type=F32))
        acc = t if acc is None else acc + t
    return acc


def _rms_rows(x, gain):
    ms = jnp.mean(x * x, axis=-1, keepdims=True)
    return x * lax.rsqrt(ms + EPS) * gain


def _in_proj_main_kernel(x_ref, g_ref, w_ref, wdt_ref, cw_ref, cb_ref, proj_ref, dt_ref,
                         h_scr, xp_scr, halo_scr, *, tiles_per_seq):
    i = pl.program_id(0)
    j = pl.program_id(1)
    tm, tn = proj_ref.shape
    conv_tiles = SSD_CONV_DIM // tn
    z_tiles = D_INNER // tn

    @pl.when(j == 0)
    def _():
        h = _rms_rows(x_ref[...], g_ref[...]).astype(BF16)
        h_scr[...] = h
        dt_ref[...] = jnp.dot(h, wdt_ref[...], preferred_element_type=F32)

    @pl.when(jnp.logical_and(i == 0, j == 0))
    def _():
        halo_scr[...] = jnp.zeros(halo_scr.shape, F32)

    @pl.when(j < conv_tiles)
    def _():
        r = jnp.dot(h_scr[...], w_ref[...], preferred_element_type=F32)
        first = (i % tiles_per_seq) == 0
        for c in range(tn // LANES):
            cs = slice(c * LANES, (c + 1) * LANES)
            xp_scr[c, 0:HALO, :] = jnp.where(first, 0.0, halo_scr[j, c])
            xp_scr[c, HALO:HALO + tm, :] = r[:, cs]
            acc = 0.5 * cb_ref[:, cs]
            for t in range(SSD_CONV):
                tap = xp_scr[c, pl.ds(HALO - (SSD_CONV - 1) + t, tm, stride=1), :]
                acc = acc + tap * (0.5 * cw_ref[t:t + 1, cs])
            proj_ref[:, cs] = _silu_of_half(acc).astype(proj_ref.dtype)
            halo_scr[j, c] = xp_scr[c, tm:tm + HALO, :]

    @pl.when(jnp.logical_and(j >= conv_tiles, j < conv_tiles + z_tiles))
    def _():
        r = jnp.dot(h_scr[...], w_ref[...], preferred_element_type=F32)
        proj_ref[...] = _silu_of_half(0.5 * r).astype(proj_ref.dtype)

    @pl.when(j >= conv_tiles + z_tiles)
    def _():
        proj_ref[...] = jnp.dot(h_scr[...], w_ref[...], preferred_element_type=F32).astype(proj_ref.dtype)


def _in_proj_main(x2, g1, w_main, w_dt, conv_w, conv_b, tm, tn, seq):
    m = x2.shape[0]
    assert SSD_CONV_DIM % tn == 0 and D_INNER % tn == 0 and seq % tm == 0
    conv_tiles = SSD_CONV_DIM // tn
    conv_col = lambda i, j: (0, jnp.minimum(j, conv_tiles - 1))
    return pl.pallas_call(
        functools.partial(_in_proj_main_kernel, tiles_per_seq=seq // tm),
        grid=(m // tm, PROJ_WIDTH // tn),
        in_specs=[
            pl.BlockSpec((tm, D_MODEL), lambda i, j: (i, 0)),
            pl.BlockSpec((1, D_MODEL), lambda i, j: (0, 0)),
            pl.BlockSpec((D_MODEL, tn), lambda i, j: (0, j)),
            pl.BlockSpec((D_MODEL, LANES), lambda i, j: (0, 0)),
            pl.BlockSpec((SSD_CONV, tn), conv_col),
            pl.BlockSpec((1, tn), conv_col),
        ],
        out_specs=[
            pl.BlockSpec((tm, tn), lambda i, j: (i, j)),
            pl.BlockSpec((tm, LANES), lambda i, j: (i, 0)),
        ],
        out_shape=[
            jax.ShapeDtypeStruct((m, PROJ_WIDTH), BF16),
            jax.ShapeDtypeStruct((m, LANES), F32),
        ],
        scratch_shapes=[
            pltpu.VMEM((tm, D_MODEL), BF16),
            pltpu.VMEM((tn // LANES, HALO + tm, LANES), F32),
            pltpu.VMEM((conv_tiles, tn // LANES, HALO, LANES), F32),
        ],
        compiler_params=pltpu.CompilerParams(
            dimension_semantics=("arbitrary", "arbitrary"), vmem_limit_bytes=VMEM_LIMIT),
        name="in_proj_main",
    )(x2, g1, w_main, w_dt, conv_w, conv_b)


def _in_proj_qkv_kernel(x_ref, g_ref, w_ref, gain_ref, seg_ref, out_ref, h_scr, t_scr):
    j = pl.program_id(1)
    seg_w = seg_ref.shape[0]

    @pl.when(j == 0)
    def _():
        h_scr[...] = _rms_rows(x_ref[...], g_ref[...]).astype(BF16)

    @pl.when(j < 2)
    def _():
        t_scr[...] = jnp.dot(h_scr[...], w_ref[...], preferred_element_type=F32)
        gain = gain_ref[pl.ds(j, 1), :]
        for c in range(ATTN_WIDTH // seg_w):
            cs = slice(c * seg_w, (c + 1) * seg_w)
            t = t_scr[:, cs]
            ss = jnp.dot((t * t).astype(BF16), seg_ref[...], preferred_element_type=F32)
            out_ref[:, cs] = (t * lax.rsqrt(ss * (1.0 / ATTN_HEAD_DIM) + EPS) * gain).astype(out_ref.dtype)

    @pl.when(j == 2)
    def _():
        out_ref[...] = jnp.dot(h_scr[...], w_ref[...], preferred_element_type=F32).astype(out_ref.dtype)


def _in_proj_qkv(x2, g1, w_qkv, gains, seg, tm):
    m = x2.shape[0]
    return pl.pallas_call(
        _in_proj_qkv_kernel,
        grid=(m // tm, 3),
        in_specs=[
            pl.BlockSpec((tm, D_MODEL), lambda i, j: (i, 0)),
            pl.BlockSpec((1, D_MODEL), lambda i, j: (0, 0)),
            pl.BlockSpec((D_MODEL, ATTN_WIDTH), lambda i, j: (0, j)),
            pl.BlockSpec(gains.shape, lambda i, j: (0, 0)),
            pl.BlockSpec(seg.shape, lambda i, j: (0, 0)),
        ],
        out_specs=pl.BlockSpec((tm, ATTN_WIDTH), lambda i, j: (i, j)),
        out_shape=jax.ShapeDtypeStruct((m, 3 * ATTN_WIDTH), BF16),
        scratch_shapes=[pltpu.VMEM((tm, D_MODEL), BF16), pltpu.VMEM((tm, ATTN_WIDTH), F32)],
        compiler_params=pltpu.CompilerParams(
            dimension_semantics=("arbitrary", "arbitrary"), vmem_limit_bytes=VMEM_LIMIT),
        name="in_proj_qkv",
    )(x2, g1, w_qkv, gains, seg)


def _ssd_chunk(act, dt_raw, dtb, a_log, dskip, tri, expand2, y_ref, rows, state_scr):
    q = SSD_CHUNK
    dt = jax.nn.softplus(dt_raw + dtb)
    a = dt * (-jnp.exp(a_log))
    a_cs = _dot_split(_split_bf16(a, 3), tri, lhs_side=False)
    a_cs_t_adj = a_cs.T - jnp.log(dt.T)
    e_in = jnp.exp(a_cs)
    w_out = jnp.exp(a_cs[q - 1:q, :] - a_cs) * dt
    lhs = jnp.concatenate([jnp.concatenate(_split_bf16(e_in, 2), axis=1),
                           jnp.concatenate(_split_bf16(w_out, 2), axis=1)], axis=0)
    both = jnp.dot(lhs, expand2, preferred_element_type=F32)
    e_exp, w_exp = both[:q], both[q:]

    li = lax.broadcasted_iota(jnp.int32, (q, q), 0)
    si = lax.broadcasted_iota(jnp.int32, (q, q), 1)
    causal = li >= si
    head_of_lane = lax.broadcasted_iota(jnp.int32, (q, SSD_GROUP_WIDTH), 1) // SSD_HEAD_DIM

    for g in range(SSD_GROUPS):
        gs = slice(g * SSD_GROUP_WIDTH, (g + 1) * SSD_GROUP_WIDTH)
        xs_b = act[:, gs]
        xs = xs_b.astype(F32)
        b_g = act[:, D_INNER + g * D_STATE:D_INNER + (g + 1) * D_STATE]
        c_g = act[:, D_INNER + (SSD_GROUPS + g) * D_STATE:D_INNER + (SSD_GROUPS + g + 1) * D_STATE]
        cb = lax.dot_general(c_g, b_g, (((1,), (1,)), ((), ())), preferred_element_type=F32)
        st = state_scr[g]
        y_off = jnp.dot(c_g, st.astype(BF16), preferred_element_type=F32) * e_exp[:, gs]
        mms, blocks = [], []
        for k in range(SSD_HEADS_PER_GROUP):
            h = g * SSD_HEADS_PER_GROUP + k
            seg = a_cs[:, h:h + 1] - a_cs_t_adj[h:h + 1, :]
            mms.append((cb * jnp.exp(jnp.where(causal, seg, -jnp.inf))).astype(BF16))
            blocks.append(jnp.where(head_of_lane == k, xs_b, jnp.zeros_like(xs_b)))
        y_diag = jnp.dot(jnp.concatenate(mms, axis=1), jnp.concatenate(blocks, axis=0),
                         preferred_element_type=F32)
        y = y_diag + y_off + dskip[:, gs] * xs
        y_ref[rows, gs] = y.astype(y_ref.dtype)
        xw = (xs * w_exp[:, gs]).astype(BF16)
        upd = lax.dot_general(b_g, xw, (((0,), (0,)), ((), ())), preferred_element_type=F32)
        state_scr[g] = st * e_exp[q - 1:q, gs] + upd


def _ssd_kernel(act_ref, dt_ref, dtb_ref, alog_ref, dskip_ref, tri_ref, expand2_ref, y_ref, state_scr):
    @pl.when(pl.program_id(1) == 0)
    def _():
        state_scr[...] = jnp.zeros(state_scr.shape, F32)

    for ci in range(act_ref.shape[0] // SSD_CHUNK):
        rows = slice(ci * SSD_CHUNK, (ci + 1) * SSD_CHUNK)
        _ssd_chunk(act_ref[rows, :], dt_ref[rows, :], dtb_ref[...], alog_ref[...], dskip_ref[...],
                   tri_ref[...], expand2_ref[...], y_ref, rows, state_scr)


def _ssd(proj, dt_raw, dt_bias, a_log, d_skip_exp, tri, expand2, batch, seq, chunks_per_step):
    rows = SSD_CHUNK * chunks_per_step
    steps = seq // rows
    row = lambda b, c: b * steps + c
    const = lambda b, c: (0, 0)
    return pl.pallas_call(
        _ssd_kernel,
        grid=(batch, steps),
        in_specs=[
            pl.BlockSpec((rows, SSD_CONV_DIM), lambda b, c: (row(b, c), COL_XBC // SSD_CONV_DIM)),
            pl.BlockSpec((rows, LANES), lambda b, c: (row(b, c), 0)),
            pl.BlockSpec((1, LANES), const),
            pl.BlockSpec((1, LANES), const),
            pl.BlockSpec((1, D_INNER), const),
            pl.BlockSpec((SSD_CHUNK, SSD_CHUNK), const),
            pl.BlockSpec((2 * LANES, D_INNER), const),
        ],
        out_specs=pl.BlockSpec((rows, D_INNER), lambda b, c: (row(b, c), 0)),
        out_shape=jax.ShapeDtypeStruct((batch * seq, D_INNER), BF16),
        scratch_shapes=[pltpu.VMEM((SSD_GROUPS, D_STATE, SSD_GROUP_WIDTH), F32)],
        compiler_params=pltpu.CompilerParams(
            dimension_semantics=("arbitrary", "arbitrary"), vmem_limit_bytes=VMEM_LIMIT),
        name="ssd",
    )(proj, dt_raw, dt_bias, a_log, d_skip_exp, tri, expand2)


def _attn_kernel(q_ref, k_ref, v_ref, o_ref, lse_ref, stage, q_de, k_de, v_de, *maybe_mid, dil, n_back):
    blk = ATTN_BLOCK
    per_res = ATTN_SLAB_BLOCKS // dil
    run = per_res + 1
    first_slab = pl.program_id(1) == 0

    def keep_previous(dst):
        @pl.when(first_slab)
        def _():
            def body(res, carry):
                dst[res * run] = jnp.zeros(dst.shape[1:], BF16)
                return carry
            lax.fori_loop(0, dil, body, 0)

        @pl.when(jnp.logical_not(first_slab))
        def _():
            def body(res, carry):
                dst[res * run] = dst[res * run + per_res]
                return carry
            lax.fori_loop(0, dil, body, 0)

    def deinterleave(src_ref, dst, skip):
        if dil == 1:
            def copy(bi, carry):
                rows = pl.ds(pl.multiple_of(bi * blk, blk), blk)
                for c in range(ATTN_LANE_CHUNKS):
                    dst[bi + skip, c] = src_ref[rows, c * LANES:(c + 1) * LANES]
                return carry

            lax.fori_loop(0, ATTN_SLAB_BLOCKS, copy, 0)
            return

        for c in range(ATTN_LANE_CHUNKS):
            stage[c] = src_ref[:, c * LANES:(c + 1) * LANES].astype(F32)

        if maybe_mid:
            mid, = maybe_mid
            f = DEINTERLEAVE_FACTOR
            for c in range(ATTN_LANE_CHUNKS):
                for r in range(f):
                    mid[c * f + r] = stage[c, pl.ds(r, ATTN_SLAB // f, stride=f), :]

            def body(bi, carry):
                for c in range(ATTN_LANE_CHUNKS):
                    dst[bi + skip * (bi + 1), c] = (
                        mid[c * f + bi % f, pl.ds(bi // f, blk, stride=f), :].astype(BF16))
                return carry
        else:
            def body(bi, carry):
                res = bi // per_res
                start = res + dil * blk * (bi % per_res)
                for c in range(ATTN_LANE_CHUNKS):
                    dst[bi + skip * (res + 1), c] = stage[c, pl.ds(start, blk, stride=dil), :].astype(BF16)
                return carry

        lax.fori_loop(0, ATTN_SLAB_BLOCKS, body, 0)

    keep_previous(k_de)
    keep_previous(v_de)
    deinterleave(k_ref, k_de, 1)
    deinterleave(v_ref, v_de, 1)
    deinterleave(q_ref, q_de, 0)

    lane = lax.broadcasted_iota(jnp.int32, (blk, LANES), 1)
    even = lane < ATTN_HEAD_DIM
    qi = lax.broadcasted_iota(jnp.int32, (blk, 2 * blk), 0)
    ki = lax.broadcasted_iota(jnp.int32, (blk, 2 * blk), 1)
    dist = blk + qi - ki
    band = (dist >= 0) & (dist <= n_back)

    def attend(bi, carry):
        res = bi // per_res
        j = bi % per_res
        key_lo = jnp.where(jnp.logical_and(first_slab, j == 0), blk, 0)
        mask = band & (ki >= key_lo)
        kv = res * run + j
        start = res + dil * blk * j
        lse_tile = jnp.zeros((blk, LANES), F32)
        for hp in range(ATTN_LANE_CHUNKS):
            q_pair = q_de[bi, hp]
            k2 = jnp.concatenate([k_de[kv, hp], k_de[kv + 1, hp]], axis=0)
            v2 = jnp.concatenate([v_de[kv, hp], v_de[kv + 1, hp]], axis=0)
            v2 = jnp.concatenate([v2, jnp.ones_like(v2)], axis=1)
            halves = []
            for e, sel in enumerate((even, jnp.logical_not(even))):
                qm = jnp.where(sel, q_pair, jnp.zeros_like(q_pair))
                s = lax.dot_general(qm, k2, (((1,), (1,)), ((), ())), preferred_element_type=F32)
                s = jnp.where(mask, s, -jnp.inf)
                m = jnp.max(s, axis=-1, keepdims=True)
                p = jnp.exp(s - m)
                pv = jnp.dot(p.astype(BF16), v2, preferred_element_type=F32)
                den = pv[:, LANES:]
                halves.append(pv[:, :LANES] * (1.0 / den))
                lse_tile = jnp.where(lane == 2 * hp + e, m + jnp.log(den), lse_tile)
            stage[hp, pl.ds(start, blk, stride=dil), :] = jnp.where(even, halves[0], halves[1])
        lse_ref[pl.ds(start, blk, stride=dil), :] = lse_tile
        return carry

    lax.fori_loop(0, ATTN_SLAB_BLOCKS, attend, 0, unroll=ATTN_UNROLL)
    for c in range(ATTN_LANE_CHUNKS):
        o_ref[:, c * LANES:(c + 1) * LANES] = stage[c].astype(o_ref.dtype)


def _attn_group(qkv, gi, batch, seq):
    dil = ATTN_DILATIONS[gi]
    n_back = ATTN_WINDOWS[gi] // dil
    assert n_back <= ATTN_BLOCK and ATTN_SLAB_BLOCKS % dil == 0 and seq % ATTN_SLAB == 0
    slabs = seq // ATTN_SLAB

    def cur(which):
        return pl.BlockSpec((ATTN_SLAB, ATTN_GROUP_WIDTH), lambda b, s: (b * slabs + s, which * ATTN_GROUPS + gi))

    run = ATTN_SLAB_BLOCKS // dil + 1
    de_block = (ATTN_LANE_CHUNKS, ATTN_BLOCK, LANES)
    f = DEINTERLEAVE_FACTOR
    two_pass = dil == f * f and dil == ATTN_SLAB_BLOCKS
    two_pass_scratch = [pltpu.VMEM((ATTN_LANE_CHUNKS * f, ATTN_SLAB // f, LANES), F32)] if two_pass else []
    return pl.pallas_call(
        functools.partial(_attn_kernel, dil=dil, n_back=n_back),
        grid=(batch, slabs),
        in_specs=[cur(0), cur(1), cur(2)],
        out_specs=[pl.BlockSpec((ATTN_SLAB, ATTN_GROUP_WIDTH), lambda b, s: (b * slabs + s, 0)),
                   pl.BlockSpec((ATTN_SLAB, LANES), lambda b, s: (b * slabs + s, 0))],
        out_shape=[jax.ShapeDtypeStruct((batch * seq, ATTN_GROUP_WIDTH), BF16),
                   jax.ShapeDtypeStruct((batch * seq, LANES), F32)],
        scratch_shapes=[
            pltpu.VMEM((ATTN_LANE_CHUNKS, ATTN_SLAB, LANES), F32),
            pltpu.VMEM((ATTN_SLAB_BLOCKS,) + de_block, BF16),
            pltpu.VMEM((dil * run,) + de_block, BF16),
            pltpu.VMEM((dil * run,) + de_block, BF16),
        ] + two_pass_scratch,
        compiler_params=pltpu.CompilerParams(
            dimension_semantics=("arbitrary", "arbitrary"), vmem_limit_bytes=VMEM_LIMIT),
        name=f"attn_g{gi}",
    )(qkv, qkv, qkv)


def _expand_heads(w, hexp):
    return jnp.dot(jnp.concatenate(_split_bf16(w, 2), axis=1), hexp, preferred_element_type=F32)


def _merge_kernel(x_ref, yssd_ref, gz_ref, o0_ref, o1_ref, o2_ref, l0_ref, l1_ref, l2_ref, gssd_ref, gattn_ref,
                  ng_ref, wssd_ref, wattn_ref, wout_ref, hexp_ref, g2_ref, x1_ref, h2_ref):
    l0, l1, l2 = l0_ref[...], l1_ref[...], l2_ref[...]
    mx = jnp.maximum(jnp.maximum(l0, l1), l2)
    e0, e1, e2 = jnp.exp(l0 - mx), jnp.exp(l1 - mx), jnp.exp(l2 - mx)
    inv = 1.0 / (e0 + e1 + e2)
    y_attn = jnp.zeros(o0_ref.shape, F32)
    for e, o_ref in ((e0, o0_ref), (e1, o1_ref), (e2, o2_ref)):
        y_attn = y_attn + _expand_heads(e * inv, hexp_ref[...]) * o_ref[...].astype(F32)
    p_ssd = None
    for g in range(SSD_GROUPS):
        gs = slice(g * SSD_GROUP_WIDTH, (g + 1) * SSD_GROUP_WIDTH)
        y = yssd_ref[:, gs].astype(F32) * gz_ref[:, gs].astype(F32)
        yn = _rms_rows(y, ng_ref[:, gs]).astype(BF16)
        part = jnp.dot(yn, wssd_ref[gs, :], preferred_element_type=F32)
        p_ssd = part if p_ssd is None else p_ssd + part
    p_attn = jnp.dot(y_attn.astype(BF16), wattn_ref[...], preferred_element_type=F32)
    merged = (_sigmoid(gssd_ref[...].astype(F32)) * p_ssd
              + _sigmoid(gattn_ref[...].astype(F32)) * p_attn)
    x1 = x_ref[...] + jnp.dot(merged.astype(BF16), wout_ref[...], preferred_element_type=F32)
    x1_ref[...] = x1
    h2_ref[...] = _rms_rows(x1, g2_ref[...]).astype(h2_ref.dtype)


def _merge(x2, y_ssd, outs, lses, proj, ssd_norm_g, w_ssd, w_attn, w_out, head_expand, g2, tm):
    m = x2.shape[0]
    rows = lambda width, cb=0: pl.BlockSpec((tm, width), lambda i: (i, cb))
    const = lambda shape: pl.BlockSpec(shape, lambda i: (0, 0))
    return pl.pallas_call(
        _merge_kernel,
        grid=(m // tm,),
        in_specs=[rows(D_MODEL), rows(D_INNER), rows(D_INNER, COL_Z // D_INNER),
                  rows(ATTN_GROUP_WIDTH), rows(ATTN_GROUP_WIDTH), rows(ATTN_GROUP_WIDTH),
                  rows(LANES), rows(LANES), rows(LANES),
                  rows(D_MODEL, COL_GSSD // D_MODEL), rows(D_MODEL, COL_GATTN // D_MODEL),
                  const((1, D_INNER)), const((D_INNER, D_MODEL)), const((ATTN_GROUP_WIDTH, D_MODEL)),
                  const((D_MODEL, D_MODEL)),
                  const((2 * LANES, ATTN_GROUP_WIDTH)), const((1, D_MODEL))],
        out_specs=[rows(D_MODEL), rows(D_MODEL)],
        out_shape=[jax.ShapeDtypeStruct((m, D_MODEL), F32), jax.ShapeDtypeStruct((m, D_MODEL), BF16)],
        compiler_params=pltpu.CompilerParams(
            dimension_semantics=("arbitrary",), vmem_limit_bytes=VMEM_LIMIT),
        name="merge",
    )(x2, y_ssd, proj, outs[0], outs[1], outs[2], lses[0], lses[1], lses[2], proj, proj,
      ssd_norm_g, w_ssd, w_attn, w_out, head_expand, g2)


def _ffn_kernel(h_ref, x1_ref, wg_ref, wv_ref, cwg_ref, cwv_ref, cbg_ref, cbv_ref, wd_ref, out_ref,
                ug_scr, uv_scr, halo_g, halo_v, *, tiles_per_seq):
    i = pl.program_id(0)
    tm = h_ref.shape[0]
    tf = wg_ref.shape[1]
    first = (i % tiles_per_seq) == 0

    @pl.when(i == 0)
    def _():
        halo_g[...] = jnp.zeros(halo_g.shape, F32)
        halo_v[...] = jnp.zeros(halo_v.shape, F32)

    def conv(u, u_scr, halo, w_ref, b_ref, c, scale):
        cs = slice(c * LANES, (c + 1) * LANES)
        u_scr[c, 0:HALO, :] = jnp.where(first, 0.0, halo[c])
        u_scr[c, HALO:HALO + tm, :] = u
        acc = scale * b_ref[:, cs]
        for t in range(FFN_CONV):
            tap = u_scr[c, pl.ds(HALO - (FFN_CONV - 1) + t, tm, stride=1), :]
            acc = acc + tap * (scale * w_ref[t:t + 1, cs])
        halo[c] = u_scr[c, tm:tm + HALO, :]
        return acc

    per = FFN_CHUNK // LANES
    total = None
    for fc in range(tf // FFN_CHUNK):
        cols = slice(fc * FFN_CHUNK, (fc + 1) * FFN_CHUNK)
        ug = jnp.dot(h_ref[...], wg_ref[:, cols], preferred_element_type=F32)
        uv = jnp.dot(h_ref[...], wv_ref[:, cols], preferred_element_type=F32)
        acts = []
        for ci in range(per):
            c = fc * per + ci
            ls = slice(ci * LANES, (ci + 1) * LANES)
            gate = conv(ug[:, ls], ug_scr, halo_g, cwg_ref, cbg_ref, c, 0.5)
            val = conv(uv[:, ls], uv_scr, halo_v, cwv_ref, cbv_ref, c, 1.0)
            acts.append((_silu_of_half(gate) * val).astype(BF16))
        part = jnp.dot(jnp.concatenate(acts, axis=1), wd_ref[cols, :], preferred_element_type=F32)
        total = part if total is None else total + part
    out_ref[...] = x1_ref[...] + total


def _ffn(h2, x1, w_up, conv_w, conv_b, w_down, tm, seq):
    m = h2.shape[0]
    tf = D_FF
    once = dict(pipeline_mode=pl.Buffered(1))
    gate_half = lambda i: (0, 0)
    value_half = lambda i: (0, 1)
    return pl.pallas_call(
        functools.partial(_ffn_kernel, tiles_per_seq=seq // tm),
        grid=(m // tm,),
        in_specs=[
            pl.BlockSpec((tm, D_MODEL), lambda i: (i, 0)),
            pl.BlockSpec((tm, D_MODEL), lambda i: (i, 0)),
            pl.BlockSpec((D_MODEL, tf), gate_half, **once),
            pl.BlockSpec((D_MODEL, tf), value_half, **once),
            pl.BlockSpec((FFN_CONV, tf), gate_half, **once),
            pl.BlockSpec((FFN_CONV, tf), value_half, **once),
            pl.BlockSpec((1, tf), gate_half, **once),
            pl.BlockSpec((1, tf), value_half, **once),
            pl.BlockSpec((tf, D_MODEL), lambda i: (0, 0), **once),
        ],
        out_specs=pl.BlockSpec((tm, D_MODEL), lambda i: (i, 0)),
        out_shape=jax.ShapeDtypeStruct((m, D_MODEL), F32),
        scratch_shapes=[
            pltpu.VMEM((tf // LANES, tm + HALO, LANES), F32),
            pltpu.VMEM((tf // LANES, tm + HALO, LANES), F32),
            pltpu.VMEM((tf // LANES, HALO, LANES), F32),
            pltpu.VMEM((tf // LANES, HALO, LANES), F32),
        ],
        compiler_params=pltpu.CompilerParams(
            dimension_semantics=("arbitrary",), vmem_limit_bytes=VMEM_LIMIT),
        name="ffn",
    )(h2, x1, w_up, w_up, conv_w, conv_w, conv_b, conv_b, w_down)


def _pad_lanes(v):
    return jnp.pad(v.astype(F32), (0, LANES - v.shape[0])).reshape(1, LANES)


def kernel(x, norm1_g, w_in, ssd_conv_w, ssd_conv_b, dt_bias, a_log, d_skip, ssd_norm_g, w_ssd_proj,
           q_norm_g, k_norm_g, w_attn_proj, w_out, norm2_g, w_up, ffn_conv_w, ffn_conv_b, w_down):
    batch, seq, _ = x.shape
    assert norm1_g.shape[0] == 1, "single-layer block"
    x2 = x.reshape(batch * seq, D_MODEL)
    g1 = norm1_g[0].reshape(1, D_MODEL)

    w = w_in[0]
    o_xbc = D_INNER
    o_dt = o_xbc + SSD_CONV_DIM
    o_q = o_dt + SSD_HEADS
    o_g = o_q + 3 * ATTN_WIDTH
    w_main = jnp.concatenate([w[:, o_xbc:o_dt], w[:, :o_xbc], w[:, o_g:]], axis=1).astype(BF16)
    w_dt = jnp.pad(w[:, o_dt:o_q], ((0, 0), (0, LANES - SSD_HEADS))).astype(BF16)
    w_qkv = w[:, o_q:o_g].astype(BF16)

    proj, dt_raw = _in_proj_main(x2, g1, w_main, w_dt, ssd_conv_w[0], ssd_conv_b[0].reshape(1, -1),
                                 tm=TM_IN_PROJ, tn=TN_IN_PROJ, seq=seq)

    heads_per_tile = MXU_WIDTH // ATTN_HEAD_DIM
    qk_gains = jnp.stack([jnp.tile(q_norm_g[0] * (ATTN_HEAD_DIM ** -0.5), heads_per_tile),
                          jnp.tile(k_norm_g[0], heads_per_tile)]).astype(F32)
    lane_head = jnp.arange(MXU_WIDTH) // ATTN_HEAD_DIM
    seg = (lane_head[:, None] == lane_head[None, :]).astype(BF16)
    qkv = _in_proj_qkv(x2, g1, w_qkv, qk_gains, seg, tm=TM_IN_PROJ)

    tri = jnp.tril(jnp.ones((SSD_CHUNK, SSD_CHUNK), BF16))
    head_of_col = jnp.arange(D_INNER) // SSD_HEAD_DIM
    expand = (jnp.arange(LANES)[:, None] == head_of_col[None, :]).astype(BF16)
    y_ssd = _ssd(proj, dt_raw, _pad_lanes(dt_bias[0]), _pad_lanes(a_log[0]),
                 jnp.repeat(d_skip[0], SSD_HEAD_DIM).reshape(1, D_INNER),
                 tri, jnp.concatenate([expand, expand], axis=0),
                 batch, seq, chunks_per_step=SSD_CHUNKS_PER_STEP)

    outs, lses = zip(*[_attn_group(qkv, gi, batch, seq) for gi in range(ATTN_GROUPS)])

    attn_head_of_col = jnp.arange(ATTN_GROUP_WIDTH) // ATTN_HEAD_DIM
    head_expand = (jnp.arange(2 * LANES)[:, None] % LANES == attn_head_of_col[None, :]).astype(BF16)
    x1, h2 = _merge(x2, y_ssd, outs, lses, proj, ssd_norm_g[0].reshape(1, D_INNER),
                    w_ssd_proj[0].astype(BF16), w_attn_proj[0].astype(BF16), w_out[0].astype(BF16),
                    head_expand, norm2_g[0].reshape(1, D_MODEL), tm=TM_MERGE)

    out = _ffn(h2, x1, w_up[0].astype(BF16), ffn_conv_w[0], ffn_conv_b[0].reshape(1, -1),
               w_down[0].astype(BF16), tm=TM_FFN, seq=seq)
    return out.reshape(batch, seq, D_MODEL)
```

```python
import functools

import jax
import jax.numpy as jnp
from jax import lax
from jax.experimental import pallas as pl
from jax.experimental.pallas import tpu as pltpu

F32 = jnp.float32
BF16 = jnp.bfloat16

D_MODEL = 1024
D_INNER = 2048
SSD_HEADS = 32
SSD_HEAD_DIM = 64
SSD_GROUPS = 8
SSD_HEADS_PER_GROUP = SSD_HEADS // SSD_GROUPS
D_STATE = 128
SSD_CONV = 4
SSD_CHUNK = 128
SSD_CONV_DIM = D_INNER + 2 * SSD_GROUPS * D_STATE
SSD_GROUP_WIDTH = D_INNER // SSD_GROUPS
ATTN_WINDOWS = (128, 512, 2048)
ATTN_DILATIONS = (1, 4, 16)
ATTN_GROUPS = 3
ATTN_HEADS = 8
ATTN_HEAD_DIM = 64
ATTN_BLOCK = 128
ATTN_GROUP_WIDTH = ATTN_HEADS * ATTN_HEAD_DIM
ATTN_WIDTH = ATTN_GROUPS * ATTN_GROUP_WIDTH
D_FF = 2816
FFN_CONV = 3
FFN_CHUNK = D_FF // 2
EPS = 1e-6
LANES = 128
MXU_WIDTH = 256
HALO = 8
ATTN_SLAB = ATTN_BLOCK * max(ATTN_DILATIONS)
ATTN_SLAB_BLOCKS = ATTN_SLAB // ATTN_BLOCK
ATTN_LANE_CHUNKS = ATTN_GROUP_WIDTH // LANES
ATTN_UNROLL = 16
DEINTERLEAVE_FACTOR = 4

COL_XBC = 0
COL_Z = COL_XBC + SSD_CONV_DIM
COL_GSSD = COL_Z + D_INNER
COL_GATTN = COL_GSSD + D_MODEL
PROJ_WIDTH = COL_GATTN + D_MODEL

V7X_VMEM_BYTES = 64 * 1024 * 1024
VMEM_LIMIT = V7X_VMEM_BYTES * 7 // 8

TM_IN_PROJ = 1024
TN_IN_PROJ = 2048
TM_MERGE = 512
TM_FFN = 512
SSD_CHUNKS_PER_STEP = 4


def _sigmoid(v):
    return 0.5 * jnp.tanh(0.5 * v) + 0.5


def _silu_of_half(hv):
    return hv + hv * jnp.tanh(hv)


def _split_bf16(v, terms):
    parts = []
    for _ in range(terms):
        p = v.astype(BF16)
        parts.append(p)
        v = v - p.astype(F32)
    return parts


def _dot_split(parts, rhs, lhs_side=True):
    acc = None
    for p in parts:
        t = (jnp.dot(p, rhs, preferred_element_type=F32) if lhs_side
             else jnp.dot(rhs, p, preferred_element_type=F32))
        acc = t if acc is None else acc + t
    return acc


def _rms_rows(x, gain):
    ms = jnp.mean(x * x, axis=-1, keepdims=True)
    return x * lax.rsqrt(ms + EPS) * gain


def _in_proj_main_kernel(x_ref, g_ref, w_ref, wgate_ref, wdt_ref, cw_ref, cb_ref, proj_ref, dt_ref, h_ref,
                         xp_scr, halo_scr, *, tiles_per_seq):
    i = pl.program_id(0)
    j = pl.program_id(1)
    tm, tn = proj_ref.shape
    conv_tiles = SSD_CONV_DIM // tn
    z_tiles = D_INNER // tn

    @pl.when(j == 0)
    def _():
        h = _rms_rows(x_ref[...], g_ref[...]).astype(BF16)
        h_ref[...] = h
        dt_ref[...] = jnp.dot(h, wdt_ref[...], preferred_element_type=F32)

    @pl.when(jnp.logical_and(i == 0, j == 0))
    def _():
        halo_scr[...] = jnp.zeros(halo_scr.shape, F32)

    @pl.when(j < conv_tiles)
    def _():
        r = jnp.dot(h_ref[...], w_ref[...], preferred_element_type=F32)
        first = (i % tiles_per_seq) == 0
        for c in range(tn // LANES):
            cs = slice(c * LANES, (c + 1) * LANES)
            xp_scr[c, 0:HALO, :] = jnp.where(first, 0.0, halo_scr[j, c])
            xp_scr[c, HALO:HALO + tm, :] = r[:, cs]
            acc = 0.5 * cb_ref[:, cs]
            for t in range(SSD_CONV):
                tap = xp_scr[c, pl.ds(HALO - (SSD_CONV - 1) + t, tm, stride=1), :]
                acc = acc + tap * (0.5 * cw_ref[t:t + 1, cs])
            proj_ref[:, cs] = _silu_of_half(acc).astype(proj_ref.dtype)
            halo_scr[j, c] = xp_scr[c, tm:tm + HALO, :]

    @pl.when(jnp.logical_and(j >= conv_tiles, j < conv_tiles + z_tiles))
    def _():
        r = jnp.dot(h_ref[...], w_ref[...], preferred_element_type=F32)
        proj_ref[...] = _silu_of_half(0.5 * r).astype(proj_ref.dtype)

    @pl.when(j >= conv_tiles + z_tiles)
    def _():
        proj_ref[...] = jnp.dot(h_ref[...], wgate_ref[...], preferred_element_type=F32).astype(proj_ref.dtype)


def _in_proj_main(x2, g1, w_zx, w_gate, w_dt, conv_w, conv_b, tm, tn, seq):
    m = x2.shape[0]
    assert SSD_CONV_DIM % tn == 0 and D_INNER % tn == 0 and seq % tm == 0 and w_gate.shape[1] == tn
    conv_tiles = SSD_CONV_DIM // tn
    z_tiles = D_INNER // tn
    conv_col = lambda i, j: (0, jnp.minimum(j, conv_tiles - 1))
    zx_col = lambda i, j: (0, jnp.where(j < conv_tiles, j + z_tiles, jnp.minimum(j - conv_tiles, z_tiles - 1)))
    return pl.pallas_call(
        functools.partial(_in_proj_main_kernel, tiles_per_seq=seq // tm),
        grid=(m // tm, PROJ_WIDTH // tn),
        in_specs=[
            pl.BlockSpec((tm, D_MODEL), lambda i, j: (i, 0)),
            pl.BlockSpec((1, D_MODEL), lambda i, j: (0, 0)),
            pl.BlockSpec((D_MODEL, tn), zx_col),
            pl.BlockSpec((D_MODEL, tn), lambda i, j: (0, 0), pipeline_mode=pl.Buffered(1)),
            pl.BlockSpec((D_MODEL, LANES), lambda i, j: (0, 0)),
            pl.BlockSpec((SSD_CONV, tn), conv_col),
            pl.BlockSpec((1, tn), conv_col),
        ],
        out_specs=[
            pl.BlockSpec((tm, tn), lambda i, j: (i, j)),
            pl.BlockSpec((tm, LANES), lambda i, j: (i, 0)),
            pl.BlockSpec((tm, D_MODEL), lambda i, j: (i, 0)),
        ],
        out_shape=[
            jax.ShapeDtypeStruct((m, PROJ_WIDTH), BF16),
            jax.ShapeDtypeStruct((m, LANES), F32),
            jax.ShapeDtypeStruct((m, D_MODEL), BF16),
        ],
        scratch_shapes=[
            pltpu.VMEM((tn // LANES, HALO + tm, LANES), F32),
            pltpu.VMEM((conv_tiles, tn // LANES, HALO, LANES), F32),
        ],
        compiler_params=pltpu.CompilerParams(
            dimension_semantics=("arbitrary", "arbitrary"), vmem_limit_bytes=VMEM_LIMIT),
        name="in_proj_main",
    )(x2, g1, w_zx, w_gate, w_dt, conv_w, conv_b)


def _in_proj_qkv_kernel(h_ref, w_ref, gain_ref, seg_ref, out_ref, t_scr):
    j = pl.program_id(1)
    seg_w = seg_ref.shape[0]

    @pl.when(j < 2)
    def _():
        t_scr[...] = jnp.dot(h_ref[...], w_ref[...], preferred_element_type=F32)
        gain = gain_ref[pl.ds(j, 1), :]
        for c in range(ATTN_WIDTH // seg_w):
            cs = slice(c * seg_w, (c + 1) * seg_w)
            t = t_scr[:, cs]
            ss = jnp.dot((t * t).astype(BF16), seg_ref[...], preferred_element_type=F32)
            out_ref[:, cs] = (t * lax.rsqrt(ss * (1.0 / ATTN_HEAD_DIM) + EPS) * gain).astype(out_ref.dtype)

    @pl.when(j == 2)
    def _():
        out_ref[...] = jnp.dot(h_ref[...], w_ref[...], preferred_element_type=F32).astype(out_ref.dtype)


def _in_proj_qkv(h, w_qkv, gains, seg, tm):
    m = h.shape[0]
    return pl.pallas_call(
        _in_proj_qkv_kernel,
        grid=(m // tm, 3),
        in_specs=[
            pl.BlockSpec((tm, D_MODEL), lambda i, j: (i, 0)),
            pl.BlockSpec((D_MODEL, ATTN_WIDTH), lambda i, j: (0, j)),
            pl.BlockSpec(gains.shape, lambda i, j: (0, 0)),
            pl.BlockSpec(seg.shape, lambda i, j: (0, 0)),
        ],
        out_specs=pl.BlockSpec((tm, ATTN_WIDTH), lambda i, j: (i, j)),
        out_shape=jax.ShapeDtypeStruct((m, 3 * ATTN_WIDTH), BF16),
        scratch_shapes=[pltpu.VMEM((tm, ATTN_WIDTH), F32)],
        compiler_params=pltpu.CompilerParams(
            dimension_semantics=("arbitrary", "arbitrary"), vmem_limit_bytes=VMEM_LIMIT),
        name="in_proj_qkv",
    )(h, w_qkv, gains, seg)


def _ssd_chunk(act, dt_raw, dtb, a_log, dskip, tri, expand2, y_ref, rows, state_scr):
    q = SSD_CHUNK
    dt = jax.nn.softplus(dt_raw + dtb)
    a = dt * (-jnp.exp(a_log))
    a_cs = _dot_split(_split_bf16(a, 3), tri, lhs_side=False)
    a_cs_t_adj = a_cs.T - jnp.log(dt.T)
    e_in = jnp.exp(a_cs)
    w_out = jnp.exp(a_cs[q - 1:q, :] - a_cs) * dt
    lhs = jnp.concatenate([jnp.concatenate(_split_bf16(e_in, 2), axis=1),
                           jnp.concatenate(_split_bf16(w_out, 2), axis=1)], axis=0)
    both = jnp.dot(lhs, expand2, preferred_element_type=F32)
    e_exp, w_exp = both[:q], both[q:]

    li = lax.broadcasted_iota(jnp.int32, (q, q), 0)
    si = lax.broadcasted_iota(jnp.int32, (q, q), 1)
    causal = li >= si
    head_of_lane = lax.broadcasted_iota(jnp.int32, (q, SSD_GROUP_WIDTH), 1) // SSD_HEAD_DIM

    for g in range(SSD_GROUPS):
        gs = slice(g * SSD_GROUP_WIDTH, (g + 1) * SSD_GROUP_WIDTH)
        xs_b = act[:, gs]
        xs = xs_b.astype(F32)
        b_g = act[:, D_INNER + g * D_STATE:D_INNER + (g + 1) * D_STATE]
        c_g = act[:, D_INNER + (SSD_GROUPS + g) * D_STATE:D_INNER + (SSD_GROUPS + g + 1) * D_STATE]
        cb = lax.dot_general(c_g, b_g, (((1,), (1,)), ((), ())), preferred_element_type=F32)
        st = state_scr[g]
        y_off = jnp.dot(c_g, st.astype(BF16), preferred_element_type=F32) * e_exp[:, gs]
        mms, blocks = [], []
        for k in range(SSD_HEADS_PER_GROUP):
            h = g * SSD_HEADS_PER_GROUP + k
            seg = a_cs[:, h:h + 1] - a_cs_t_adj[h:h + 1, :]
            mms.append((cb * jnp.exp(jnp.where(causal, seg, -jnp.inf))).astype(BF16))
            blocks.append(jnp.where(head_of_lane == k, xs_b, jnp.zeros_like(xs_b)))
        y_diag = jnp.dot(jnp.concatenate(mms, axis=1), jnp.concatenate(blocks, axis=0),
                         preferred_element_type=F32)
        y = y_diag + y_off + dskip[:, gs] * xs
        y_ref[rows, gs] = y.astype(y_ref.dtype)
        xw = (xs * w_exp[:, gs]).astype(BF16)
        upd = lax.dot_general(b_g, xw, (((0,), (0,)), ((), ())), preferred_element_type=F32)
        state_scr[g] = st * e_exp[q - 1:q, gs] + upd


def _ssd_kernel(act_ref, dt_ref, dtb_ref, alog_ref, dskip_ref, tri_ref, expand2_ref, y_ref, state_scr):
    @pl.when(pl.program_id(1) == 0)
    def _():
        state_scr[...] = jnp.zeros(state_scr.shape, F32)

    for ci in range(act_ref.shape[0] // SSD_CHUNK):
        rows = slice(ci * SSD_CHUNK, (ci + 1) * SSD_CHUNK)
        _ssd_chunk(act_ref[rows, :], dt_ref[rows, :], dtb_ref[...], alog_ref[...], dskip_ref[...],
                   tri_ref[...], expand2_ref[...], y_ref, rows, state_scr)


def _ssd(proj, dt_raw, dt_bias, a_log, d_skip_exp, tri, expand2, batch, seq, chunks_per_step):
    rows = SSD_CHUNK * chunks_per_step
    steps = seq // rows
    row = lambda b, c: b * steps + c
    const = lambda b, c: (0, 0)
    return pl.pallas_call(
        _ssd_kernel,
        grid=(batch, steps),
        in_specs=[
            pl.BlockSpec((rows, SSD_CONV_DIM), lambda b, c: (row(b, c), COL_XBC // SSD_CONV_DIM)),
            pl.BlockSpec((rows, LANES), lambda b, c: (row(b, c), 0)),
            pl.BlockSpec((1, LANES), const),
            pl.BlockSpec((1, LANES), const),
            pl.BlockSpec((1, D_INNER), const),
            pl.BlockSpec((SSD_CHUNK, SSD_CHUNK), const),
            pl.BlockSpec((2 * LANES, D_INNER), const),
        ],
        out_specs=pl.BlockSpec((rows, D_INNER), lambda b, c: (row(b, c), 0)),
        out_shape=jax.ShapeDtypeStruct((batch * seq, D_INNER), BF16),
        scratch_shapes=[pltpu.VMEM((SSD_GROUPS, D_STATE, SSD_GROUP_WIDTH), F32)],
        compiler_params=pltpu.CompilerParams(
            dimension_semantics=("arbitrary", "arbitrary"), vmem_limit_bytes=VMEM_LIMIT),
        name="ssd",
    )(proj, dt_raw, dt_bias, a_log, d_skip_exp, tri, expand2)


def _attn_kernel(q_ref, k_ref, v_ref, o_ref, lse_ref, stage, q_de, k_de, v_de, *maybe_mid, dil, n_back):
    blk = ATTN_BLOCK
    per_res = ATTN_SLAB_BLOCKS // dil
    run = per_res + 1
    first_slab = pl.program_id(1) == 0

    def keep_previous(dst):
        @pl.when(first_slab)
        def _():
            def body(res, carry):
                dst[res * run] = jnp.zeros(dst.shape[1:], BF16)
                return carry
            lax.fori_loop(0, dil, body, 0)

        @pl.when(jnp.logical_not(first_slab))
        def _():
            def body(res, carry):
                dst[res * run] = dst[res * run + per_res]
                return carry
            lax.fori_loop(0, dil, body, 0)

    def deinterleave(src_ref, dst, skip):
        if dil == 1:
            def copy(bi, carry):
                rows = pl.ds(pl.multiple_of(bi * blk, blk), blk)
                for c in range(ATTN_LANE_CHUNKS):
                    dst[bi + skip, c] = src_ref[rows, c * LANES:(c + 1) * LANES]
                return carry

            lax.fori_loop(0, ATTN_SLAB_BLOCKS, copy, 0)
            return

        for c in range(ATTN_LANE_CHUNKS):
            stage[c] = src_ref[:, c * LANES:(c + 1) * LANES].astype(F32)

        if maybe_mid:
            mid, = maybe_mid
            f = DEINTERLEAVE_FACTOR
            for c in range(ATTN_LANE_CHUNKS):
                for r in range(f):
                    mid[c * f + r] = stage[c, pl.ds(r, ATTN_SLAB // f, stride=f), :]

            def body(bi, carry):
                for c in range(ATTN_LANE_CHUNKS):
                    dst[bi + skip * (bi + 1), c] = (
                        mid[c * f + bi % f, pl.ds(bi // f, blk, stride=f), :].astype(BF16))
                return carry
        else:
            def body(bi, carry):
                res = bi // per_res
                start = res + dil * blk * (bi % per_res)
                for c in range(ATTN_LANE_CHUNKS):
                    dst[bi + skip * (res + 1), c] = stage[c, pl.ds(start, blk, stride=dil), :].astype(BF16)
                return carry

        lax.fori_loop(0, ATTN_SLAB_BLOCKS, body, 0)

    keep_previous(k_de)
    keep_previous(v_de)
    deinterleave(k_ref, k_de, 1)
    deinterleave(v_ref, v_de, 1)
    deinterleave(q_ref, q_de, 0)

    lane = lax.broadcasted_iota(jnp.int32, (blk, LANES), 1)
    even = lane < ATTN_HEAD_DIM
    qi = lax.broadcasted_iota(jnp.int32, (blk, 2 * blk), 0)
    ki = lax.broadcasted_iota(jnp.int32, (blk, 2 * blk), 1)
    dist = blk + qi - ki
    band = (dist >= 0) & (dist <= n_back)

    def attend(bi, carry):
        res = bi // per_res
        j = bi % per_res
        key_lo = jnp.where(jnp.logical_and(first_slab, j == 0), blk, 0)
        mask = band & (ki >= key_lo)
        kv = res * run + j
        start = res + dil * blk * j
        lse_tile = jnp.zeros((blk, LANES), F32)
        for hp in range(ATTN_LANE_CHUNKS):
            q_pair = q_de[bi, hp]
            k2 = jnp.concatenate([k_de[kv, hp], k_de[kv + 1, hp]], axis=0)
            v2 = jnp.concatenate([v_de[kv, hp], v_de[kv + 1, hp]], axis=0)
            v2 = jnp.concatenate([v2, jnp.ones_like(v2)], axis=1)
            halves = []
            for e, sel in enumerate((even, jnp.logical_not(even))):
                qm = jnp.where(sel, q_pair, jnp.zeros_like(q_pair))
                s = lax.dot_general(qm, k2, (((1,), (1,)), ((), ())), preferred_element_type=F32)
                s = jnp.where(mask, s, -jnp.inf)
                m = jnp.max(s, axis=-1, keepdims=True)
                p = jnp.exp(s - m)
                pv = jnp.dot(p.astype(BF16), v2, preferred_element_type=F32)
                den = pv[:, LANES:]
                halves.append(pv[:, :LANES] * (1.0 / den))
                lse_tile = jnp.where(lane == 2 * hp + e, m + jnp.log(den), lse_tile)
            o_pair = jnp.where(even, halves[0], halves[1])
            if dil == 1:
                rows = pl.ds(pl.multiple_of(start, blk), blk)
                o_ref[rows, hp * LANES:(hp + 1) * LANES] = o_pair.astype(o_ref.dtype)
            else:
                stage[hp, pl.ds(start, blk, stride=dil), :] = o_pair
        lse_ref[pl.ds(start, blk, stride=dil), :] = lse_tile
        return carry

    lax.fori_loop(0, ATTN_SLAB_BLOCKS, attend, 0, unroll=ATTN_UNROLL)
    if dil > 1:
        for c in range(ATTN_LANE_CHUNKS):
            o_ref[:, c * LANES:(c + 1) * LANES] = stage[c].astype(o_ref.dtype)


def _attn_group(qkv, gi, batch, seq):
    dil = ATTN_DILATIONS[gi]
    n_back = ATTN_WINDOWS[gi] // dil
    assert n_back <= ATTN_BLOCK and ATTN_SLAB_BLOCKS % dil == 0 and seq % ATTN_SLAB == 0
    slabs = seq // ATTN_SLAB

    def cur(which):
        return pl.BlockSpec((ATTN_SLAB, ATTN_GROUP_WIDTH), lambda b, s: (b * slabs + s, which * ATTN_GROUPS + gi))

    run = ATTN_SLAB_BLOCKS // dil + 1
    de_block = (ATTN_LANE_CHUNKS, ATTN_BLOCK, LANES)
    f = DEINTERLEAVE_FACTOR
    two_pass = dil == f * f and dil == ATTN_SLAB_BLOCKS
    two_pass_scratch = [pltpu.VMEM((ATTN_LANE_CHUNKS * f, ATTN_SLAB // f, LANES), F32)] if two_pass else []
    return pl.pallas_call(
        functools.partial(_attn_kernel, dil=dil, n_back=n_back),
        grid=(batch, slabs),
        in_specs=[cur(0), cur(1), cur(2)],
        out_specs=[pl.BlockSpec((ATTN_SLAB, ATTN_GROUP_WIDTH), lambda b, s: (b * slabs + s, 0)),
                   pl.BlockSpec((ATTN_SLAB, LANES), lambda b, s: (b * slabs + s, 0))],
        out_shape=[jax.ShapeDtypeStruct((batch * seq, ATTN_GROUP_WIDTH), BF16),
                   jax.ShapeDtypeStruct((batch * seq, LANES), F32)],
        scratch_shapes=[
            pltpu.VMEM((ATTN_LANE_CHUNKS, ATTN_SLAB, LANES), F32),
            pltpu.VMEM((ATTN_SLAB_BLOCKS,) + de_block, BF16),
            pltpu.VMEM((dil * run,) + de_block, BF16),
            pltpu.VMEM((dil * run,) + de_block, BF16),
        ] + two_pass_scratch,
        compiler_params=pltpu.CompilerParams(
            dimension_semantics=("arbitrary", "arbitrary"), vmem_limit_bytes=VMEM_LIMIT),
        name=f"attn_g{gi}",
    )(qkv, qkv, qkv)


def _expand_heads(w, hexp):
    return jnp.dot(jnp.concatenate(_split_bf16(w, 2), axis=1), hexp, preferred_element_type=F32)


def _merge_kernel(x_ref, yssd_ref, gz_ref, o0_ref, o1_ref, o2_ref, l0_ref, l1_ref, l2_ref, gssd_ref, gattn_ref,
                  ng_ref, wssd_ref, wattn_ref, wout_ref, hexp_ref, g2_ref, x1_ref, h2_ref):
    l0, l1, l2 = l0_ref[...], l1_ref[...], l2_ref[...]
    mx = jnp.maximum(jnp.maximum(l0, l1), l2)
    e0, e1, e2 = jnp.exp(l0 - mx), jnp.exp(l1 - mx), jnp.exp(l2 - mx)
    inv = 1.0 / (e0 + e1 + e2)
    y_attn = jnp.zeros(o0_ref.shape, F32)
    for e, o_ref in ((e0, o0_ref), (e1, o1_ref), (e2, o2_ref)):
        y_attn = y_attn + _expand_heads(e * inv, hexp_ref[...]) * o_ref[...].astype(F32)
    p_ssd = None
    for g in range(SSD_GROUPS):
        gs = slice(g * SSD_GROUP_WIDTH, (g + 1) * SSD_GROUP_WIDTH)
        y = yssd_ref[:, gs].astype(F32) * gz_ref[:, gs].astype(F32)
        yn = _rms_rows(y, ng_ref[:, gs]).astype(BF16)
        part = jnp.dot(yn, wssd_ref[gs, :], preferred_element_type=F32)
        p_ssd = part if p_ssd is None else p_ssd + part
    p_attn = jnp.dot(y_attn.astype(BF16), wattn_ref[...], preferred_element_type=F32)
    merged = (_sigmoid(gssd_ref[...].astype(F32)) * p_ssd
              + _sigmoid(gattn_ref[...].astype(F32)) * p_attn)
    x1 = x_ref[...] + jnp.dot(merged.astype(BF16), wout_ref[...], preferred_element_type=F32)
    x1_ref[...] = x1
    h2_ref[...] = _rms_rows(x1, g2_ref[...]).astype(h2_ref.dtype)


def _merge(x2, y_ssd, outs, lses, proj, ssd_norm_g, w_ssd, w_attn, w_out, head_expand, g2, tm):
    m = x2.shape[0]
    rows = lambda width, cb=0: pl.BlockSpec((tm, width), lambda i: (i, cb))
    const = lambda shape: pl.BlockSpec(shape, lambda i: (0, 0))
    return pl.pallas_call(
        _merge_kernel,
        grid=(m // tm,),
        in_specs=[rows(D_MODEL), rows(D_INNER), rows(D_INNER, COL_Z // D_INNER),
                  rows(ATTN_GROUP_WIDTH), rows(ATTN_GROUP_WIDTH), rows(ATTN_GROUP_WIDTH),
                  rows(LANES), rows(LANES), rows(LANES),
                  rows(D_MODEL, COL_GSSD // D_MODEL), rows(D_MODEL, COL_GATTN // D_MODEL),
                  const((1, D_INNER)), const((D_INNER, D_MODEL)), const((ATTN_GROUP_WIDTH, D_MODEL)),
                  const((D_MODEL, D_MODEL)),
                  const((2 * LANES, ATTN_GROUP_WIDTH)), const((1, D_MODEL))],
        out_specs=[rows(D_MODEL), rows(D_MODEL)],
        out_shape=[jax.ShapeDtypeStruct((m, D_MODEL), F32), jax.ShapeDtypeStruct((m, D_MODEL), BF16)],
        compiler_params=pltpu.CompilerParams(
            dimension_semantics=("arbitrary",), vmem_limit_bytes=VMEM_LIMIT),
        name="merge",
    )(x2, y_ssd, proj, outs[0], outs[1], outs[2], lses[0], lses[1], lses[2], proj, proj,
      ssd_norm_g, w_ssd, w_attn, w_out, head_expand, g2)


def _ffn_kernel(h_ref, x1_ref, wg_ref, wv_ref, cwg_ref, cwv_ref, cbg_ref, cbv_ref, wd_ref, out_ref,
                ug_scr, uv_scr, halo_g, halo_v, *, tiles_per_seq):
    i = pl.program_id(0)
    tm = h_ref.shape[0]
    tf = wg_ref.shape[1]
    first = (i % tiles_per_seq) == 0

    @pl.when(i == 0)
    def _():
        halo_g[...] = jnp.zeros(halo_g.shape, F32)
        halo_v[...] = jnp.zeros(halo_v.shape, F32)

    def conv(u, u_scr, halo, w_ref, b_ref, c, scale):
        cs = slice(c * LANES, (c + 1) * LANES)
        u_scr[c, 0:HALO, :] = jnp.where(first, 0.0, halo[c])
        u_scr[c, HALO:HALO + tm, :] = u
        acc = scale * b_ref[:, cs]
        for t in range(FFN_CONV):
            tap = u_scr[c, pl.ds(HALO - (FFN_CONV - 1) + t, tm, stride=1), :]
            acc = acc + tap * (scale * w_ref[t:t + 1, cs])
        halo[c] = u_scr[c, tm:tm + HALO, :]
        return acc

    per = FFN_CHUNK // LANES
    total = None
    for fc in range(tf // FFN_CHUNK):
        cols = slice(fc * FFN_CHUNK, (fc + 1) * FFN_CHUNK)
        ug = jnp.dot(h_ref[...], wg_ref[:, cols], preferred_element_type=F32)
        uv = jnp.dot(h_ref[...], wv_ref[:, cols], preferred_element_type=F32)
        acts = []
        for ci in range(per):
            c = fc * per + ci
            ls = slice(ci * LANES, (ci + 1) * LANES)
            gate = conv(ug[:, ls], ug_scr, halo_g, cwg_ref, cbg_ref, c, 0.5)
            val = conv(uv[:, ls], uv_scr, halo_v, cwv_ref, cbv_ref, c, 1.0)
            acts.append((_silu_of_half(gate) * val).astype(BF16))
        part = jnp.dot(jnp.concatenate(acts, axis=1), wd_ref[cols, :], preferred_element_type=F32)
        total = part if total is None else total + part
    out_ref[...] = x1_ref[...] + total


def _ffn(h2, x1, w_up, conv_w, conv_b, w_down, tm, seq):
    m = h2.shape[0]
    tf = D_FF
    once = dict(pipeline_mode=pl.Buffered(1))
    gate_half = lambda i: (0, 0)
    value_half = lambda i: (0, 1)
    return pl.pallas_call(
        functools.partial(_ffn_kernel, tiles_per_seq=seq // tm),
        grid=(m // tm,),
        in_specs=[
            pl.BlockSpec((tm, D_MODEL), lambda i: (i, 0)),
            pl.BlockSpec((tm, D_MODEL), lambda i: (i, 0)),
            pl.BlockSpec((D_MODEL, tf), gate_half, **once),
            pl.BlockSpec((D_MODEL, tf), value_half, **once),
            pl.BlockSpec((FFN_CONV, tf), gate_half, **once),
            pl.BlockSpec((FFN_CONV, tf), value_half, **once),
            pl.BlockSpec((1, tf), gate_half, **once),
            pl.BlockSpec((1, tf), value_half, **once),
            pl.BlockSpec((tf, D_MODEL), lambda i: (0, 0), **once),
        ],
        out_specs=pl.BlockSpec((tm, D_MODEL), lambda i: (i, 0)),
        out_shape=jax.ShapeDtypeStruct((m, D_MODEL), F32),
        scratch_shapes=[
            pltpu.VMEM((tf // LANES, tm + HALO, LANES), F32),
            pltpu.VMEM((tf // LANES, tm + HALO, LANES), F32),
            pltpu.VMEM((tf // LANES, HALO, LANES), F32),
            pltpu.VMEM((tf // LANES, HALO, LANES), F32),
        ],
        compiler_params=pltpu.CompilerParams(
            dimension_semantics=("arbitrary",), vmem_limit_bytes=VMEM_LIMIT),
        name="ffn",
    )(h2, x1, w_up, w_up, conv_w, conv_w, conv_b, conv_b, w_down)


def _pad_lanes(v):
    return jnp.pad(v.astype(F32), (0, LANES - v.shape[0])).reshape(1, LANES)


def kernel(x, norm1_g, w_in, ssd_conv_w, ssd_conv_b, dt_bias, a_log, d_skip, ssd_norm_g, w_ssd_proj,
           q_norm_g, k_norm_g, w_attn_proj, w_out, norm2_g, w_up, ffn_conv_w, ffn_conv_b, w_down):
    batch, seq, _ = x.shape
    assert norm1_g.shape[0] == 1, "single-layer block"
    x2 = x.reshape(batch * seq, D_MODEL)
    g1 = norm1_g[0].reshape(1, D_MODEL)

    w = w_in[0]
    o_xbc = D_INNER
    o_dt = o_xbc + SSD_CONV_DIM
    o_q = o_dt + SSD_HEADS
    o_g = o_q + 3 * ATTN_WIDTH
    w_zx = w[:, :o_dt].astype(BF16)
    w_gate = w[:, o_g:].astype(BF16)
    w_dt = jnp.pad(w[:, o_dt:o_q], ((0, 0), (0, LANES - SSD_HEADS))).astype(BF16)
    w_qkv = w[:, o_q:o_g].astype(BF16)

    proj, dt_raw, h = _in_proj_main(x2, g1, w_zx, w_gate, w_dt, ssd_conv_w[0], ssd_conv_b[0].reshape(1, -1),
                                    tm=TM_IN_PROJ, tn=TN_IN_PROJ, seq=seq)

    heads_per_tile = MXU_WIDTH // ATTN_HEAD_DIM
    qk_gains = jnp.stack([jnp.tile(q_norm_g[0] * (ATTN_HEAD_DIM ** -0.5), heads_per_tile),
                          jnp.tile(k_norm_g[0], heads_per_tile)]).astype(F32)
    lane_head = jnp.arange(MXU_WIDTH) // ATTN_HEAD_DIM
    seg = (lane_head[:, None] == lane_head[None, :]).astype(BF16)
    qkv = _in_proj_qkv(h, w_qkv, qk_gains, seg, tm=TM_IN_PROJ)

    tri = jnp.tril(jnp.ones((SSD_CHUNK, SSD_CHUNK), BF16))
    head_of_col = jnp.arange(D_INNER) // SSD_HEAD_DIM
    expand = (jnp.arange(LANES)[:, None] == head_of_col[None, :]).astype(BF16)
    y_ssd = _ssd(proj, dt_raw, _pad_lanes(dt_bias[0]), _pad_lanes(a_log[0]),
                 jnp.repeat(d_skip[0], SSD_HEAD_DIM).reshape(1, D_INNER),
                 tri, jnp.concatenate([expand, expand], axis=0),
                 batch, seq, chunks_per_step=SSD_CHUNKS_PER_STEP)

    outs, lses = zip(*[_attn_group(qkv, gi, batch, seq) for gi in range(ATTN_GROUPS)])

    attn_head_of_col = jnp.arange(ATTN_GROUP_WIDTH) // ATTN_HEAD_DIM
    head_expand = (jnp.arange(2 * LANES)[:, None] % LANES == attn_head_of_col[None, :]).astype(BF16)
    x1, h2 = _merge(x2, y_ssd, outs, lses, proj, ssd_norm_g[0].reshape(1, D_INNER),
                    w_ssd_proj[0].astype(BF16), w_attn_proj[0].astype(BF16), w_out[0].astype(BF16),
                    head_expand, norm2_g[0].reshape(1, D_MODEL), tm=TM_MERGE)

    out = _ffn(h2, x1, w_up[0].astype(BF16), ffn_conv_w[0], ffn_conv_b[0].reshape(1, -1),
               w_down[0].astype(BF16), tm=TM_FFN, seq=seq)
    return out.reshape(batch, seq, D_MODEL)
```

```python
import functools

import jax
import jax.numpy as jnp
from jax import lax
from jax.experimental import pallas as pl
from jax.experimental.pallas import tpu as pltpu

F32 = jnp.float32
BF16 = jnp.bfloat16

D_MODEL = 1024
D_INNER = 2048
SSD_HEADS = 32
SSD_HEAD_DIM = 64
SSD_GROUPS = 8
SSD_HEADS_PER_GROUP = SSD_HEADS // SSD_GROUPS
D_STATE = 128
SSD_CONV = 4
SSD_CHUNK = 128
SSD_CONV_DIM = D_INNER + 2 * SSD_GROUPS * D_STATE
SSD_GROUP_WIDTH = D_INNER // SSD_GROUPS
ATTN_WINDOWS = (128, 512, 2048)
ATTN_DILATIONS = (1, 4, 16)
ATTN_GROUPS = 3
ATTN_HEADS = 8
ATTN_HEAD_DIM = 64
ATTN_BLOCK = 128
ATTN_GROUP_WIDTH = ATTN_HEADS * ATTN_HEAD_DIM
ATTN_WIDTH = ATTN_GROUPS * ATTN_GROUP_WIDTH
D_FF = 2816
FFN_CONV = 3
FFN_CHUNK = D_FF // 2
EPS = 1e-6
LANES = 128
MXU_WIDTH = 256
HALO = 8
ATTN_SLAB = ATTN_BLOCK * max(ATTN_DILATIONS)
ATTN_SLAB_BLOCKS = ATTN_SLAB // ATTN_BLOCK
ATTN_LANE_CHUNKS = ATTN_GROUP_WIDTH // LANES
ATTN_UNROLL = 16
DEINTERLEAVE_FACTOR = 4

COL_XBC = 0
COL_Z = COL_XBC + SSD_CONV_DIM
COL_GSSD = COL_Z + D_INNER
COL_GATTN = COL_GSSD + D_MODEL
PROJ_WIDTH = COL_GATTN + D_MODEL

V7X_VMEM_BYTES = 64 * 1024 * 1024
VMEM_LIMIT = V7X_VMEM_BYTES * 7 // 8

TM_IN_PROJ = 1024
TN_IN_PROJ = 2048
TM_MERGE = 512
TM_FFN = 512
SSD_CHUNKS_PER_STEP = 4


def _sigmoid(v):
    return 0.5 * jnp.tanh(0.5 * v) + 0.5


def _silu_of_half(hv):
    return hv + hv * jnp.tanh(hv)


def _split_bf16(v, terms):
    parts = []
    for _ in range(terms):
        p = v.astype(BF16)
        parts.append(p)
        v = v - p.astype(F32)
    return parts


def _dot_split(parts, rhs, lhs_side=True):
    acc = None
    for p in parts:
        t = (jnp.dot(p, rhs, preferred_element_type=F32) if lhs_side
             else jnp.dot(rhs, p, preferred_element_type=F32))
        acc = t if acc is None else acc + t
    return acc


def _rms_rows(x, gain):
    ms = jnp.mean(x * x, axis=-1, keepdims=True)
    return x * lax.rsqrt(ms + EPS) * gain


def _in_proj_main_kernel(x_ref, g_ref, w_ref, wgate_ref, wdt_ref, cw_ref, cb_ref, proj_ref, dt_ref, h_ref,
                         xp_scr, halo_scr, *, tiles_per_seq):
    i = pl.program_id(0)
    j = pl.program_id(1)
    tm, tn = proj_ref.shape
    conv_tiles = SSD_CONV_DIM // tn
    z_tiles = D_INNER // tn

    @pl.when(j == 0)
    def _():
        h = _rms_rows(x_ref[...], g_ref[...]).astype(BF16)
        h_ref[...] = h
        dt_ref[...] = jnp.dot(h, wdt_ref[...], preferred_element_type=F32)

    @pl.when(jnp.logical_and(i == 0, j == 0))
    def _():
        halo_scr[...] = jnp.zeros(halo_scr.shape, F32)

    @pl.when(j < conv_tiles)
    def _():
        r = jnp.dot(h_ref[...], w_ref[...], preferred_element_type=F32)
        first = (i % tiles_per_seq) == 0
        for c in range(tn // LANES):
            cs = slice(c * LANES, (c + 1) * LANES)
            xp_scr[c, 0:HALO, :] = jnp.where(first, 0.0, halo_scr[j, c])
            xp_scr[c, HALO:HALO + tm, :] = r[:, cs]
            acc = 0.5 * cb_ref[:, cs]
            for t in range(SSD_CONV):
                tap = xp_scr[c, pl.ds(HALO - (SSD_CONV - 1) + t, tm, stride=1), :]
                acc = acc + tap * (0.5 * cw_ref[t:t + 1, cs])
            proj_ref[:, cs] = _silu_of_half(acc).astype(proj_ref.dtype)
            halo_scr[j, c] = xp_scr[c, tm:tm + HALO, :]

    @pl.when(jnp.logical_and(j >= conv_tiles, j < conv_tiles + z_tiles))
    def _():
        r = jnp.dot(h_ref[...], w_ref[...], preferred_element_type=F32)
        proj_ref[...] = _silu_of_half(0.5 * r).astype(proj_ref.dtype)

    @pl.when(j >= conv_tiles + z_tiles)
    def _():
        proj_ref[...] = jnp.dot(h_ref[...], wgate_ref[...], preferred_element_type=F32).astype(proj_ref.dtype)


def _in_proj_main(x2, g1, w_zx, w_gate, w_dt, conv_w, conv_b, tm, tn, seq):
    m = x2.shape[0]
    assert SSD_CONV_DIM % tn == 0 and D_INNER % tn == 0 and seq % tm == 0 and w_gate.shape[1] == tn
    conv_tiles = SSD_CONV_DIM // tn
    z_tiles = D_INNER // tn
    conv_col = lambda i, j: (0, jnp.minimum(j, conv_tiles - 1))
    zx_col = lambda i, j: (0, jnp.where(j < conv_tiles, j + z_tiles, jnp.minimum(j - conv_tiles, z_tiles - 1)))
    return pl.pallas_call(
        functools.partial(_in_proj_main_kernel, tiles_per_seq=seq // tm),
        grid=(m // tm, PROJ_WIDTH // tn),
        in_specs=[
            pl.BlockSpec((tm, D_MODEL), lambda i, j: (i, 0)),
            pl.BlockSpec((1, D_MODEL), lambda i, j: (0, 0)),
            pl.BlockSpec((D_MODEL, tn), zx_col),
            pl.BlockSpec((D_MODEL, tn), lambda i, j: (0, 0), pipeline_mode=pl.Buffered(1)),
            pl.BlockSpec((D_MODEL, LANES), lambda i, j: (0, 0)),
            pl.BlockSpec((SSD_CONV, tn), conv_col),
            pl.BlockSpec((1, tn), conv_col),
        ],
        out_specs=[
            pl.BlockSpec((tm, tn), lambda i, j: (i, j)),
            pl.BlockSpec((tm, LANES), lambda i, j: (i, 0)),
            pl.BlockSpec((tm, D_MODEL), lambda i, j: (i, 0)),
        ],
        out_shape=[
            jax.ShapeDtypeStruct((m, PROJ_WIDTH), BF16),
            jax.ShapeDtypeStruct((m, LANES), F32),
            jax.ShapeDtypeStruct((m, D_MODEL), BF16),
        ],
        scratch_shapes=[
            pltpu.VMEM((tn // LANES, HALO + tm, LANES), F32),
            pltpu.VMEM((conv_tiles, tn // LANES, HALO, LANES), F32),
        ],
        compiler_params=pltpu.CompilerParams(
            dimension_semantics=("arbitrary", "arbitrary"), vmem_limit_bytes=VMEM_LIMIT),
        name="in_proj_main",
    )(x2, g1, w_zx, w_gate, w_dt, conv_w, conv_b)


def _in_proj_qkv_kernel(h_ref, w_ref, gain_ref, seg_ref, out_ref, t_scr):
    j = pl.program_id(1)
    seg_w = seg_ref.shape[0]

    @pl.when(j < 2)
    def _():
        t_scr[...] = jnp.dot(h_ref[...], w_ref[...], preferred_element_type=F32)
        gain = gain_ref[pl.ds(j, 1), :]
        tm = t_scr.shape[0]
        n_seg = ATTN_WIDTH // seg_w
        sq = [(t_scr[:, c * seg_w:(c + 1) * seg_w] ** 2).astype(BF16) for c in range(n_seg)]
        ss = jnp.dot(jnp.concatenate(sq, axis=0), seg_ref[...], preferred_element_type=F32)
        for c in range(n_seg):
            cs = slice(c * seg_w, (c + 1) * seg_w)
            inv_rms = lax.rsqrt(ss[c * tm:(c + 1) * tm] * (1.0 / ATTN_HEAD_DIM) + EPS)
            out_ref[:, cs] = (t_scr[:, cs] * inv_rms * gain).astype(out_ref.dtype)

    @pl.when(j == 2)
    def _():
        out_ref[...] = jnp.dot(h_ref[...], w_ref[...], preferred_element_type=F32).astype(out_ref.dtype)


def _in_proj_qkv(h, w_qkv, gains, seg, tm):
    m = h.shape[0]
    return pl.pallas_call(
        _in_proj_qkv_kernel,
        grid=(m // tm, 3),
        in_specs=[
            pl.BlockSpec((tm, D_MODEL), lambda i, j: (i, 0)),
            pl.BlockSpec((D_MODEL, ATTN_WIDTH), lambda i, j: (0, j)),
            pl.BlockSpec(gains.shape, lambda i, j: (0, 0)),
            pl.BlockSpec(seg.shape, lambda i, j: (0, 0)),
        ],
        out_specs=pl.BlockSpec((tm, ATTN_WIDTH), lambda i, j: (i, j)),
        out_shape=jax.ShapeDtypeStruct((m, 3 * ATTN_WIDTH), BF16),
        scratch_shapes=[pltpu.VMEM((tm, ATTN_WIDTH), F32)],
        compiler_params=pltpu.CompilerParams(
            dimension_semantics=("arbitrary", "arbitrary"), vmem_limit_bytes=VMEM_LIMIT),
        name="in_proj_qkv",
    )(h, w_qkv, gains, seg)


def _ssd_decay(dt_raw, dtb, a_log, tri):
    q = SSD_CHUNK
    dt = jax.nn.softplus(dt_raw + dtb)
    a = dt * (-jnp.exp(a_log))
    a_cs = _dot_split(_split_bf16(a, 3), tri, lhs_side=False)
    a_cs_t_adj = a_cs.T - jnp.log(dt.T)
    e_in = jnp.exp(a_cs)
    w_out = jnp.exp(a_cs[q - 1:q, :] - a_cs) * dt
    spread = jnp.concatenate([jnp.concatenate(_split_bf16(e_in, 2), axis=1),
                              jnp.concatenate(_split_bf16(w_out, 2), axis=1)], axis=0)
    return a_cs, a_cs_t_adj, spread


def _ssd_chunk(act, a_cs, a_cs_t_adj, e_exp, w_exp, dskip, y_ref, rows, state_scr):
    q = SSD_CHUNK
    li = lax.broadcasted_iota(jnp.int32, (q, q), 0)
    si = lax.broadcasted_iota(jnp.int32, (q, q), 1)
    causal = li >= si
    head_of_lane = lax.broadcasted_iota(jnp.int32, (q, SSD_GROUP_WIDTH), 1) // SSD_HEAD_DIM

    for g in range(SSD_GROUPS):
        gs = slice(g * SSD_GROUP_WIDTH, (g + 1) * SSD_GROUP_WIDTH)
        xs_b = act[:, gs]
        xs = xs_b.astype(F32)
        b_g = act[:, D_INNER + g * D_STATE:D_INNER + (g + 1) * D_STATE]
        c_g = act[:, D_INNER + (SSD_GROUPS + g) * D_STATE:D_INNER + (SSD_GROUPS + g + 1) * D_STATE]
        cb = lax.dot_general(c_g, b_g, (((1,), (1,)), ((), ())), preferred_element_type=F32)
        st = state_scr[g]
        y_off = jnp.dot(c_g, st.astype(BF16), preferred_element_type=F32) * e_exp[:, gs]
        mms, blocks = [], []
        for k in range(SSD_HEADS_PER_GROUP):
            h = g * SSD_HEADS_PER_GROUP + k
            seg = a_cs[:, h:h + 1] - a_cs_t_adj[h:h + 1, :]
            mms.append((cb * jnp.exp(jnp.where(causal, seg, -jnp.inf))).astype(BF16))
            blocks.append(jnp.where(head_of_lane == k, xs_b, jnp.zeros_like(xs_b)))
        y_diag = jnp.dot(jnp.concatenate(mms, axis=1), jnp.concatenate(blocks, axis=0),
                         preferred_element_type=F32)
        y = y_diag + y_off + dskip[:, gs] * xs
        y_ref[rows, gs] = y.astype(y_ref.dtype)
        xw = (xs * w_exp[:, gs]).astype(BF16)
        upd = lax.dot_general(b_g, xw, (((0,), (0,)), ((), ())), preferred_element_type=F32)
        state_scr[g] = st * e_exp[q - 1:q, gs] + upd


def _ssd_kernel(act_ref, dt_ref, dtb_ref, alog_ref, dskip_ref, tri_ref, expand2_ref, y_ref, state_scr):
    @pl.when(pl.program_id(1) == 0)
    def _():
        state_scr[...] = jnp.zeros(state_scr.shape, F32)

    q = SSD_CHUNK
    chunks = [slice(ci * q, (ci + 1) * q) for ci in range(act_ref.shape[0] // q)]
    decay = [_ssd_decay(dt_ref[rows, :], dtb_ref[...], alog_ref[...], tri_ref[...]) for rows in chunks]
    spread = jnp.dot(jnp.concatenate([d[2] for d in decay], axis=0), expand2_ref[...],
                     preferred_element_type=F32)
    for ci, rows in enumerate(chunks):
        a_cs, a_cs_t_adj, _ = decay[ci]
        e_exp = spread[2 * ci * q:(2 * ci + 1) * q]
        w_exp = spread[(2 * ci + 1) * q:(2 * ci + 2) * q]
        _ssd_chunk(act_ref[rows, :], a_cs, a_cs_t_adj, e_exp, w_exp, dskip_ref[...], y_ref, rows, state_scr)


def _ssd(proj, dt_raw, dt_bias, a_log, d_skip_exp, tri, expand2, batch, seq, chunks_per_step):
    rows = SSD_CHUNK * chunks_per_step
    steps = seq // rows
    row = lambda b, c: b * steps + c
    const = lambda b, c: (0, 0)
    return pl.pallas_call(
        _ssd_kernel,
        grid=(batch, steps),
        in_specs=[
            pl.BlockSpec((rows, SSD_CONV_DIM), lambda b, c: (row(b, c), COL_XBC // SSD_CONV_DIM)),
            pl.BlockSpec((rows, LANES), lambda b, c: (row(b, c), 0)),
            pl.BlockSpec((1, LANES), const),
            pl.BlockSpec((1, LANES), const),
            pl.BlockSpec((1, D_INNER), const),
            pl.BlockSpec((SSD_CHUNK, SSD_CHUNK), const),
            pl.BlockSpec((2 * LANES, D_INNER), const),
        ],
        out_specs=pl.BlockSpec((rows, D_INNER), lambda b, c: (row(b, c), 0)),
        out_shape=jax.ShapeDtypeStruct((batch * seq, D_INNER), BF16),
        scratch_shapes=[pltpu.VMEM((SSD_GROUPS, D_STATE, SSD_GROUP_WIDTH), F32)],
        compiler_params=pltpu.CompilerParams(
            dimension_semantics=("arbitrary", "arbitrary"), vmem_limit_bytes=VMEM_LIMIT),
        name="ssd",
    )(proj, dt_raw, dt_bias, a_log, d_skip_exp, tri, expand2)


def _attn_kernel(q_ref, k_ref, v_ref, o_ref, lse_ref, stage, q_de, k_de, v_de, *maybe_mid, dil, n_back):
    blk = ATTN_BLOCK
    per_res = ATTN_SLAB_BLOCKS // dil
    run = per_res + 1
    first_slab = pl.program_id(1) == 0

    def keep_previous(dst):
        @pl.when(first_slab)
        def _():
            def body(res, carry):
                dst[res * run] = jnp.zeros(dst.shape[1:], BF16)
                return carry
            lax.fori_loop(0, dil, body, 0)

        @pl.when(jnp.logical_not(first_slab))
        def _():
            def body(res, carry):
                dst[res * run] = dst[res * run + per_res]
                return carry
            lax.fori_loop(0, dil, body, 0)

    def deinterleave(src_ref, dst, skip):
        if dil == 1:
            def copy(bi, carry):
                rows = pl.ds(pl.multiple_of(bi * blk, blk), blk)
                for c in range(ATTN_LANE_CHUNKS):
                    dst[bi + skip, c] = src_ref[rows, c * LANES:(c + 1) * LANES]
                return carry

            lax.fori_loop(0, ATTN_SLAB_BLOCKS, copy, 0)
            return

        for c in range(ATTN_LANE_CHUNKS):
            stage[c] = src_ref[:, c * LANES:(c + 1) * LANES].astype(F32)

        if maybe_mid:
            mid, = maybe_mid
            f = DEINTERLEAVE_FACTOR
            for c in range(ATTN_LANE_CHUNKS):
                for r in range(f):
                    mid[c * f + r] = stage[c, pl.ds(r, ATTN_SLAB // f, stride=f), :]

            def body(bi, carry):
                for c in range(ATTN_LANE_CHUNKS):
                    dst[bi + skip * (bi + 1), c] = (
                        mid[c * f + bi % f, pl.ds(bi // f, blk, stride=f), :].astype(BF16))
                return carry
        else:
            def body(bi, carry):
                res = bi // per_res
                start = res + dil * blk * (bi % per_res)
                for c in range(ATTN_LANE_CHUNKS):
                    dst[bi + skip * (res + 1), c] = stage[c, pl.ds(start, blk, stride=dil), :].astype(BF16)
                return carry

        lax.fori_loop(0, ATTN_SLAB_BLOCKS, body, 0)

    keep_previous(k_de)
    keep_previous(v_de)
    deinterleave(k_ref, k_de, 1)
    deinterleave(v_ref, v_de, 1)
    deinterleave(q_ref, q_de, 0)

    lane = lax.broadcasted_iota(jnp.int32, (blk, LANES), 1)
    even = lane < ATTN_HEAD_DIM
    qi = lax.broadcasted_iota(jnp.int32, (blk, 2 * blk), 0)
    ki = lax.broadcasted_iota(jnp.int32, (blk, 2 * blk), 1)
    dist = blk + qi - ki
    band = (dist >= 0) & (dist <= n_back)

    def attend(bi, carry):
        res = bi // per_res
        j = bi % per_res
        key_lo = jnp.where(jnp.logical_and(first_slab, j == 0), blk, 0)
        mask = band & (ki >= key_lo)
        mask2 = jnp.concatenate([mask, mask], axis=0)
        kv = res * run + j
        start = res + dil * blk * j
        lse_tile = jnp.zeros((blk, LANES), F32)
        for hp in range(ATTN_LANE_CHUNKS):
            q_pair = q_de[bi, hp]
            k2 = jnp.concatenate([k_de[kv, hp], k_de[kv + 1, hp]], axis=0)
            v2 = jnp.concatenate([v_de[kv, hp], v_de[kv + 1, hp]], axis=0)
            v2 = jnp.concatenate([v2, jnp.ones_like(v2)], axis=1)
            zero = jnp.zeros_like(q_pair)
            qm = jnp.concatenate([jnp.where(even, q_pair, zero), jnp.where(even, zero, q_pair)], axis=0)
            s = lax.dot_general(qm, k2, (((1,), (1,)), ((), ())), preferred_element_type=F32)
            s = jnp.where(mask2, s, -jnp.inf)
            m = jnp.max(s, axis=-1, keepdims=True)
            p = jnp.exp(s - m)
            pv = jnp.dot(p.astype(BF16), v2, preferred_element_type=F32)
            den = pv[:, LANES:]
            out = pv[:, :LANES] * (1.0 / den)
            lse = m + jnp.log(den)
            halves = [out[:blk], out[blk:]]
            for e in range(2):
                lse_tile = jnp.where(lane == 2 * hp + e, lse[e * blk:(e + 1) * blk], lse_tile)
            o_pair = jnp.where(even, halves[0], halves[1])
            if dil == 1:
                rows = pl.ds(pl.multiple_of(start, blk), blk)
                o_ref[rows, hp * LANES:(hp + 1) * LANES] = o_pair.astype(o_ref.dtype)
            else:
                stage[hp, pl.ds(start, blk, stride=dil), :] = o_pair
        lse_ref[pl.ds(start, blk, stride=dil), :] = lse_tile
        return carry

    lax.fori_loop(0, ATTN_SLAB_BLOCKS, attend, 0, unroll=ATTN_UNROLL)
    if dil > 1:
        for c in range(ATTN_LANE_CHUNKS):
            o_ref[:, c * LANES:(c + 1) * LANES] = stage[c].astype(o_ref.dtype)


def _attn_group(qkv, gi, batch, seq):
    dil = ATTN_DILATIONS[gi]
    n_back = ATTN_WINDOWS[gi] // dil
    assert n_back <= ATTN_BLOCK and ATTN_SLAB_BLOCKS % dil == 0 and seq % ATTN_SLAB == 0
    slabs = seq // ATTN_SLAB

    def cur(which):
        return pl.BlockSpec((ATTN_SLAB, ATTN_GROUP_WIDTH), lambda b, s: (b * slabs + s, which * ATTN_GROUPS + gi))

    run = ATTN_SLAB_BLOCKS // dil + 1
    de_block = (ATTN_LANE_CHUNKS, ATTN_BLOCK, LANES)
    f = DEINTERLEAVE_FACTOR
    two_pass = dil == f * f and dil == ATTN_SLAB_BLOCKS
    two_pass_scratch = [pltpu.VMEM((ATTN_LANE_CHUNKS * f, ATTN_SLAB // f, LANES), F32)] if two_pass else []
    return pl.pallas_call(
        functools.partial(_attn_kernel, dil=dil, n_back=n_back),
        grid=(batch, slabs),
        in_specs=[cur(0), cur(1), cur(2)],
        out_specs=[pl.BlockSpec((ATTN_SLAB, ATTN_GROUP_WIDTH), lambda b, s: (b * slabs + s, 0)),
                   pl.BlockSpec((ATTN_SLAB, LANES), lambda b, s: (b * slabs + s, 0))],
        out_shape=[jax.ShapeDtypeStruct((batch * seq, ATTN_GROUP_WIDTH), BF16),
                   jax.ShapeDtypeStruct((batch * seq, LANES), F32)],
        scratch_shapes=[
            pltpu.VMEM((ATTN_LANE_CHUNKS, ATTN_SLAB, LANES), F32),
            pltpu.VMEM((ATTN_SLAB_BLOCKS,) + de_block, BF16),
            pltpu.VMEM((dil * run,) + de_block, BF16),
            pltpu.VMEM((dil * run,) + de_block, BF16),
        ] + two_pass_scratch,
        compiler_params=pltpu.CompilerParams(
            dimension_semantics=("arbitrary", "arbitrary"), vmem_limit_bytes=VMEM_LIMIT),
        name=f"attn_g{gi}",
    )(qkv, qkv, qkv)


def _expand_heads(w, hexp):
    return jnp.dot(jnp.concatenate(_split_bf16(w, 2), axis=1), hexp, preferred_element_type=F32)


def _merge_kernel(x_ref, yssd_ref, gz_ref, o0_ref, o1_ref, o2_ref, l0_ref, l1_ref, l2_ref, gssd_ref, gattn_ref,
                  ng_ref, wssd_ref, wattn_ref, wout_ref, hexp_ref, g2_ref, x1_ref, h2_ref):
    l0, l1, l2 = l0_ref[...], l1_ref[...], l2_ref[...]
    mx = jnp.maximum(jnp.maximum(l0, l1), l2)
    e0, e1, e2 = jnp.exp(l0 - mx), jnp.exp(l1 - mx), jnp.exp(l2 - mx)
    inv = 1.0 / (e0 + e1 + e2)
    rows = o0_ref.shape[0]
    wexp = _expand_heads(jnp.concatenate([e0 * inv, e1 * inv, e2 * inv], axis=0), hexp_ref[...])
    y_attn = jnp.zeros(o0_ref.shape, F32)
    for gi, o_ref in enumerate((o0_ref, o1_ref, o2_ref)):
        y_attn = y_attn + wexp[gi * rows:(gi + 1) * rows] * o_ref[...].astype(F32)
    p_ssd = None
    for g in range(SSD_GROUPS):
        gs = slice(g * SSD_GROUP_WIDTH, (g + 1) * SSD_GROUP_WIDTH)
        y = yssd_ref[:, gs].astype(F32) * gz_ref[:, gs].astype(F32)
        yn = _rms_rows(y, ng_ref[:, gs]).astype(BF16)
        part = jnp.dot(yn, wssd_ref[gs, :], preferred_element_type=F32)
        p_ssd = part if p_ssd is None else p_ssd + part
    p_attn = jnp.dot(y_attn.astype(BF16), wattn_ref[...], preferred_element_type=F32)
    merged = (_sigmoid(gssd_ref[...].astype(F32)) * p_ssd
              + _sigmoid(gattn_ref[...].astype(F32)) * p_attn)
    x1 = x_ref[...] + jnp.dot(merged.astype(BF16), wout_ref[...], preferred_element_type=F32)
    x1_ref[...] = x1
    h2_ref[...] = _rms_rows(x1, g2_ref[...]).astype(h2_ref.dtype)


def _merge(x2, y_ssd, outs, lses, proj, ssd_norm_g, w_ssd, w_attn, w_out, head_expand, g2, tm):
    m = x2.shape[0]
    rows = lambda width, cb=0: pl.BlockSpec((tm, width), lambda i: (i, cb))
    const = lambda shape: pl.BlockSpec(shape, lambda i: (0, 0))
    return pl.pallas_call(
        _merge_kernel,
        grid=(m // tm,),
        in_specs=[rows(D_MODEL), rows(D_INNER), rows(D_INNER, COL_Z // D_INNER),
                  rows(ATTN_GROUP_WIDTH), rows(ATTN_GROUP_WIDTH), rows(ATTN_GROUP_WIDTH),
                  rows(LANES), rows(LANES), rows(LANES),
                  rows(D_MODEL, COL_GSSD // D_MODEL), rows(D_MODEL, COL_GATTN // D_MODEL),
                  const((1, D_INNER)), const((D_INNER, D_MODEL)), const((ATTN_GROUP_WIDTH, D_MODEL)),
                  const((D_MODEL, D_MODEL)),
                  const((2 * LANES, ATTN_GROUP_WIDTH)), const((1, D_MODEL))],
        out_specs=[rows(D_MODEL), rows(D_MODEL)],
        out_shape=[jax.ShapeDtypeStruct((m, D_MODEL), F32), jax.ShapeDtypeStruct((m, D_MODEL), BF16)],
        compiler_params=pltpu.CompilerParams(
            dimension_semantics=("arbitrary",), vmem_limit_bytes=VMEM_LIMIT),
        name="merge",
    )(x2, y_ssd, proj, outs[0], outs[1], outs[2], lses[0], lses[1], lses[2], proj, proj,
      ssd_norm_g, w_ssd, w_attn, w_out, head_expand, g2)


def _ffn_kernel(h_ref, x1_ref, wg_ref, wv_ref, cwg_ref, cwv_ref, cbg_ref, cbv_ref, wd_ref, out_ref,
                ug_scr, uv_scr, halo_g, halo_v, *, tiles_per_seq):
    i = pl.program_id(0)
    tm = h_ref.shape[0]
    tf = wg_ref.shape[1]
    first = (i % tiles_per_seq) == 0

    @pl.when(i == 0)
    def _():
        halo_g[...] = jnp.zeros(halo_g.shape, F32)
        halo_v[...] = jnp.zeros(halo_v.shape, F32)

    def conv(u, u_scr, halo, w_ref, b_ref, c, scale):
        cs = slice(c * LANES, (c + 1) * LANES)
        u_scr[c, 0:HALO, :] = jnp.where(first, 0.0, halo[c])
        u_scr[c, HALO:HALO + tm, :] = u
        acc = scale * b_ref[:, cs]
        for t in range(FFN_CONV):
            tap = u_scr[c, pl.ds(HALO - (FFN_CONV - 1) + t, tm, stride=1), :]
            acc = acc + tap * (scale * w_ref[t:t + 1, cs])
        halo[c] = u_scr[c, tm:tm + HALO, :]
        return acc

    per = FFN_CHUNK // LANES
    total = None
    for fc in range(tf // FFN_CHUNK):
        cols = slice(fc * FFN_CHUNK, (fc + 1) * FFN_CHUNK)
        ug = jnp.dot(h_ref[...], wg_ref[:, cols], preferred_element_type=F32)
        uv = jnp.dot(h_ref[...], wv_ref[:, cols], preferred_element_type=F32)
        acts = []
        for ci in range(per):
            c = fc * per + ci
            ls = slice(ci * LANES, (ci + 1) * LANES)
            gate = conv(ug[:, ls], ug_scr, halo_g, cwg_ref, cbg_ref, c, 0.5)
            val = conv(uv[:, ls], uv_scr, halo_v, cwv_ref, cbv_ref, c, 1.0)
            acts.append((_silu_of_half(gate) * val).astype(BF16))
        part = jnp.dot(jnp.concatenate(acts, axis=1), wd_ref[cols, :], preferred_element_type=F32)
        total = part if total is None else total + part
    out_ref[...] = x1_ref[...] + total


def _ffn(h2, x1, w_up, conv_w, conv_b, w_down, tm, seq):
    m = h2.shape[0]
    tf = D_FF
    once = dict(pipeline_mode=pl.Buffered(1))
    gate_half = lambda i: (0, 0)
    value_half = lambda i: (0, 1)
    return pl.pallas_call(
        functools.partial(_ffn_kernel, tiles_per_seq=seq // tm),
        grid=(m // tm,),
        in_specs=[
            pl.BlockSpec((tm, D_MODEL), lambda i: (i, 0)),
            pl.BlockSpec((tm, D_MODEL), lambda i: (i, 0)),
            pl.BlockSpec((D_MODEL, tf), gate_half, **once),
            pl.BlockSpec((D_MODEL, tf), value_half, **once),
            pl.BlockSpec((FFN_CONV, tf), gate_half, **once),
            pl.BlockSpec((FFN_CONV, tf), value_half, **once),
            pl.BlockSpec((1, tf), gate_half, **once),
            pl.BlockSpec((1, tf), value_half, **once),
            pl.BlockSpec((tf, D_MODEL), lambda i: (0, 0), **once),
        ],
        out_specs=pl.BlockSpec((tm, D_MODEL), lambda i: (i, 0)),
        out_shape=jax.ShapeDtypeStruct((m, D_MODEL), F32),
        scratch_shapes=[
            pltpu.VMEM((tf // LANES, tm + HALO, LANES), F32),
            pltpu.VMEM((tf // LANES, tm + HALO, LANES), F32),
            pltpu.VMEM((tf // LANES, HALO, LANES), F32),
            pltpu.VMEM((tf // LANES, HALO, LANES), F32),
        ],
        compiler_params=pltpu.CompilerParams(
            dimension_semantics=("arbitrary",), vmem_limit_bytes=VMEM_LIMIT),
        name="ffn",
    )(h2, x1, w_up, w_up, conv_w, conv_w, conv_b, conv_b, w_down)


def _pad_lanes(v):
    return jnp.pad(v.astype(F32), (0, LANES - v.shape[0])).reshape(1, LANES)


def kernel(x, norm1_g, w_in, ssd_conv_w, ssd_conv_b, dt_bias, a_log, d_skip, ssd_norm_g, w_ssd_proj,
           q_norm_g, k_norm_g, w_attn_proj, w_out, norm2_g, w_up, ffn_conv_w, ffn_conv_b, w_down):
    batch, seq, _ = x.shape
    assert norm1_g.shape[0] == 1, "single-layer block"
    x2 = x.reshape(batch * seq, D_MODEL)
    g1 = norm1_g[0].reshape(1, D_MODEL)

    w = w_in[0]
    o_xbc = D_INNER
    o_dt = o_xbc + SSD_CONV_DIM
    o_q = o_dt + SSD_HEADS
    o_g = o_q + 3 * ATTN_WIDTH
    w_zx = w[:, :o_dt].astype(BF16)
    w_gate = w[:, o_g:].astype(BF16)
    w_dt = jnp.pad(w[:, o_dt:o_q], ((0, 0), (0, LANES - SSD_HEADS))).astype(BF16)
    w_qkv = w[:, o_q:o_g].astype(BF16)

    proj, dt_raw, h = _in_proj_main(x2, g1, w_zx, w_gate, w_dt, ssd_conv_w[0], ssd_conv_b[0].reshape(1, -1),
                                    tm=TM_IN_PROJ, tn=TN_IN_PROJ, seq=seq)

    heads_per_tile = MXU_WIDTH // ATTN_HEAD_DIM
    qk_gains = jnp.stack([jnp.tile(q_norm_g[0] * (ATTN_HEAD_DIM ** -0.5), heads_per_tile),
                          jnp.tile(k_norm_g[0], heads_per_tile)]).astype(F32)
    lane_head = jnp.arange(MXU_WIDTH) // ATTN_HEAD_DIM
    seg = (lane_head[:, None] == lane_head[None, :]).astype(BF16)
    qkv = _in_proj_qkv(h, w_qkv, qk_gains, seg, tm=TM_IN_PROJ)

    tri = jnp.tril(jnp.ones((SSD_CHUNK, SSD_CHUNK), BF16))
    head_of_col = jnp.arange(D_INNER) // SSD_HEAD_DIM
    expand = (jnp.arange(LANES)[:, None] == head_of_col[None, :]).astype(BF16)
    y_ssd = _ssd(proj, dt_raw, _pad_lanes(dt_bias[0]), _pad_lanes(a_log[0]),
                 jnp.repeat(d_skip[0], SSD_HEAD_DIM).reshape(1, D_INNER),
                 tri, jnp.concatenate([expand, expand], axis=0),
                 batch, seq, chunks_per_step=SSD_CHUNKS_PER_STEP)

    outs, lses = zip(*[_attn_group(qkv, gi, batch, seq) for gi in range(ATTN_GROUPS)])

    attn_head_of_col = jnp.arange(ATTN_GROUP_WIDTH) // ATTN_HEAD_DIM
    head_expand = (jnp.arange(2 * LANES)[:, None] % LANES == attn_head_of_col[None, :]).astype(BF16)
    x1, h2 = _merge(x2, y_ssd, outs, lses, proj, ssd_norm_g[0].reshape(1, D_INNER),
                    w_ssd_proj[0].astype(BF16), w_attn_proj[0].astype(BF16), w_out[0].astype(BF16),
                    head_expand, norm2_g[0].reshape(1, D_MODEL), tm=TM_MERGE)

    out = _ffn(h2, x1, w_up[0].astype(BF16), ffn_conv_w[0], ffn_conv_b[0].reshape(1, -1),
               w_down[0].astype(BF16), tm=TM_FFN, seq=seq)
    return out.reshape(batch, seq, D_MODEL)
```

```python
import functools

import jax
import jax.numpy as jnp
from jax import lax
from jax.experimental import pallas as pl
from jax.experimental.pallas import tpu as pltpu

F32 = jnp.float32
BF16 = jnp.bfloat16

D_MODEL = 1024
D_INNER = 2048
SSD_HEADS = 32
SSD_HEAD_DIM = 64
SSD_GROUPS = 8
SSD_HEADS_PER_GROUP = SSD_HEADS // SSD_GROUPS
D_STATE = 128
SSD_CONV = 4
SSD_CHUNK = 128
SSD_CONV_DIM = D_INNER + 2 * SSD_GROUPS * D_STATE
SSD_GROUP_WIDTH = D_INNER // SSD_GROUPS
ATTN_WINDOWS = (128, 512, 2048)
ATTN_DILATIONS = (1, 4, 16)
ATTN_GROUPS = 3
ATTN_HEADS = 8
ATTN_HEAD_DIM = 64
ATTN_BLOCK = 128
ATTN_GROUP_WIDTH = ATTN_HEADS * ATTN_HEAD_DIM
ATTN_WIDTH = ATTN_GROUPS * ATTN_GROUP_WIDTH
D_FF = 2816
FFN_CONV = 3
FFN_CHUNK = D_FF // 2
EPS = 1e-6
LANES = 128
MXU_WIDTH = 256
HALO = 8
ATTN_SLAB = ATTN_BLOCK * max(ATTN_DILATIONS)
ATTN_SLAB_BLOCKS = ATTN_SLAB // ATTN_BLOCK
ATTN_LANE_CHUNKS = ATTN_GROUP_WIDTH // LANES
ATTN_UNROLL = 16
DEINTERLEAVE_FACTOR = 4

COL_XBC = 0
COL_Z = COL_XBC + SSD_CONV_DIM
COL_GSSD = COL_Z + D_INNER
COL_GATTN = COL_GSSD + D_MODEL
PROJ_WIDTH = COL_GATTN + D_MODEL

V7X_VMEM_BYTES = 64 * 1024 * 1024
VMEM_LIMIT = V7X_VMEM_BYTES * 7 // 8

TM_IN_PROJ = 1024
TN_IN_PROJ = 2048
TM_MERGE = 512
TM_FFN = 512
SSD_CHUNKS_PER_STEP = 4


def _sigmoid(v):
    return 0.5 * jnp.tanh(0.5 * v) + 0.5


def _silu_of_half(hv):
    return hv + hv * jnp.tanh(hv)


def _split_bf16(v, terms):
    parts = []
    for _ in range(terms):
        p = v.astype(BF16)
        parts.append(p)
        v = v - p.astype(F32)
    return parts


def _dot_split(parts, rhs, lhs_side=True):
    acc = None
    for p in parts:
        t = (jnp.dot(p, rhs, preferred_element_type=F32) if lhs_side
             else jnp.dot(rhs, p, preferred_element_type=F32))
        acc = t if acc is None else acc + t
    return acc


def _rms_rows(x, gain):
    ms = jnp.mean(x * x, axis=-1, keepdims=True)
    return x * lax.rsqrt(ms + EPS) * gain


def _in_proj_main_kernel(x_ref, g_ref, w_ref, wgate_ref, wdt_ref, cw_ref, cb_ref, proj_ref, dt_ref, h_ref,
                         xp_scr, halo_scr, *, tiles_per_seq):
    i = pl.program_id(0)
    j = pl.program_id(1)
    tm, tn = proj_ref.shape
    conv_tiles = SSD_CONV_DIM // tn
    z_tiles = D_INNER // tn

    @pl.when(j == 0)
    def _():
        h = _rms_rows(x_ref[...], g_ref[...]).astype(BF16)
        h_ref[...] = h
        dt_ref[...] = jnp.dot(h, wdt_ref[...], preferred_element_type=F32)

    @pl.when(jnp.logical_and(i == 0, j == 0))
    def _():
        halo_scr[...] = jnp.zeros(halo_scr.shape, F32)

    @pl.when(j < conv_tiles)
    def _():
        r = jnp.dot(h_ref[...], w_ref[...], preferred_element_type=F32)
        first = (i % tiles_per_seq) == 0
        for c in range(tn // LANES):
            cs = slice(c * LANES, (c + 1) * LANES)
            xp_scr[c, 0:HALO, :] = jnp.where(first, 0.0, halo_scr[j, c])
            xp_scr[c, HALO:HALO + tm, :] = r[:, cs]
            acc = 0.5 * cb_ref[:, cs]
            for t in range(SSD_CONV):
                tap = xp_scr[c, pl.ds(HALO - (SSD_CONV - 1) + t, tm, stride=1), :]
                acc = acc + tap * (0.5 * cw_ref[t:t + 1, cs])
            proj_ref[:, cs] = _silu_of_half(acc).astype(proj_ref.dtype)
            halo_scr[j, c] = xp_scr[c, tm:tm + HALO, :]

    @pl.when(jnp.logical_and(j >= conv_tiles, j < conv_tiles + z_tiles))
    def _():
        r = jnp.dot(h_ref[...], w_ref[...], preferred_element_type=F32)
        proj_ref[...] = _silu_of_half(0.5 * r).astype(proj_ref.dtype)

    @pl.when(j >= conv_tiles + z_tiles)
    def _():
        proj_ref[...] = jnp.dot(h_ref[...], wgate_ref[...], preferred_element_type=F32).astype(proj_ref.dtype)


def _in_proj_main(x2, g1, w_zx, w_gate, w_dt, conv_w, conv_b, tm, tn, seq):
    m = x2.shape[0]
    assert SSD_CONV_DIM % tn == 0 and D_INNER % tn == 0 and seq % tm == 0 and w_gate.shape[1] == tn
    conv_tiles = SSD_CONV_DIM // tn
    z_tiles = D_INNER // tn
    conv_col = lambda i, j: (0, jnp.minimum(j, conv_tiles - 1))
    zx_col = lambda i, j: (0, jnp.where(j < conv_tiles, j + z_tiles, jnp.minimum(j - conv_tiles, z_tiles - 1)))
    return pl.pallas_call(
        functools.partial(_in_proj_main_kernel, tiles_per_seq=seq // tm),
        grid=(m // tm, PROJ_WIDTH // tn),
        in_specs=[
            pl.BlockSpec((tm, D_MODEL), lambda i, j: (i, 0)),
            pl.BlockSpec((1, D_MODEL), lambda i, j: (0, 0)),
            pl.BlockSpec((D_MODEL, tn), zx_col),
            pl.BlockSpec((D_MODEL, tn), lambda i, j: (0, 0), pipeline_mode=pl.Buffered(1)),
            pl.BlockSpec((D_MODEL, LANES), lambda i, j: (0, 0)),
            pl.BlockSpec((SSD_CONV, tn), conv_col),
            pl.BlockSpec((1, tn), conv_col),
        ],
        out_specs=[
            pl.BlockSpec((tm, tn), lambda i, j: (i, j)),
            pl.BlockSpec((tm, LANES), lambda i, j: (i, 0)),
            pl.BlockSpec((tm, D_MODEL), lambda i, j: (i, 0)),
        ],
        out_shape=[
            jax.ShapeDtypeStruct((m, PROJ_WIDTH), BF16),
            jax.ShapeDtypeStruct((m, LANES), F32),
            jax.ShapeDtypeStruct((m, D_MODEL), BF16),
        ],
        scratch_shapes=[
            pltpu.VMEM((tn // LANES, HALO + tm, LANES), F32),
            pltpu.VMEM((conv_tiles, tn // LANES, HALO, LANES), F32),
        ],
        compiler_params=pltpu.CompilerParams(
            dimension_semantics=("arbitrary", "arbitrary"), vmem_limit_bytes=VMEM_LIMIT),
        name="in_proj_main",
    )(x2, g1, w_zx, w_gate, w_dt, conv_w, conv_b)


def _in_proj_qkv_kernel(h_ref, w_ref, gain_ref, seg_ref, out_ref, t_scr):
    j = pl.program_id(1)
    seg_w = seg_ref.shape[0]

    @pl.when(j < 2)
    def _():
        t_scr[...] = jnp.dot(h_ref[...], w_ref[...], preferred_element_type=F32)
        gain = gain_ref[pl.ds(j, 1), :]
        tm = t_scr.shape[0]
        n_seg = ATTN_WIDTH // seg_w
        sq = [(t_scr[:, c * seg_w:(c + 1) * seg_w] ** 2).astype(BF16) for c in range(n_seg)]
        ss = jnp.dot(jnp.concatenate(sq, axis=0), seg_ref[...], preferred_element_type=F32)
        for c in range(n_seg):
            cs = slice(c * seg_w, (c + 1) * seg_w)
            inv_rms = lax.rsqrt(ss[c * tm:(c + 1) * tm] * (1.0 / ATTN_HEAD_DIM) + EPS)
            out_ref[:, cs] = (t_scr[:, cs] * inv_rms * gain).astype(out_ref.dtype)

    @pl.when(j == 2)
    def _():
        out_ref[...] = jnp.dot(h_ref[...], w_ref[...], preferred_element_type=F32).astype(out_ref.dtype)


def _in_proj_qkv(h, w_qkv, gains, seg, tm):
    m = h.shape[0]
    return pl.pallas_call(
        _in_proj_qkv_kernel,
        grid=(m // tm, 3),
        in_specs=[
            pl.BlockSpec((tm, D_MODEL), lambda i, j: (i, 0)),
            pl.BlockSpec((D_MODEL, ATTN_WIDTH), lambda i, j: (0, j)),
            pl.BlockSpec(gains.shape, lambda i, j: (0, 0)),
            pl.BlockSpec(seg.shape, lambda i, j: (0, 0)),
        ],
        out_specs=pl.BlockSpec((tm, ATTN_WIDTH), lambda i, j: (i, j)),
        out_shape=jax.ShapeDtypeStruct((m, 3 * ATTN_WIDTH), BF16),
        scratch_shapes=[pltpu.VMEM((tm, ATTN_WIDTH), F32)],
        compiler_params=pltpu.CompilerParams(
            dimension_semantics=("arbitrary", "arbitrary"), vmem_limit_bytes=VMEM_LIMIT),
        name="in_proj_qkv",
    )(h, w_qkv, gains, seg)


def _ssd_decay(dt_raw, dtb, a_log, tri):
    q = SSD_CHUNK
    dt = jax.nn.softplus(dt_raw + dtb)
    a = dt * (-jnp.exp(a_log))
    a_cs = _dot_split(_split_bf16(a, 3), tri, lhs_side=False)
    a_cs_t_adj = a_cs.T - jnp.log(dt.T)
    e_in = jnp.exp(a_cs)
    w_out = jnp.exp(a_cs[q - 1:q, :] - a_cs) * dt
    spread = jnp.concatenate([jnp.concatenate(_split_bf16(e_in, 2), axis=1),
                              jnp.concatenate(_split_bf16(w_out, 2), axis=1)], axis=0)
    return a_cs, a_cs_t_adj, spread


def _ssd_chunk(act, a_cs, a_cs_t_adj, e_exp, w_exp, dskip, y_ref, rows, state_scr):
    q = SSD_CHUNK
    li = lax.broadcasted_iota(jnp.int32, (q, q), 0)
    si = lax.broadcasted_iota(jnp.int32, (q, q), 1)
    causal = li >= si
    head_of_lane = lax.broadcasted_iota(jnp.int32, (q, SSD_GROUP_WIDTH), 1) // SSD_HEAD_DIM

    for g in range(SSD_GROUPS):
        gs = slice(g * SSD_GROUP_WIDTH, (g + 1) * SSD_GROUP_WIDTH)
        xs_b = act[:, gs]
        xs = xs_b.astype(F32)
        b_g = act[:, D_INNER + g * D_STATE:D_INNER + (g + 1) * D_STATE]
        c_g = act[:, D_INNER + (SSD_GROUPS + g) * D_STATE:D_INNER + (SSD_GROUPS + g + 1) * D_STATE]
        cb = lax.dot_general(c_g, b_g, (((1,), (1,)), ((), ())), preferred_element_type=F32)
        st = state_scr[g]
        y_off = jnp.dot(c_g, st.astype(BF16), preferred_element_type=F32) * e_exp[:, gs]
        mms, blocks = [], []
        for k in range(SSD_HEADS_PER_GROUP):
            h = g * SSD_HEADS_PER_GROUP + k
            seg = a_cs[:, h:h + 1] - a_cs_t_adj[h:h + 1, :]
            mms.append((cb * jnp.exp(jnp.where(causal, seg, -jnp.inf))).astype(BF16))
            blocks.append(jnp.where(head_of_lane == k, xs_b, jnp.zeros_like(xs_b)))
        y_diag = jnp.dot(jnp.concatenate(mms, axis=1), jnp.concatenate(blocks, axis=0),
                         preferred_element_type=F32)
        y = y_diag + y_off + dskip[:, gs] * xs
        y_ref[rows, gs] = y.astype(y_ref.dtype)
        xw = (xs * w_exp[:, gs]).astype(BF16)
        upd = lax.dot_general(b_g, xw, (((0,), (0,)), ((), ())), preferred_element_type=F32)
        state_scr[g] = st * e_exp[q - 1:q, gs] + upd


def _ssd_kernel(act_ref, dt_ref, dtb_ref, alog_ref, dskip_ref, tri_ref, expand2_ref, y_ref, state_scr):
    @pl.when(pl.program_id(1) == 0)
    def _():
        state_scr[...] = jnp.zeros(state_scr.shape, F32)

    q = SSD_CHUNK
    chunks = [slice(ci * q, (ci + 1) * q) for ci in range(act_ref.shape[0] // q)]
    decay = [_ssd_decay(dt_ref[rows, :], dtb_ref[...], alog_ref[...], tri_ref[...]) for rows in chunks]
    spread = jnp.dot(jnp.concatenate([d[2] for d in decay], axis=0), expand2_ref[...],
                     preferred_element_type=F32)
    for ci, rows in enumerate(chunks):
        a_cs, a_cs_t_adj, _ = decay[ci]
        e_exp = spread[2 * ci * q:(2 * ci + 1) * q]
        w_exp = spread[(2 * ci + 1) * q:(2 * ci + 2) * q]
        _ssd_chunk(act_ref[rows, :], a_cs, a_cs_t_adj, e_exp, w_exp, dskip_ref[...], y_ref, rows, state_scr)


def _ssd(proj, dt_raw, dt_bias, a_log, d_skip_exp, tri, expand2, batch, seq, chunks_per_step):
    rows = SSD_CHUNK * chunks_per_step
    steps = seq // rows
    row = lambda b, c: b * steps + c
    const = lambda b, c: (0, 0)
    return pl.pallas_call(
        _ssd_kernel,
        grid=(batch, steps),
        in_specs=[
            pl.BlockSpec((rows, SSD_CONV_DIM), lambda b, c: (row(b, c), COL_XBC // SSD_CONV_DIM)),
            pl.BlockSpec((rows, LANES), lambda b, c: (row(b, c), 0)),
            pl.BlockSpec((1, LANES), const),
            pl.BlockSpec((1, LANES), const),
            pl.BlockSpec((1, D_INNER), const),
            pl.BlockSpec((SSD_CHUNK, SSD_CHUNK), const),
            pl.BlockSpec((2 * LANES, D_INNER), const),
        ],
        out_specs=pl.BlockSpec((rows, D_INNER), lambda b, c: (row(b, c), 0)),
        out_shape=jax.ShapeDtypeStruct((batch * seq, D_INNER), BF16),
        scratch_shapes=[pltpu.VMEM((SSD_GROUPS, D_STATE, SSD_GROUP_WIDTH), F32)],
        compiler_params=pltpu.CompilerParams(
            dimension_semantics=("arbitrary", "arbitrary"), vmem_limit_bytes=VMEM_LIMIT),
        name="ssd",
    )(proj, dt_raw, dt_bias, a_log, d_skip_exp, tri, expand2)


def _attn_kernel(q_ref, k_ref, v_ref, o_ref, lse_ref, stage, q_de, k_de, v_de, *maybe_mid, dil, n_back):
    blk = ATTN_BLOCK
    per_res = ATTN_SLAB_BLOCKS // dil
    run = per_res + 1
    first_slab = pl.program_id(1) == 0

    def keep_previous(dst):
        @pl.when(first_slab)
        def _():
            def body(res, carry):
                dst[res * run] = jnp.zeros(dst.shape[1:], BF16)
                return carry
            lax.fori_loop(0, dil, body, 0)

        @pl.when(jnp.logical_not(first_slab))
        def _():
            def body(res, carry):
                dst[res * run] = dst[res * run + per_res]
                return carry
            lax.fori_loop(0, dil, body, 0)

    def deinterleave(src_ref, dst, skip):
        if dil == 1:
            def copy(bi, carry):
                rows = pl.ds(pl.multiple_of(bi * blk, blk), blk)
                for c in range(ATTN_LANE_CHUNKS):
                    dst[bi + skip, c] = src_ref[rows, c * LANES:(c + 1) * LANES]
                return carry

            lax.fori_loop(0, ATTN_SLAB_BLOCKS, copy, 0)
            return

        for c in range(ATTN_LANE_CHUNKS):
            stage[c] = src_ref[:, c * LANES:(c + 1) * LANES].astype(F32)

        if maybe_mid:
            mid, = maybe_mid
            f = DEINTERLEAVE_FACTOR
            for c in range(ATTN_LANE_CHUNKS):
                for r in range(f):
                    mid[c * f + r] = stage[c, pl.ds(r, ATTN_SLAB // f, stride=f), :]

            def body(bi, carry):
                for c in range(ATTN_LANE_CHUNKS):
                    dst[bi + skip * (bi + 1), c] = (
                        mid[c * f + bi % f, pl.ds(bi // f, blk, stride=f), :].astype(BF16))
                return carry
        else:
            def body(bi, carry):
                res = bi // per_res
                start = res + dil * blk * (bi % per_res)
                for c in range(ATTN_LANE_CHUNKS):
                    dst[bi + skip * (res + 1), c] = stage[c, pl.ds(start, blk, stride=dil), :].astype(BF16)
                return carry

        lax.fori_loop(0, ATTN_SLAB_BLOCKS, body, 0)

    keep_previous(k_de)
    keep_previous(v_de)
    deinterleave(k_ref, k_de, 1)
    deinterleave(v_ref, v_de, 1)
    deinterleave(q_ref, q_de, 0)

    lane = lax.broadcasted_iota(jnp.int32, (blk, LANES), 1)
    even = lane < ATTN_HEAD_DIM
    qi = lax.broadcasted_iota(jnp.int32, (blk, 2 * blk), 0)
    ki = lax.broadcasted_iota(jnp.int32, (blk, 2 * blk), 1)
    dist = blk + qi - ki
    band = (dist >= 0) & (dist <= n_back)

    def attend(bi, carry):
        res = bi // per_res
        j = bi % per_res
        key_lo = jnp.where(jnp.logical_and(first_slab, j == 0), blk, 0)
        mask = band & (ki >= key_lo)
        mask2 = jnp.concatenate([mask, mask], axis=0)
        kv = res * run + j
        start = res + dil * blk * j
        lse_tile = jnp.zeros((blk, LANES), F32)
        for hp in range(ATTN_LANE_CHUNKS):
            q_pair = q_de[bi, hp]
            k2 = jnp.concatenate([k_de[kv, hp], k_de[kv + 1, hp]], axis=0)
            v2 = jnp.concatenate([v_de[kv, hp], v_de[kv + 1, hp]], axis=0)
            v2 = jnp.concatenate([v2, jnp.ones_like(v2)], axis=1)
            zero = jnp.zeros_like(q_pair)
            qm = jnp.concatenate([jnp.where(even, q_pair, zero), jnp.where(even, zero, q_pair)], axis=0)
            s = lax.dot_general(qm, k2, (((1,), (1,)), ((), ())), preferred_element_type=F32)
            s = jnp.where(mask2, s, -jnp.inf)
            m = jnp.max(s, axis=-1, keepdims=True)
            p = jnp.exp(s - m)
            pv = jnp.dot(p.astype(BF16), v2, preferred_element_type=F32)
            den = pv[:, LANES:]
            out = pv[:, :LANES] * (1.0 / den)
            lse = m + jnp.log(den)
            halves = [out[:blk], out[blk:]]
            for e in range(2):
                lse_tile = jnp.where(lane == 2 * hp + e, lse[e * blk:(e + 1) * blk], lse_tile)
            o_pair = jnp.where(even, halves[0], halves[1])
            if dil == 1:
                rows = pl.ds(pl.multiple_of(start, blk), blk)
                o_ref[rows, hp * LANES:(hp + 1) * LANES] = o_pair.astype(o_ref.dtype)
            else:
                stage[hp, pl.ds(start, blk, stride=dil), :] = o_pair
        lse_ref[pl.ds(start, blk, stride=dil), :] = lse_tile
        return carry

    lax.fori_loop(0, ATTN_SLAB_BLOCKS, attend, 0, unroll=ATTN_UNROLL)
    if dil > 1:
        for c in range(ATTN_LANE_CHUNKS):
            o_ref[:, c * LANES:(c + 1) * LANES] = stage[c].astype(o_ref.dtype)


def _attn_group(qkv, gi, batch, seq):
    dil = ATTN_DILATIONS[gi]
    n_back = ATTN_WINDOWS[gi] // dil
    assert n_back <= ATTN_BLOCK and ATTN_SLAB_BLOCKS % dil == 0 and seq % ATTN_SLAB == 0
    slabs = seq // ATTN_SLAB

    def cur(which):
        return pl.BlockSpec((ATTN_SLAB, ATTN_GROUP_WIDTH), lambda b, s: (b * slabs + s, which * ATTN_GROUPS + gi))

    run = ATTN_SLAB_BLOCKS // dil + 1
    de_block = (ATTN_LANE_CHUNKS, ATTN_BLOCK, LANES)
    f = DEINTERLEAVE_FACTOR
    two_pass = dil == f * f and dil == ATTN_SLAB_BLOCKS
    two_pass_scratch = [pltpu.VMEM((ATTN_LANE_CHUNKS * f, ATTN_SLAB // f, LANES), F32)] if two_pass else []
    return pl.pallas_call(
        functools.partial(_attn_kernel, dil=dil, n_back=n_back),
        grid=(batch, slabs),
        in_specs=[cur(0), cur(1), cur(2)],
        out_specs=[pl.BlockSpec((ATTN_SLAB, ATTN_GROUP_WIDTH), lambda b, s: (b * slabs + s, 0)),
                   pl.BlockSpec((ATTN_SLAB, LANES), lambda b, s: (b * slabs + s, 0))],
        out_shape=[jax.ShapeDtypeStruct((batch * seq, ATTN_GROUP_WIDTH), BF16),
                   jax.ShapeDtypeStruct((batch * seq, LANES), F32)],
        scratch_shapes=[
            pltpu.VMEM((ATTN_LANE_CHUNKS, ATTN_SLAB, LANES), F32),
            pltpu.VMEM((ATTN_SLAB_BLOCKS,) + de_block, BF16),
            pltpu.VMEM((dil * run,) + de_block, BF16),
            pltpu.VMEM((dil * run,) + de_block, BF16),
        ] + two_pass_scratch,
        compiler_params=pltpu.CompilerParams(
            dimension_semantics=("arbitrary", "arbitrary"), vmem_limit_bytes=VMEM_LIMIT),
        name=f"attn_g{gi}",
    )(qkv, qkv, qkv)


def _expand_heads(w, hexp):
    return jnp.dot(jnp.concatenate(_split_bf16(w, 2), axis=1), hexp, preferred_element_type=F32)


def _merge_kernel(x_ref, yssd_ref, gz_ref, o0_ref, o1_ref, o2_ref, l0_ref, l1_ref, l2_ref, gssd_ref, gattn_ref,
                  ng_ref, wssd_ref, wattn_ref, wout_ref, hexp_ref, x1_ref):
    l0, l1, l2 = l0_ref[...], l1_ref[...], l2_ref[...]
    mx = jnp.maximum(jnp.maximum(l0, l1), l2)
    e0, e1, e2 = jnp.exp(l0 - mx), jnp.exp(l1 - mx), jnp.exp(l2 - mx)
    inv = 1.0 / (e0 + e1 + e2)
    rows = o0_ref.shape[0]
    wexp = _expand_heads(jnp.concatenate([e0 * inv, e1 * inv, e2 * inv], axis=0), hexp_ref[...])
    y_attn = jnp.zeros(o0_ref.shape, F32)
    for gi, o_ref in enumerate((o0_ref, o1_ref, o2_ref)):
        y_attn = y_attn + wexp[gi * rows:(gi + 1) * rows] * o_ref[...].astype(F32)
    p_ssd = None
    for g in range(SSD_GROUPS):
        gs = slice(g * SSD_GROUP_WIDTH, (g + 1) * SSD_GROUP_WIDTH)
        y = yssd_ref[:, gs].astype(F32) * gz_ref[:, gs].astype(F32)
        yn = _rms_rows(y, ng_ref[:, gs]).astype(BF16)
        part = jnp.dot(yn, wssd_ref[gs, :], preferred_element_type=F32)
        p_ssd = part if p_ssd is None else p_ssd + part
    p_attn = jnp.dot(y_attn.astype(BF16), wattn_ref[...], preferred_element_type=F32)
    merged = (_sigmoid(gssd_ref[...].astype(F32)) * p_ssd
              + _sigmoid(gattn_ref[...].astype(F32)) * p_attn)
    x1 = x_ref[...] + jnp.dot(merged.astype(BF16), wout_ref[...], preferred_element_type=F32)
    x1_ref[...] = x1


def _merge(x2, y_ssd, outs, lses, proj, ssd_norm_g, w_ssd, w_attn, w_out, head_expand, tm):
    m = x2.shape[0]
    rows = lambda width, cb=0: pl.BlockSpec((tm, width), lambda i: (i, cb))
    const = lambda shape: pl.BlockSpec(shape, lambda i: (0, 0))
    return pl.pallas_call(
        _merge_kernel,
        grid=(m // tm,),
        in_specs=[rows(D_MODEL), rows(D_INNER), rows(D_INNER, COL_Z // D_INNER),
                  rows(ATTN_GROUP_WIDTH), rows(ATTN_GROUP_WIDTH), rows(ATTN_GROUP_WIDTH),
                  rows(LANES), rows(LANES), rows(LANES),
                  rows(D_MODEL, COL_GSSD // D_MODEL), rows(D_MODEL, COL_GATTN // D_MODEL),
                  const((1, D_INNER)), const((D_INNER, D_MODEL)), const((ATTN_GROUP_WIDTH, D_MODEL)),
                  const((D_MODEL, D_MODEL)),
                  const((2 * LANES, ATTN_GROUP_WIDTH))],
        out_specs=rows(D_MODEL),
        out_shape=jax.ShapeDtypeStruct((m, D_MODEL), F32),
        compiler_params=pltpu.CompilerParams(
            dimension_semantics=("arbitrary",), vmem_limit_bytes=VMEM_LIMIT),
        name="merge",
    )(x2, y_ssd, proj, outs[0], outs[1], outs[2], lses[0], lses[1], lses[2], proj, proj,
      ssd_norm_g, w_ssd, w_attn, w_out, head_expand)


def _ffn_kernel(x1_ref, g2_ref, wg_ref, wv_ref, cwg_ref, cwv_ref, cbg_ref, cbv_ref, wd_ref, out_ref,
                ug_scr, uv_scr, halo_g, halo_v, *, tiles_per_seq):
    i = pl.program_id(0)
    tm = x1_ref.shape[0]
    tf = wg_ref.shape[1]
    first = (i % tiles_per_seq) == 0

    @pl.when(i == 0)
    def _():
        halo_g[...] = jnp.zeros(halo_g.shape, F32)
        halo_v[...] = jnp.zeros(halo_v.shape, F32)

    def conv(u, u_scr, halo, w_ref, b_ref, c, scale):
        cs = slice(c * LANES, (c + 1) * LANES)
        u_scr[c, 0:HALO, :] = jnp.where(first, 0.0, halo[c])
        u_scr[c, HALO:HALO + tm, :] = u
        acc = scale * b_ref[:, cs]
        for t in range(FFN_CONV):
            tap = u_scr[c, pl.ds(HALO - (FFN_CONV - 1) + t, tm, stride=1), :]
            acc = acc + tap * (scale * w_ref[t:t + 1, cs])
        halo[c] = u_scr[c, tm:tm + HALO, :]
        return acc

    h = _rms_rows(x1_ref[...], g2_ref[...]).astype(BF16)
    per = FFN_CHUNK // LANES
    total = None
    for fc in range(tf // FFN_CHUNK):
        cols = slice(fc * FFN_CHUNK, (fc + 1) * FFN_CHUNK)
        ug = jnp.dot(h, wg_ref[:, cols], preferred_element_type=F32)
        uv = jnp.dot(h, wv_ref[:, cols], preferred_element_type=F32)
        acts = []
        for ci in range(per):
            c = fc * per + ci
            ls = slice(ci * LANES, (ci + 1) * LANES)
            gate = conv(ug[:, ls], ug_scr, halo_g, cwg_ref, cbg_ref, c, 0.5)
            val = conv(uv[:, ls], uv_scr, halo_v, cwv_ref, cbv_ref, c, 1.0)
            acts.append((_silu_of_half(gate) * val).astype(BF16))
        part = jnp.dot(jnp.concatenate(acts, axis=1), wd_ref[cols, :], preferred_element_type=F32)
        total = part if total is None else total + part
    out_ref[...] = x1_ref[...] + total


def _ffn(x1, g2, w_up, conv_w, conv_b, w_down, tm, seq):
    m = x1.shape[0]
    tf = D_FF
    once = dict(pipeline_mode=pl.Buffered(1))
    gate_half = lambda i: (0, 0)
    value_half = lambda i: (0, 1)
    return pl.pallas_call(
        functools.partial(_ffn_kernel, tiles_per_seq=seq // tm),
        grid=(m // tm,),
        in_specs=[
            pl.BlockSpec((tm, D_MODEL), lambda i: (i, 0)),
            pl.BlockSpec((1, D_MODEL), lambda i: (0, 0), **once),
            pl.BlockSpec((D_MODEL, tf), gate_half, **once),
            pl.BlockSpec((D_MODEL, tf), value_half, **once),
            pl.BlockSpec((FFN_CONV, tf), gate_half, **once),
            pl.BlockSpec((FFN_CONV, tf), value_half, **once),
            pl.BlockSpec((1, tf), gate_half, **once),
            pl.BlockSpec((1, tf), value_half, **once),
            pl.BlockSpec((tf, D_MODEL), lambda i: (0, 0), **once),
        ],
        out_specs=pl.BlockSpec((tm, D_MODEL), lambda i: (i, 0)),
        out_shape=jax.ShapeDtypeStruct((m, D_MODEL), F32),
        scratch_shapes=[
            pltpu.VMEM((tf // LANES, tm + HALO, LANES), F32),
            pltpu.VMEM((tf // LANES, tm + HALO, LANES), F32),
            pltpu.VMEM((tf // LANES, HALO, LANES), F32),
            pltpu.VMEM((tf // LANES, HALO, LANES), F32),
        ],
        compiler_params=pltpu.CompilerParams(
            dimension_semantics=("arbitrary",), vmem_limit_bytes=VMEM_LIMIT),
        name="ffn",
    )(x1, g2, w_up, w_up, conv_w, conv_w, conv_b, conv_b, w_down)


def _pad_lanes(v):
    return jnp.pad(v.astype(F32), (0, LANES - v.shape[0])).reshape(1, LANES)


def kernel(x, norm1_g, w_in, ssd_conv_w, ssd_conv_b, dt_bias, a_log, d_skip, ssd_norm_g, w_ssd_proj,
           q_norm_g, k_norm_g, w_attn_proj, w_out, norm2_g, w_up, ffn_conv_w, ffn_conv_b, w_down):
    batch, seq, _ = x.shape
    assert norm1_g.shape[0] == 1, "single-layer block"
    x2 = x.reshape(batch * seq, D_MODEL)
    g1 = norm1_g[0].reshape(1, D_MODEL)

    w = w_in[0]
    o_xbc = D_INNER
    o_dt = o_xbc + SSD_CONV_DIM
    o_q = o_dt + SSD_HEADS
    o_g = o_q + 3 * ATTN_WIDTH
    w_zx = w[:, :o_dt].astype(BF16)
    w_gate = w[:, o_g:].astype(BF16)
    w_dt = jnp.pad(w[:, o_dt:o_q], ((0, 0), (0, LANES - SSD_HEADS))).astype(BF16)
    w_qkv = w[:, o_q:o_g].astype(BF16)

    proj, dt_raw, h = _in_proj_main(x2, g1, w_zx, w_gate, w_dt, ssd_conv_w[0], ssd_conv_b[0].reshape(1, -1),
                                    tm=TM_IN_PROJ, tn=TN_IN_PROJ, seq=seq)

    heads_per_tile = MXU_WIDTH // ATTN_HEAD_DIM
    qk_gains = jnp.stack([jnp.tile(q_norm_g[0] * (ATTN_HEAD_DIM ** -0.5), heads_per_tile),
                          jnp.tile(k_norm_g[0], heads_per_tile)]).astype(F32)
    lane_head = jnp.arange(MXU_WIDTH) // ATTN_HEAD_DIM
    seg = (lane_head[:, None] == lane_head[None, :]).astype(BF16)
    qkv = _in_proj_qkv(h, w_qkv, qk_gains, seg, tm=TM_IN_PROJ)

    tri = jnp.tril(jnp.ones((SSD_CHUNK, SSD_CHUNK), BF16))
    head_of_col = jnp.arange(D_INNER) // SSD_HEAD_DIM
    expand = (jnp.arange(LANES)[:, None] == head_of_col[None, :]).astype(BF16)
    y_ssd = _ssd(proj, dt_raw, _pad_lanes(dt_bias[0]), _pad_lanes(a_log[0]),
                 jnp.repeat(d_skip[0], SSD_HEAD_DIM).reshape(1, D_INNER),
                 tri, jnp.concatenate([expand, expand], axis=0),
                 batch, seq, chunks_per_step=SSD_CHUNKS_PER_STEP)

    outs, lses = zip(*[_attn_group(qkv, gi, batch, seq) for gi in range(ATTN_GROUPS)])

    attn_head_of_col = jnp.arange(ATTN_GROUP_WIDTH) // ATTN_HEAD_DIM
    head_expand = (jnp.arange(2 * LANES)[:, None] % LANES == attn_head_of_col[None, :]).astype(BF16)
    x1 = _merge(x2, y_ssd, outs, lses, proj, ssd_norm_g[0].reshape(1, D_INNER),
                w_ssd_proj[0].astype(BF16), w_attn_proj[0].astype(BF16), w_out[0].astype(BF16),
                head_expand, tm=TM_MERGE)

    out = _ffn(x1, norm2_g[0].reshape(1, D_MODEL), w_up[0].astype(BF16), ffn_conv_w[0],
               ffn_conv_b[0].reshape(1, -1), w_down[0].astype(BF16), tm=TM_FFN, seq=seq)
    return out.reshape(batch, seq, D_MODEL)
```

```python
import functools

import jax
import jax.numpy as jnp
from jax import lax
from jax.experimental import pallas as pl
from jax.experimental.pallas import tpu as pltpu

F32 = jnp.float32
BF16 = jnp.bfloat16

D_MODEL = 1024
D_INNER = 2048
SSD_HEADS = 32
SSD_HEAD_DIM = 64
SSD_GROUPS = 8
SSD_HEADS_PER_GROUP = SSD_HEADS // SSD_GROUPS
D_STATE = 128
SSD_CONV = 4
SSD_CHUNK = 128
SSD_CONV_DIM = D_INNER + 2 * SSD_GROUPS * D_STATE
SSD_GROUP_WIDTH = D_INNER // SSD_GROUPS
ATTN_WINDOWS = (128, 512, 2048)
ATTN_DILATIONS = (1, 4, 16)
ATTN_GROUPS = 3
ATTN_HEADS = 8
ATTN_HEAD_DIM = 64
ATTN_BLOCK = 128
ATTN_GROUP_WIDTH = ATTN_HEADS * ATTN_HEAD_DIM
ATTN_WIDTH = ATTN_GROUPS * ATTN_GROUP_WIDTH
D_FF = 2816
FFN_CONV = 3
FFN_CHUNK = D_FF // 2
EPS = 1e-6
LANES = 128
MXU_WIDTH = 256
HALO = 8
ATTN_SLAB = ATTN_BLOCK * max(ATTN_DILATIONS)
ATTN_SLAB_BLOCKS = ATTN_SLAB // ATTN_BLOCK
ATTN_LANE_CHUNKS = ATTN_GROUP_WIDTH // LANES
ATTN_UNROLL = 16
DEINTERLEAVE_FACTOR = 4

COL_XBC = 0
COL_Z = COL_XBC + SSD_CONV_DIM
COL_GSSD = COL_Z + D_INNER
COL_GATTN = COL_GSSD + D_MODEL
PROJ_WIDTH = COL_GATTN + D_MODEL

V7X_VMEM_BYTES = 64 * 1024 * 1024
VMEM_LIMIT = V7X_VMEM_BYTES * 7 // 8

TM_IN_PROJ = 1024
TN_IN_PROJ = 2048
TM_MERGE = 512
TM_FFN = 512
SSD_CHUNKS_PER_STEP = 4


def _sigmoid(v):
    return 0.5 * jnp.tanh(0.5 * v) + 0.5


def _silu_of_half(hv):
    return hv + hv * jnp.tanh(hv)


def _split_bf16(v, terms):
    parts = []
    for _ in range(terms):
        p = v.astype(BF16)
        parts.append(p)
        v = v - p.astype(F32)
    return parts


def _dot_split(parts, rhs, lhs_side=True):
    acc = None
    for p in parts:
        t = (jnp.dot(p, rhs, preferred_element_type=F32) if lhs_side
             else jnp.dot(rhs, p, preferred_element_type=F32))
        acc = t if acc is None else acc + t
    return acc


def _rms_rows(x, gain):
    ms = jnp.mean(x * x, axis=-1, keepdims=True)
    return x * lax.rsqrt(ms + EPS) * gain


def _in_proj_main_kernel(x_ref, g_ref, w_ref, wgate_ref, wdt_ref, cw_ref, cb_ref, proj_ref, dt_ref, h_ref,
                         xp_scr, halo_scr, *, tiles_per_seq):
    i = pl.program_id(0)
    j = pl.program_id(1)
    tm, tn = proj_ref.shape
    conv_tiles = SSD_CONV_DIM // tn
    z_tiles = D_INNER // tn

    @pl.when(j == 0)
    def _():
        h = _rms_rows(x_ref[...], g_ref[...]).astype(BF16)
        h_ref[...] = h
        dt_ref[...] = jnp.dot(h, wdt_ref[...], preferred_element_type=F32)

    @pl.when(jnp.logical_and(i == 0, j == 0))
    def _():
        halo_scr[...] = jnp.zeros(halo_scr.shape, F32)

    @pl.when(j < conv_tiles)
    def _():
        r = jnp.dot(h_ref[...], w_ref[...], preferred_element_type=F32)
        first = (i % tiles_per_seq) == 0
        for c in range(tn // LANES):
            cs = slice(c * LANES, (c + 1) * LANES)
            xp_scr[c, 0:HALO, :] = jnp.where(first, 0.0, halo_scr[j, c])
            xp_scr[c, HALO:HALO + tm, :] = r[:, cs]
            acc = 0.5 * cb_ref[:, cs]
            for t in range(SSD_CONV):
                tap = xp_scr[c, pl.ds(HALO - (SSD_CONV - 1) + t, tm, stride=1), :]
                acc = acc + tap * (0.5 * cw_ref[t:t + 1, cs])
            proj_ref[:, cs] = _silu_of_half(acc).astype(proj_ref.dtype)
            halo_scr[j, c] = xp_scr[c, tm:tm + HALO, :]

    @pl.when(jnp.logical_and(j >= conv_tiles, j < conv_tiles + z_tiles))
    def _():
        r = jnp.dot(h_ref[...], w_ref[...], preferred_element_type=F32)
        proj_ref[...] = _silu_of_half(0.5 * r).astype(proj_ref.dtype)

    @pl.when(j >= conv_tiles + z_tiles)
    def _():
        proj_ref[...] = jnp.dot(h_ref[...], wgate_ref[...], preferred_element_type=F32).astype(proj_ref.dtype)


def _in_proj_main(x2, g1, w_zx, w_gate, w_dt, conv_w, conv_b, tm, tn, seq):
    m = x2.shape[0]
    assert SSD_CONV_DIM % tn == 0 and D_INNER % tn == 0 and seq % tm == 0 and w_gate.shape[1] == tn
    conv_tiles = SSD_CONV_DIM // tn
    z_tiles = D_INNER // tn
    conv_col = lambda i, j: (0, jnp.minimum(j, conv_tiles - 1))
    zx_col = lambda i, j: (0, jnp.where(j < conv_tiles, j + z_tiles, jnp.minimum(j - conv_tiles, z_tiles - 1)))
    return pl.pallas_call(
        functools.partial(_in_proj_main_kernel, tiles_per_seq=seq // tm),
        grid=(m // tm, PROJ_WIDTH // tn),
        in_specs=[
            pl.BlockSpec((tm, D_MODEL), lambda i, j: (i, 0)),
            pl.BlockSpec((1, D_MODEL), lambda i, j: (0, 0)),
            pl.BlockSpec((D_MODEL, tn), zx_col),
            pl.BlockSpec((D_MODEL, tn), lambda i, j: (0, 0), pipeline_mode=pl.Buffered(1)),
            pl.BlockSpec((D_MODEL, LANES), lambda i, j: (0, 0)),
            pl.BlockSpec((SSD_CONV, tn), conv_col),
            pl.BlockSpec((1, tn), conv_col),
        ],
        out_specs=[
            pl.BlockSpec((tm, tn), lambda i, j: (i, j)),
            pl.BlockSpec((tm, LANES), lambda i, j: (i, 0)),
            pl.BlockSpec((tm, D_MODEL), lambda i, j: (i, 0)),
        ],
        out_shape=[
            jax.ShapeDtypeStruct((m, PROJ_WIDTH), BF16),
            jax.ShapeDtypeStruct((m, LANES), F32),
            jax.ShapeDtypeStruct((m, D_MODEL), BF16),
        ],
        scratch_shapes=[
            pltpu.VMEM((tn // LANES, HALO + tm, LANES), F32),
            pltpu.VMEM((conv_tiles, tn // LANES, HALO, LANES), F32),
        ],
        compiler_params=pltpu.CompilerParams(
            dimension_semantics=("arbitrary", "arbitrary"), vmem_limit_bytes=VMEM_LIMIT),
        name="in_proj_main",
    )(x2, g1, w_zx, w_gate, w_dt, conv_w, conv_b)


def _in_proj_qkv_kernel(h_ref, w_ref, gain_ref, seg_ref, out_ref, t_scr):
    j = pl.program_id(1)
    seg_w = seg_ref.shape[0]

    @pl.when(j < 2)
    def _():
        t_scr[...] = jnp.dot(h_ref[...], w_ref[...], preferred_element_type=F32)
        gain = gain_ref[pl.ds(j, 1), :]
        tm = t_scr.shape[0]
        n_seg = ATTN_WIDTH // seg_w
        sq = [(t_scr[:, c * seg_w:(c + 1) * seg_w] ** 2).astype(BF16) for c in range(n_seg)]
        ss = jnp.dot(jnp.concatenate(sq, axis=0), seg_ref[...], preferred_element_type=F32)
        for c in range(n_seg):
            cs = slice(c * seg_w, (c + 1) * seg_w)
            inv_rms = lax.rsqrt(ss[c * tm:(c + 1) * tm] * (1.0 / ATTN_HEAD_DIM) + EPS)
            out_ref[:, cs] = (t_scr[:, cs] * inv_rms * gain).astype(out_ref.dtype)

    @pl.when(j == 2)
    def _():
        out_ref[...] = jnp.dot(h_ref[...], w_ref[...], preferred_element_type=F32).astype(out_ref.dtype)


def _in_proj_qkv(h, w_qkv, gains, seg, tm):
    m = h.shape[0]
    return pl.pallas_call(
        _in_proj_qkv_kernel,
        grid=(m // tm, 3),
        in_specs=[
            pl.BlockSpec((tm, D_MODEL), lambda i, j: (i, 0)),
            pl.BlockSpec((D_MODEL, ATTN_WIDTH), lambda i, j: (0, j)),
            pl.BlockSpec(gains.shape, lambda i, j: (0, 0)),
            pl.BlockSpec(seg.shape, lambda i, j: (0, 0)),
        ],
        out_specs=pl.BlockSpec((tm, ATTN_WIDTH), lambda i, j: (i, j)),
        out_shape=jax.ShapeDtypeStruct((m, 3 * ATTN_WIDTH), BF16),
        scratch_shapes=[pltpu.VMEM((tm, ATTN_WIDTH), F32)],
        compiler_params=pltpu.CompilerParams(
            dimension_semantics=("arbitrary", "arbitrary"), vmem_limit_bytes=VMEM_LIMIT),
        name="in_proj_qkv",
    )(h, w_qkv, gains, seg)


def _ssd_decay(dt_raw, dtb, a_log, tri):
    q = SSD_CHUNK
    dt = jax.nn.softplus(dt_raw + dtb)
    a = dt * (-jnp.exp(a_log))
    a_cs = _dot_split(_split_bf16(a, 3), tri, lhs_side=False)
    a_cs_t_adj = a_cs.T - jnp.log(dt.T)
    e_in = jnp.exp(a_cs)
    w_out = jnp.exp(a_cs[q - 1:q, :] - a_cs) * dt
    spread = jnp.concatenate([jnp.concatenate(_split_bf16(e_in, 2), axis=1),
                              jnp.concatenate(_split_bf16(w_out, 2), axis=1)], axis=0)
    return a_cs, a_cs_t_adj, spread


def _ssd_chunk(act, a_cs, a_cs_t_adj, e_exp, w_exp, dskip, y_ref, rows, state_scr):
    q = SSD_CHUNK
    li = lax.broadcasted_iota(jnp.int32, (q, q), 0)
    si = lax.broadcasted_iota(jnp.int32, (q, q), 1)
    causal = li >= si
    head_of_lane = lax.broadcasted_iota(jnp.int32, (q, SSD_GROUP_WIDTH), 1) // SSD_HEAD_DIM

    for g in range(SSD_GROUPS):
        gs = slice(g * SSD_GROUP_WIDTH, (g + 1) * SSD_GROUP_WIDTH)
        xs_b = act[:, gs]
        xs = xs_b.astype(F32)
        b_g = act[:, D_INNER + g * D_STATE:D_INNER + (g + 1) * D_STATE]
        c_g = act[:, D_INNER + (SSD_GROUPS + g) * D_STATE:D_INNER + (SSD_GROUPS + g + 1) * D_STATE]
        cb = lax.dot_general(c_g, b_g, (((1,), (1,)), ((), ())), preferred_element_type=F32)
        st = state_scr[g]
        y_off = jnp.dot(c_g, st.astype(BF16), preferred_element_type=F32) * e_exp[:, gs]
        mms, blocks = [], []
        for k in range(SSD_HEADS_PER_GROUP):
            h = g * SSD_HEADS_PER_GROUP + k
            seg = a_cs[:, h:h + 1] - a_cs_t_adj[h:h + 1, :]
            mms.append((cb * jnp.exp(jnp.where(causal, seg, -jnp.inf))).astype(BF16))
            blocks.append(jnp.where(head_of_lane == k, xs_b, jnp.zeros_like(xs_b)))
        y_diag = jnp.dot(jnp.concatenate(mms, axis=1), jnp.concatenate(blocks, axis=0),
                         preferred_element_type=F32)
        y = y_diag + y_off + dskip[:, gs] * xs
        y_ref[rows, gs] = y.astype(y_ref.dtype)
        xw = (xs * w_exp[:, gs]).astype(BF16)
        upd = lax.dot_general(b_g, xw, (((0,), (0,)), ((), ())), preferred_element_type=F32)
        state_scr[g] = st * e_exp[q - 1:q, gs] + upd


def _ssd_kernel(act_ref, dt_ref, dtb_ref, alog_ref, dskip_ref, tri_ref, expand2_ref, y_ref, state_scr):
    @pl.when(pl.program_id(1) == 0)
    def _():
        state_scr[...] = jnp.zeros(state_scr.shape, F32)

    q = SSD_CHUNK
    chunks = [slice(ci * q, (ci + 1) * q) for ci in range(act_ref.shape[0] // q)]
    decay = [_ssd_decay(dt_ref[rows, :], dtb_ref[...], alog_ref[...], tri_ref[...]) for rows in chunks]
    spread = jnp.dot(jnp.concatenate([d[2] for d in decay], axis=0), expand2_ref[...],
                     preferred_element_type=F32)
    for ci, rows in enumerate(chunks):
        a_cs, a_cs_t_adj, _ = decay[ci]
        e_exp = spread[2 * ci * q:(2 * ci + 1) * q]
        w_exp = spread[(2 * ci + 1) * q:(2 * ci + 2) * q]
        _ssd_chunk(act_ref[rows, :], a_cs, a_cs_t_adj, e_exp, w_exp, dskip_ref[...], y_ref, rows, state_scr)


def _ssd(proj, dt_raw, dt_bias, a_log, d_skip_exp, tri, expand2, batch, seq, chunks_per_step):
    rows = SSD_CHUNK * chunks_per_step
    steps = seq // rows
    row = lambda b, c: b * steps + c
    const = lambda b, c: (0, 0)
    return pl.pallas_call(
        _ssd_kernel,
        grid=(batch, steps),
        in_specs=[
            pl.BlockSpec((rows, SSD_CONV_DIM), lambda b, c: (row(b, c), COL_XBC // SSD_CONV_DIM)),
            pl.BlockSpec((rows, LANES), lambda b, c: (row(b, c), 0)),
            pl.BlockSpec((1, LANES), const),
            pl.BlockSpec((1, LANES), const),
            pl.BlockSpec((1, D_INNER), const),
            pl.BlockSpec((SSD_CHUNK, SSD_CHUNK), const),
            pl.BlockSpec((2 * LANES, D_INNER), const),
        ],
        out_specs=pl.BlockSpec((rows, D_INNER), lambda b, c: (row(b, c), 0)),
        out_shape=jax.ShapeDtypeStruct((batch * seq, D_INNER), BF16),
        scratch_shapes=[pltpu.VMEM((SSD_GROUPS, D_STATE, SSD_GROUP_WIDTH), F32)],
        compiler_params=pltpu.CompilerParams(
            dimension_semantics=("arbitrary", "arbitrary"), vmem_limit_bytes=VMEM_LIMIT),
        name="ssd",
    )(proj, dt_raw, dt_bias, a_log, d_skip_exp, tri, expand2)


def _attn_kernel(q_ref, k_ref, v_ref, o_ref, lse_ref, stage, q_de, k_de, v_de, *maybe_mid, dil, n_back):
    blk = ATTN_BLOCK
    per_res = ATTN_SLAB_BLOCKS // dil
    run = per_res + 1
    first_slab = pl.program_id(1) == 0

    def keep_previous(dst):
        @pl.when(first_slab)
        def _():
            def body(res, carry):
                dst[res * run] = jnp.zeros(dst.shape[1:], BF16)
                return carry
            lax.fori_loop(0, dil, body, 0)

        @pl.when(jnp.logical_not(first_slab))
        def _():
            def body(res, carry):
                dst[res * run] = dst[res * run + per_res]
                return carry
            lax.fori_loop(0, dil, body, 0)

    def deinterleave(src_ref, dst, skip):
        if dil == 1:
            def copy(bi, carry):
                rows = pl.ds(pl.multiple_of(bi * blk, blk), blk)
                for c in range(ATTN_LANE_CHUNKS):
                    dst[bi + skip, c] = src_ref[rows, c * LANES:(c + 1) * LANES]
                return carry

            lax.fori_loop(0, ATTN_SLAB_BLOCKS, copy, 0)
            return

        for c in range(ATTN_LANE_CHUNKS):
            stage[c] = src_ref[:, c * LANES:(c + 1) * LANES].astype(F32)

        if maybe_mid:
            mid, = maybe_mid
            f = DEINTERLEAVE_FACTOR
            for c in range(ATTN_LANE_CHUNKS):
                for r in range(f):
                    mid[c * f + r] = stage[c, pl.ds(r, ATTN_SLAB // f, stride=f), :]

            def body(bi, carry):
                for c in range(ATTN_LANE_CHUNKS):
                    dst[bi + skip * (bi + 1), c] = (
                        mid[c * f + bi % f, pl.ds(bi // f, blk, stride=f), :].astype(BF16))
                return carry
        else:
            def body(bi, carry):
                res = bi // per_res
                start = res + dil * blk * (bi % per_res)
                for c in range(ATTN_LANE_CHUNKS):
                    dst[bi + skip * (res + 1), c] = stage[c, pl.ds(start, blk, stride=dil), :].astype(BF16)
                return carry

        lax.fori_loop(0, ATTN_SLAB_BLOCKS, body, 0)

    keep_previous(k_de)
    keep_previous(v_de)
    deinterleave(k_ref, k_de, 1)
    deinterleave(v_ref, v_de, 1)
    deinterleave(q_ref, q_de, 0)

    lane = lax.broadcasted_iota(jnp.int32, (blk, LANES), 1)
    even = lane < ATTN_HEAD_DIM
    qi = lax.broadcasted_iota(jnp.int32, (blk, 2 * blk), 0)
    ki = lax.broadcasted_iota(jnp.int32, (blk, 2 * blk), 1)
    dist = blk + qi - ki
    band = (dist >= 0) & (dist <= n_back)

    def attend(bi, carry):
        res = bi // per_res
        j = bi % per_res
        key_lo = jnp.where(jnp.logical_and(first_slab, j == 0), blk, 0)
        mask = band & (ki >= key_lo)
        mask2 = jnp.concatenate([mask, mask], axis=0)
        kv = res * run + j
        start = res + dil * blk * j
        lse_tile = jnp.zeros((blk, LANES), F32)
        for hp in range(ATTN_LANE_CHUNKS):
            q_pair = q_de[bi, hp]
            k2 = jnp.concatenate([k_de[kv, hp], k_de[kv + 1, hp]], axis=0)
            v2 = jnp.concatenate([v_de[kv, hp], v_de[kv + 1, hp]], axis=0)
            v2 = jnp.concatenate([v2, jnp.ones_like(v2)], axis=1)
            zero = jnp.zeros_like(q_pair)
            qm = jnp.concatenate([jnp.where(even, q_pair, zero), jnp.where(even, zero, q_pair)], axis=0)
            s = lax.dot_general(qm, k2, (((1,), (1,)), ((), ())), preferred_element_type=F32)
            s = jnp.where(mask2, s, -jnp.inf)
            m = jnp.max(s, axis=-1, keepdims=True)
            p = jnp.exp(s - m)
            pv = jnp.dot(p.astype(BF16), v2, preferred_element_type=F32)
            den = pv[:, LANES:]
            out = pv[:, :LANES] * (1.0 / den)
            lse = m + jnp.log(den)
            halves = [out[:blk], out[blk:]]
            for e in range(2):
                lse_tile = jnp.where(lane == 2 * hp + e, lse[e * blk:(e + 1) * blk], lse_tile)
            o_pair = jnp.where(even, halves[0], halves[1])
            if dil == 1:
                rows = pl.ds(pl.multiple_of(start, blk), blk)
                o_ref[rows, hp * LANES:(hp + 1) * LANES] = o_pair.astype(o_ref.dtype)
            elif maybe_mid:
                f = DEINTERLEAVE_FACTOR
                maybe_mid[0][hp * f + bi % f, pl.ds(bi // f, blk, stride=f), :] = o_pair
            else:
                stage[hp, pl.ds(start, blk, stride=dil), :] = o_pair
        lse_ref[pl.ds(start, blk, stride=dil), :] = lse_tile
        return carry

    lax.fori_loop(0, ATTN_SLAB_BLOCKS, attend, 0, unroll=ATTN_UNROLL)
    if maybe_mid:
        f = DEINTERLEAVE_FACTOR
        for c in range(ATTN_LANE_CHUNKS):
            for r in range(f):
                stage[c, pl.ds(r, ATTN_SLAB // f, stride=f), :] = maybe_mid[0][c * f + r]
    if dil > 1:
        for c in range(ATTN_LANE_CHUNKS):
            o_ref[:, c * LANES:(c + 1) * LANES] = stage[c].astype(o_ref.dtype)


def _attn_group(qkv, gi, batch, seq):
    dil = ATTN_DILATIONS[gi]
    n_back = ATTN_WINDOWS[gi] // dil
    assert n_back <= ATTN_BLOCK and ATTN_SLAB_BLOCKS % dil == 0 and seq % ATTN_SLAB == 0
    slabs = seq // ATTN_SLAB

    def cur(which):
        return pl.BlockSpec((ATTN_SLAB, ATTN_GROUP_WIDTH), lambda b, s: (b * slabs + s, which * ATTN_GROUPS + gi))

    run = ATTN_SLAB_BLOCKS // dil + 1
    de_block = (ATTN_LANE_CHUNKS, ATTN_BLOCK, LANES)
    f = DEINTERLEAVE_FACTOR
    two_pass = dil == f * f and dil == ATTN_SLAB_BLOCKS
    two_pass_scratch = [pltpu.VMEM((ATTN_LANE_CHUNKS * f, ATTN_SLAB // f, LANES), F32)] if two_pass else []
    return pl.pallas_call(
        functools.partial(_attn_kernel, dil=dil, n_back=n_back),
        grid=(batch, slabs),
        in_specs=[cur(0), cur(1), cur(2)],
        out_specs=[pl.BlockSpec((ATTN_SLAB, ATTN_GROUP_WIDTH), lambda b, s: (b * slabs + s, 0)),
                   pl.BlockSpec((ATTN_SLAB, LANES), lambda b, s: (b * slabs + s, 0))],
        out_shape=[jax.ShapeDtypeStruct((batch * seq, ATTN_GROUP_WIDTH), BF16),
                   jax.ShapeDtypeStruct((batch * seq, LANES), F32)],
        scratch_shapes=[
            pltpu.VMEM((ATTN_LANE_CHUNKS, ATTN_SLAB, LANES), F32),
            pltpu.VMEM((ATTN_SLAB_BLOCKS,) + de_block, BF16),
            pltpu.VMEM((dil * run,) + de_block, BF16),
            pltpu.VMEM((dil * run,) + de_block, BF16),
        ] + two_pass_scratch,
        compiler_params=pltpu.CompilerParams(
            dimension_semantics=("arbitrary", "arbitrary"), vmem_limit_bytes=VMEM_LIMIT),
        name=f"attn_g{gi}",
    )(qkv, qkv, qkv)


def _expand_heads(w, hexp):
    return jnp.dot(jnp.concatenate(_split_bf16(w, 2), axis=1), hexp, preferred_element_type=F32)


def _merge_kernel(x_ref, yssd_ref, gz_ref, o0_ref, o1_ref, o2_ref, l0_ref, l1_ref, l2_ref, gssd_ref, gattn_ref,
                  ng_ref, wssd_ref, wattn_ref, wout_ref, hexp_ref, x1_ref):
    l0, l1, l2 = l0_ref[...], l1_ref[...], l2_ref[...]
    mx = jnp.maximum(jnp.maximum(l0, l1), l2)
    e0, e1, e2 = jnp.exp(l0 - mx), jnp.exp(l1 - mx), jnp.exp(l2 - mx)
    inv = 1.0 / (e0 + e1 + e2)
    rows = o0_ref.shape[0]
    wexp = _expand_heads(jnp.concatenate([e0 * inv, e1 * inv, e2 * inv], axis=0), hexp_ref[...])
    y_attn = jnp.zeros(o0_ref.shape, F32)
    for gi, o_ref in enumerate((o0_ref, o1_ref, o2_ref)):
        y_attn = y_attn + wexp[gi * rows:(gi + 1) * rows] * o_ref[...].astype(F32)
    p_ssd = None
    for g in range(SSD_GROUPS):
        gs = slice(g * SSD_GROUP_WIDTH, (g + 1) * SSD_GROUP_WIDTH)
        y = yssd_ref[:, gs].astype(F32) * gz_ref[:, gs].astype(F32)
        yn = _rms_rows(y, ng_ref[:, gs]).astype(BF16)
        part = jnp.dot(yn, wssd_ref[gs, :], preferred_element_type=F32)
        p_ssd = part if p_ssd is None else p_ssd + part
    p_attn = jnp.dot(y_attn.astype(BF16), wattn_ref[...], preferred_element_type=F32)
    merged = (_sigmoid(gssd_ref[...].astype(F32)) * p_ssd
              + _sigmoid(gattn_ref[...].astype(F32)) * p_attn)
    x1 = x_ref[...] + jnp.dot(merged.astype(BF16), wout_ref[...], preferred_element_type=F32)
    x1_ref[...] = x1


def _merge(x2, y_ssd, outs, lses, proj, ssd_norm_g, w_ssd, w_attn, w_out, head_expand, tm):
    m = x2.shape[0]
    rows = lambda width, cb=0: pl.BlockSpec((tm, width), lambda i: (i, cb))
    const = lambda shape: pl.BlockSpec(shape, lambda i: (0, 0))
    return pl.pallas_call(
        _merge_kernel,
        grid=(m // tm,),
        in_specs=[rows(D_MODEL), rows(D_INNER), rows(D_INNER, COL_Z // D_INNER),
                  rows(ATTN_GROUP_WIDTH), rows(ATTN_GROUP_WIDTH), rows(ATTN_GROUP_WIDTH),
                  rows(LANES), rows(LANES), rows(LANES),
                  rows(D_MODEL, COL_GSSD // D_MODEL), rows(D_MODEL, COL_GATTN // D_MODEL),
                  const((1, D_INNER)), const((D_INNER, D_MODEL)), const((ATTN_GROUP_WIDTH, D_MODEL)),
                  const((D_MODEL, D_MODEL)),
                  const((2 * LANES, ATTN_GROUP_WIDTH))],
        out_specs=rows(D_MODEL),
        out_shape=jax.ShapeDtypeStruct((m, D_MODEL), F32),
        compiler_params=pltpu.CompilerParams(
            dimension_semantics=("arbitrary",), vmem_limit_bytes=VMEM_LIMIT),
        name="merge",
    )(x2, y_ssd, proj, outs[0], outs[1], outs[2], lses[0], lses[1], lses[2], proj, proj,
      ssd_norm_g, w_ssd, w_attn, w_out, head_expand)


def _ffn_kernel(x1_ref, g2_ref, wg_ref, wv_ref, cwg_ref, cwv_ref, cbg_ref, cbv_ref, wd_ref, out_ref,
                ug_scr, uv_scr, halo_g, halo_v, *, tiles_per_seq):
    i = pl.program_id(0)
    tm = x1_ref.shape[0]
    tf = wg_ref.shape[1]
    first = (i % tiles_per_seq) == 0

    @pl.when(i == 0)
    def _():
        halo_g[...] = jnp.zeros(halo_g.shape, F32)
        halo_v[...] = jnp.zeros(halo_v.shape, F32)

    def conv(u, u_scr, halo, w_ref, b_ref, c, scale):
        cs = slice(c * LANES, (c + 1) * LANES)
        u_scr[c, 0:HALO, :] = jnp.where(first, 0.0, halo[c])
        u_scr[c, HALO:HALO + tm, :] = u
        acc = scale * b_ref[:, cs]
        for t in range(FFN_CONV):
            tap = u_scr[c, pl.ds(HALO - (FFN_CONV - 1) + t, tm, stride=1), :]
            acc = acc + tap * (scale * w_ref[t:t + 1, cs])
        halo[c] = u_scr[c, tm:tm + HALO, :]
        return acc

    h = _rms_rows(x1_ref[...], g2_ref[...]).astype(BF16)
    per = FFN_CHUNK // LANES
    total = None
    for fc in range(tf // FFN_CHUNK):
        cols = slice(fc * FFN_CHUNK, (fc + 1) * FFN_CHUNK)
        ug = jnp.dot(h, wg_ref[:, cols], preferred_element_type=F32)
        uv = jnp.dot(h, wv_ref[:, cols], preferred_element_type=F32)
        acts = []
        for ci in range(per):
            c = fc * per + ci
            ls = slice(ci * LANES, (ci + 1) * LANES)
            gate = conv(ug[:, ls], ug_scr, halo_g, cwg_ref, cbg_ref, c, 0.5)
            val = conv(uv[:, ls], uv_scr, halo_v, cwv_ref, cbv_ref, c, 1.0)
            acts.append((_silu_of_half(gate) * val).astype(BF16))
        part = jnp.dot(jnp.concatenate(acts, axis=1), wd_ref[cols, :], preferred_element_type=F32)
        total = part if total is None else total + part
    out_ref[...] = x1_ref[...] + total


def _ffn(x1, g2, w_up, conv_w, conv_b, w_down, tm, seq):
    m = x1.shape[0]
    tf = D_FF
    once = dict(pipeline_mode=pl.Buffered(1))
    gate_half = lambda i: (0, 0)
    value_half = lambda i: (0, 1)
    return pl.pallas_call(
        functools.partial(_ffn_kernel, tiles_per_seq=seq // tm),
        grid=(m // tm,),
        in_specs=[
            pl.BlockSpec((tm, D_MODEL), lambda i: (i, 0)),
            pl.BlockSpec((1, D_MODEL), lambda i: (0, 0), **once),
            pl.BlockSpec((D_MODEL, tf), gate_half, **once),
            pl.BlockSpec((D_MODEL, tf), value_half, **once),
            pl.BlockSpec((FFN_CONV, tf), gate_half, **once),
            pl.BlockSpec((FFN_CONV, tf), value_half, **once),
            pl.BlockSpec((1, tf), gate_half, **once),
            pl.BlockSpec((1, tf), value_half, **once),
            pl.BlockSpec((tf, D_MODEL), lambda i: (0, 0), **once),
        ],
        out_specs=pl.BlockSpec((tm, D_MODEL), lambda i: (i, 0)),
        out_shape=jax.ShapeDtypeStruct((m, D_MODEL), F32),
        scratch_shapes=[
            pltpu.VMEM((tf // LANES, tm + HALO, LANES), F32),
            pltpu.VMEM((tf // LANES, tm + HALO, LANES), F32),
            pltpu.VMEM((tf // LANES, HALO, LANES), F32),
            pltpu.VMEM((tf // LANES, HALO, LANES), F32),
        ],
        compiler_params=pltpu.CompilerParams(
            dimension_semantics=("arbitrary",), vmem_limit_bytes=VMEM_LIMIT),
        name="ffn",
    )(x1, g2, w_up, w_up, conv_w, conv_w, conv_b, conv_b, w_down)


def _pad_lanes(v):
    return jnp.pad(v.astype(F32), (0, LANES - v.shape[0])).reshape(1, LANES)


def kernel(x, norm1_g, w_in, ssd_conv_w, ssd_conv_b, dt_bias, a_log, d_skip, ssd_norm_g, w_ssd_proj,
           q_norm_g, k_norm_g, w_attn_proj, w_out, norm2_g, w_up, ffn_conv_w, ffn_conv_b, w_down):
    batch, seq, _ = x.shape
    assert norm1_g.shape[0] == 1, "single-layer block"
    x2 = x.reshape(batch * seq, D_MODEL)
    g1 = norm1_g[0].reshape(1, D_MODEL)

    w = w_in[0]
    o_xbc = D_INNER
    o_dt = o_xbc + SSD_CONV_DIM
    o_q = o_dt + SSD_HEADS
    o_g = o_q + 3 * ATTN_WIDTH
    w_zx = w[:, :o_dt].astype(BF16)
    w_gate = w[:, o_g:].astype(BF16)
    w_dt = jnp.pad(w[:, o_dt:o_q], ((0, 0), (0, LANES - SSD_HEADS))).astype(BF16)
    w_qkv = w[:, o_q:o_g].astype(BF16)

    proj, dt_raw, h = _in_proj_main(x2, g1, w_zx, w_gate, w_dt, ssd_conv_w[0], ssd_conv_b[0].reshape(1, -1),
                                    tm=TM_IN_PROJ, tn=TN_IN_PROJ, seq=seq)

    heads_per_tile = MXU_WIDTH // ATTN_HEAD_DIM
    qk_gains = jnp.stack([jnp.tile(q_norm_g[0] * (ATTN_HEAD_DIM ** -0.5), heads_per_tile),
                          jnp.tile(k_norm_g[0], heads_per_tile)]).astype(F32)
    lane_head = jnp.arange(MXU_WIDTH) // ATTN_HEAD_DIM
    seg = (lane_head[:, None] == lane_head[None, :]).astype(BF16)
    qkv = _in_proj_qkv(h, w_qkv, qk_gains, seg, tm=TM_IN_PROJ)

    tri = jnp.tril(jnp.ones((SSD_CHUNK, SSD_CHUNK), BF16))
    head_of_col = jnp.arange(D_INNER) // SSD_HEAD_DIM
    expand = (jnp.arange(LANES)[:, None] == head_of_col[None, :]).astype(BF16)
    y_ssd = _ssd(proj, dt_raw, _pad_lanes(dt_bias[0]), _pad_lanes(a_log[0]),
                 jnp.repeat(d_skip[0], SSD_HEAD_DIM).reshape(1, D_INNER),
                 tri, jnp.concatenate([expand, expand], axis=0),
                 batch, seq, chunks_per_step=SSD_CHUNKS_PER_STEP)

    outs, lses = zip(*[_attn_group(qkv, gi, batch, seq) for gi in range(ATTN_GROUPS)])

    attn_head_of_col = jnp.arange(ATTN_GROUP_WIDTH) // ATTN_HEAD_DIM
    head_expand = (jnp.arange(2 * LANES)[:, None] % LANES == attn_head_of_col[None, :]).astype(BF16)
    x1 = _merge(x2, y_ssd, outs, lses, proj, ssd_norm_g[0].reshape(1, D_INNER),
                w_ssd_proj[0].astype(BF16), w_attn_proj[0].astype(BF16), w_out[0].astype(BF16),
                head_expand, tm=TM_MERGE)

    out = _ffn(x1, norm2_g[0].reshape(1, D_MODEL), w_up[0].astype(BF16), ffn_conv_w[0],
               ffn_conv_b[0].reshape(1, -1), w_down[0].astype(BF16), tm=TM_FFN, seq=seq)
    return out.reshape(batch, seq, D_MODEL)
```

```python
import functools

import jax
import jax.numpy as jnp
from jax import lax
from jax.experimental import pallas as pl
from jax.experimental.pallas import tpu as pltpu

F32 = jnp.float32
BF16 = jnp.bfloat16

D_MODEL = 1024
D_INNER = 2048
SSD_HEADS = 32
SSD_HEAD_DIM = 64
SSD_GROUPS = 8
SSD_HEADS_PER_GROUP = SSD_HEADS // SSD_GROUPS
D_STATE = 128
SSD_CONV = 4
SSD_CHUNK = 128
SSD_CONV_DIM = D_INNER + 2 * SSD_GROUPS * D_STATE
SSD_GROUP_WIDTH = D_INNER // SSD_GROUPS
ATTN_WINDOWS = (128, 512, 2048)
ATTN_DILATIONS = (1, 4, 16)
ATTN_GROUPS = 3
ATTN_HEADS = 8
ATTN_HEAD_DIM = 64
ATTN_BLOCK = 128
ATTN_GROUP_WIDTH = ATTN_HEADS * ATTN_HEAD_DIM
ATTN_WIDTH = ATTN_GROUPS * ATTN_GROUP_WIDTH
D_FF = 2816
FFN_CONV = 3
FFN_CHUNK = D_FF // 2
EPS = 1e-6
LANES = 128
MXU_WIDTH = 256
HALO = 8
ATTN_SLAB = ATTN_BLOCK * max(ATTN_DILATIONS)
ATTN_SLAB_BLOCKS = ATTN_SLAB // ATTN_BLOCK
ATTN_LANE_CHUNKS = ATTN_GROUP_WIDTH // LANES
ATTN_UNROLL = 16
DEINTERLEAVE_FACTOR = 4

COL_XBC = 0
COL_Z = COL_XBC + SSD_CONV_DIM
COL_GSSD = COL_Z + D_INNER
COL_GATTN = COL_GSSD + D_MODEL
PROJ_WIDTH = COL_GATTN + D_MODEL

V7X_VMEM_BYTES = 64 * 1024 * 1024
VMEM_LIMIT = V7X_VMEM_BYTES * 7 // 8

TM_IN_PROJ = 1024
TN_IN_PROJ = 2048
TM_MERGE = 512
TM_FFN = 512
SSD_CHUNKS_PER_STEP = 4


def _sigmoid(v):
    return 0.5 * jnp.tanh(0.5 * v) + 0.5


def _silu_of_half(hv):
    return hv + hv * jnp.tanh(hv)


def _split_bf16(v, terms):
    parts = []
    for _ in range(terms):
        p = v.astype(BF16)
        parts.append(p)
        v = v - p.astype(F32)
    return parts


def _dot_split(parts, rhs, lhs_side=True):
    acc = None
    for p in parts:
        t = (jnp.dot(p, rhs, preferred_element_type=F32) if lhs_side
             else jnp.dot(rhs, p, preferred_element_type=F32))
        acc = t if acc is None else acc + t
    return acc


def _rms_rows(x, gain):
    ms = jnp.mean(x * x, axis=-1, keepdims=True)
    return x * lax.rsqrt(ms + EPS) * gain


def _in_proj_main_kernel(x_ref, g_ref, w_ref, wgate_ref, wdt_ref, cw_ref, cb_ref, proj_ref, dt_ref, h_ref,
                         xp_scr, halo_scr, *, tiles_per_seq):
    i = pl.program_id(0)
    j = pl.program_id(1)
    tm, tn = proj_ref.shape
    conv_tiles = SSD_CONV_DIM // tn
    z_tiles = D_INNER // tn

    @pl.when(j == 0)
    def _():
        h = _rms_rows(x_ref[...], g_ref[...]).astype(BF16)
        h_ref[...] = h
        dt_ref[...] = jnp.dot(h, wdt_ref[...], preferred_element_type=F32)

    @pl.when(jnp.logical_and(i == 0, j == 0))
    def _():
        halo_scr[...] = jnp.zeros(halo_scr.shape, F32)

    @pl.when(j < conv_tiles)
    def _():
        r = jnp.dot(h_ref[...], w_ref[...], preferred_element_type=F32)
        first = (i % tiles_per_seq) == 0
        for c in range(tn // LANES):
            cs = slice(c * LANES, (c + 1) * LANES)
            xp_scr[c, 0:HALO, :] = jnp.where(first, 0.0, halo_scr[j, c])
            xp_scr[c, HALO:HALO + tm, :] = r[:, cs]
            acc = 0.5 * cb_ref[:, cs]
            for t in range(SSD_CONV):
                tap = xp_scr[c, pl.ds(HALO - (SSD_CONV - 1) + t, tm, stride=1), :]
                acc = acc + tap * (0.5 * cw_ref[t:t + 1, cs])
            proj_ref[:, cs] = _silu_of_half(acc).astype(proj_ref.dtype)
            halo_scr[j, c] = xp_scr[c, tm:tm + HALO, :]

    @pl.when(jnp.logical_and(j >= conv_tiles, j < conv_tiles + z_tiles))
    def _():
        r = jnp.dot(h_ref[...], w_ref[...], preferred_element_type=F32)
        proj_ref[...] = _silu_of_half(0.5 * r).astype(proj_ref.dtype)

    @pl.when(j >= conv_tiles + z_tiles)
    def _():
        proj_ref[...] = jnp.dot(h_ref[...], wgate_ref[...], preferred_element_type=F32).astype(proj_ref.dtype)


def _in_proj_main(x2, g1, w_zx, w_gate, w_dt, conv_w, conv_b, tm, tn, seq):
    m = x2.shape[0]
    assert SSD_CONV_DIM % tn == 0 and D_INNER % tn == 0 and seq % tm == 0 and w_gate.shape[1] == tn
    conv_tiles = SSD_CONV_DIM // tn
    z_tiles = D_INNER // tn
    conv_col = lambda i, j: (0, jnp.minimum(j, conv_tiles - 1))
    zx_col = lambda i, j: (0, jnp.where(j < conv_tiles, j + z_tiles, jnp.minimum(j - conv_tiles, z_tiles - 1)))
    return pl.pallas_call(
        functools.partial(_in_proj_main_kernel, tiles_per_seq=seq // tm),
        grid=(m // tm, PROJ_WIDTH // tn),
        in_specs=[
            pl.BlockSpec((tm, D_MODEL), lambda i, j: (i, 0)),
            pl.BlockSpec((1, D_MODEL), lambda i, j: (0, 0)),
            pl.BlockSpec((D_MODEL, tn), zx_col),
            pl.BlockSpec((D_MODEL, tn), lambda i, j: (0, 0), pipeline_mode=pl.Buffered(1)),
            pl.BlockSpec((D_MODEL, LANES), lambda i, j: (0, 0)),
            pl.BlockSpec((SSD_CONV, tn), conv_col),
            pl.BlockSpec((1, tn), conv_col),
        ],
        out_specs=[
            pl.BlockSpec((tm, tn), lambda i, j: (i, j)),
            pl.BlockSpec((tm, LANES), lambda i, j: (i, 0)),
            pl.BlockSpec((tm, D_MODEL), lambda i, j: (i, 0)),
        ],
        out_shape=[
            jax.ShapeDtypeStruct((m, PROJ_WIDTH), BF16),
            jax.ShapeDtypeStruct((m, LANES), F32),
            jax.ShapeDtypeStruct((m, D_MODEL), BF16),
        ],
        scratch_shapes=[
            pltpu.VMEM((tn // LANES, HALO + tm, LANES), F32),
            pltpu.VMEM((conv_tiles, tn // LANES, HALO, LANES), F32),
        ],
        compiler_params=pltpu.CompilerParams(
            dimension_semantics=("arbitrary", "arbitrary"), vmem_limit_bytes=VMEM_LIMIT),
        name="in_proj_main",
    )(x2, g1, w_zx, w_gate, w_dt, conv_w, conv_b)


def _in_proj_qkv_kernel(h_ref, w_ref, gain_ref, seg_ref, out_ref, t_scr):
    j = pl.program_id(1)
    seg_w = seg_ref.shape[0]

    @pl.when(j < 2)
    def _():
        t_scr[...] = jnp.dot(h_ref[...], w_ref[...], preferred_element_type=F32)
        gain = gain_ref[pl.ds(j, 1), :]
        tm = t_scr.shape[0]
        n_seg = ATTN_WIDTH // seg_w
        sq = [(t_scr[:, c * seg_w:(c + 1) * seg_w] ** 2).astype(BF16) for c in range(n_seg)]
        ss = jnp.dot(jnp.concatenate(sq, axis=0), seg_ref[...], preferred_element_type=F32)
        for c in range(n_seg):
            cs = slice(c * seg_w, (c + 1) * seg_w)
            inv_rms = lax.rsqrt(ss[c * tm:(c + 1) * tm] * (1.0 / ATTN_HEAD_DIM) + EPS)
            out_ref[:, cs] = (t_scr[:, cs] * inv_rms * gain).astype(out_ref.dtype)

    @pl.when(j == 2)
    def _():
        out_ref[...] = jnp.dot(h_ref[...], w_ref[...], preferred_element_type=F32).astype(out_ref.dtype)


def _in_proj_qkv(h, w_qkv, gains, seg, tm):
    m = h.shape[0]
    return pl.pallas_call(
        _in_proj_qkv_kernel,
        grid=(m // tm, 3),
        in_specs=[
            pl.BlockSpec((tm, D_MODEL), lambda i, j: (i, 0)),
            pl.BlockSpec((D_MODEL, ATTN_WIDTH), lambda i, j: (0, j)),
            pl.BlockSpec(gains.shape, lambda i, j: (0, 0)),
            pl.BlockSpec(seg.shape, lambda i, j: (0, 0)),
        ],
        out_specs=pl.BlockSpec((tm, ATTN_WIDTH), lambda i, j: (i, j)),
        out_shape=jax.ShapeDtypeStruct((m, 3 * ATTN_WIDTH), BF16),
        scratch_shapes=[pltpu.VMEM((tm, ATTN_WIDTH), F32)],
        compiler_params=pltpu.CompilerParams(
            dimension_semantics=("arbitrary", "arbitrary"), vmem_limit_bytes=VMEM_LIMIT),
        name="in_proj_qkv",
    )(h, w_qkv, gains, seg)


def _ssd_decay(dt_raw, dtb, a_log, tri):
    q = SSD_CHUNK
    dt = jax.nn.softplus(dt_raw + dtb)
    a = dt * (-jnp.exp(a_log))
    a_cs = _dot_split(_split_bf16(a, 3), tri, lhs_side=False)
    a_cs_t_adj = a_cs.T - jnp.log(dt.T)
    e_in = jnp.exp(a_cs)
    w_out = jnp.exp(a_cs[q - 1:q, :] - a_cs) * dt
    spread = jnp.concatenate([jnp.concatenate(_split_bf16(e_in, 2), axis=1),
                              jnp.concatenate(_split_bf16(w_out, 2), axis=1)], axis=0)
    return a_cs, a_cs_t_adj, spread


def _ssd_chunk(act, a_cs, a_cs_t_adj, e_exp, w_exp, dskip, y_ref, rows, state_scr):
    q = SSD_CHUNK
    li = lax.broadcasted_iota(jnp.int32, (q, q), 0)
    si = lax.broadcasted_iota(jnp.int32, (q, q), 1)
    causal = li >= si
    head_of_lane = lax.broadcasted_iota(jnp.int32, (q, SSD_GROUP_WIDTH), 1) // SSD_HEAD_DIM

    for g in range(SSD_GROUPS):
        gs = slice(g * SSD_GROUP_WIDTH, (g + 1) * SSD_GROUP_WIDTH)
        xs_b = act[:, gs]
        xs = xs_b.astype(F32)
        b_g = act[:, D_INNER + g * D_STATE:D_INNER + (g + 1) * D_STATE]
        c_g = act[:, D_INNER + (SSD_GROUPS + g) * D_STATE:D_INNER + (SSD_GROUPS + g + 1) * D_STATE]
        cb = lax.dot_general(c_g, b_g, (((1,), (1,)), ((), ())), preferred_element_type=F32)
        st = state_scr[g]
        y_off = jnp.dot(c_g, st.astype(BF16), preferred_element_type=F32) * e_exp[:, gs]
        mms, blocks = [], []
        for k in range(SSD_HEADS_PER_GROUP):
            h = g * SSD_HEADS_PER_GROUP + k
            seg = a_cs[:, h:h + 1] - a_cs_t_adj[h:h + 1, :]
            mms.append((cb * jnp.exp(jnp.where(causal, seg, -jnp.inf))).astype(BF16))
            blocks.append(jnp.where(head_of_lane == k, xs_b, jnp.zeros_like(xs_b)))
        y_diag = jnp.dot(jnp.concatenate(mms, axis=1), jnp.concatenate(blocks, axis=0),
                         preferred_element_type=F32)
        y = y_diag + y_off + dskip[:, gs] * xs
        y_ref[rows, gs] = y.astype(y_ref.dtype)
        xw = (xs * w_exp[:, gs]).astype(BF16)
        upd = lax.dot_general(b_g, xw, (((0,), (0,)), ((), ())), preferred_element_type=F32)
        state_scr[g] = st * e_exp[q - 1:q, gs] + upd


def _ssd_kernel(act_ref, dt_ref, dtb_ref, alog_ref, dskip_ref, tri_ref, expand2_ref, y_ref, state_scr):
    @pl.when(pl.program_id(1) == 0)
    def _():
        state_scr[...] = jnp.zeros(state_scr.shape, F32)

    q = SSD_CHUNK
    chunks = [slice(ci * q, (ci + 1) * q) for ci in range(act_ref.shape[0] // q)]
    decay = [_ssd_decay(dt_ref[rows, :], dtb_ref[...], alog_ref[...], tri_ref[...]) for rows in chunks]
    spread = jnp.dot(jnp.concatenate([d[2] for d in decay], axis=0), expand2_ref[...],
                     preferred_element_type=F32)
    for ci, rows in enumerate(chunks):
        a_cs, a_cs_t_adj, _ = decay[ci]
        e_exp = spread[2 * ci * q:(2 * ci + 1) * q]
        w_exp = spread[(2 * ci + 1) * q:(2 * ci + 2) * q]
        _ssd_chunk(act_ref[rows, :], a_cs, a_cs_t_adj, e_exp, w_exp, dskip_ref[...], y_ref, rows, state_scr)


def _ssd(proj, dt_raw, dt_bias, a_log, d_skip_exp, tri, expand2, batch, seq, chunks_per_step):
    rows = SSD_CHUNK * chunks_per_step
    steps = seq // rows
    row = lambda b, c: b * steps + c
    const = lambda b, c: (0, 0)
    return pl.pallas_call(
        _ssd_kernel,
        grid=(batch, steps),
        in_specs=[
            pl.BlockSpec((rows, SSD_CONV_DIM), lambda b, c: (row(b, c), COL_XBC // SSD_CONV_DIM)),
            pl.BlockSpec((rows, LANES), lambda b, c: (row(b, c), 0)),
            pl.BlockSpec((1, LANES), const),
            pl.BlockSpec((1, LANES), const),
            pl.BlockSpec((1, D_INNER), const),
            pl.BlockSpec((SSD_CHUNK, SSD_CHUNK), const),
            pl.BlockSpec((2 * LANES, D_INNER), const),
        ],
        out_specs=pl.BlockSpec((rows, D_INNER), lambda b, c: (row(b, c), 0)),
        out_shape=jax.ShapeDtypeStruct((batch * seq, D_INNER), BF16),
        scratch_shapes=[pltpu.VMEM((SSD_GROUPS, D_STATE, SSD_GROUP_WIDTH), F32)],
        compiler_params=pltpu.CompilerParams(
            dimension_semantics=("arbitrary", "arbitrary"), vmem_limit_bytes=VMEM_LIMIT),
        name="ssd",
    )(proj, dt_raw, dt_bias, a_log, d_skip_exp, tri, expand2)


def _attn_kernel(q_ref, k_ref, v_ref, o_ref, lse_ref, stage, q_de, k_de, v_de, *maybe_mid, dil, n_back):
    blk = ATTN_BLOCK
    per_res = ATTN_SLAB_BLOCKS // dil
    run = per_res + 1
    first_slab = pl.program_id(1) == 0

    def keep_previous(dst):
        @pl.when(first_slab)
        def _():
            def body(res, carry):
                dst[res * run] = jnp.zeros(dst.shape[1:], BF16)
                return carry
            lax.fori_loop(0, dil, body, 0)

        @pl.when(jnp.logical_not(first_slab))
        def _():
            def body(res, carry):
                dst[res * run] = dst[res * run + per_res]
                return carry
            lax.fori_loop(0, dil, body, 0)

    def deinterleave(src_ref, dst, skip):
        if dil == 1:
            def copy(bi, carry):
                rows = pl.ds(pl.multiple_of(bi * blk, blk), blk)
                for c in range(ATTN_LANE_CHUNKS):
                    dst[bi + skip, c] = src_ref[rows, c * LANES:(c + 1) * LANES]
                return carry

            lax.fori_loop(0, ATTN_SLAB_BLOCKS, copy, 0)
            return

        for c in range(ATTN_LANE_CHUNKS):
            stage[c] = src_ref[:, c * LANES:(c + 1) * LANES].astype(F32)

        if maybe_mid:
            mid, = maybe_mid
            f = DEINTERLEAVE_FACTOR
            for c in range(ATTN_LANE_CHUNKS):
                for r in range(f):
                    mid[c * f + r] = stage[c, pl.ds(r, ATTN_SLAB // f, stride=f), :]

            def body(bi, carry):
                for c in range(ATTN_LANE_CHUNKS):
                    dst[bi + skip * (bi + 1), c] = (
                        mid[c * f + bi % f, pl.ds(bi // f, blk, stride=f), :].astype(BF16))
                return carry
        else:
            def body(bi, carry):
                res = bi // per_res
                start = res + dil * blk * (bi % per_res)
                for c in range(ATTN_LANE_CHUNKS):
                    dst[bi + skip * (res + 1), c] = stage[c, pl.ds(start, blk, stride=dil), :].astype(BF16)
                return carry

        lax.fori_loop(0, ATTN_SLAB_BLOCKS, body, 0)

    keep_previous(k_de)
    keep_previous(v_de)
    deinterleave(k_ref, k_de, 1)
    deinterleave(v_ref, v_de, 1)
    deinterleave(q_ref, q_de, 0)

    lane = lax.broadcasted_iota(jnp.int32, (blk, LANES), 1)
    even = lane < ATTN_HEAD_DIM
    qi = lax.broadcasted_iota(jnp.int32, (blk, 2 * blk), 0)
    ki = lax.broadcasted_iota(jnp.int32, (blk, 2 * blk), 1)
    dist = blk + qi - ki
    band = (dist >= 0) & (dist <= n_back)

    def attend(bi, carry):
        res = bi // per_res
        j = bi % per_res
        key_lo = jnp.where(jnp.logical_and(first_slab, j == 0), blk, 0)
        mask = band & (ki >= key_lo)
        mask2 = jnp.concatenate([mask, mask], axis=0)
        kv = res * run + j
        start = res + dil * blk * j
        lse_tile = jnp.zeros((blk, LANES), F32)
        for hp in range(ATTN_LANE_CHUNKS):
            q_pair = q_de[bi, hp]
            k2 = jnp.concatenate([k_de[kv, hp], k_de[kv + 1, hp]], axis=0)
            v2 = jnp.concatenate([v_de[kv, hp], v_de[kv + 1, hp]], axis=0)
            v2 = jnp.concatenate([v2, jnp.ones_like(v2)], axis=1)
            zero = jnp.zeros_like(q_pair)
            qm = jnp.concatenate([jnp.where(even, q_pair, zero), jnp.where(even, zero, q_pair)], axis=0)
            s = lax.dot_general(qm, k2, (((1,), (1,)), ((), ())), preferred_element_type=F32)
            s = jnp.where(mask2, s, -jnp.inf)
            m = jnp.max(s, axis=-1, keepdims=True)
            p = jnp.exp(s - m)
            pv = jnp.dot(p.astype(BF16), v2, preferred_element_type=F32)
            den = pv[:, LANES:]
            out = pv[:, :LANES] * (1.0 / den)
            lse = m + jnp.log(den)
            halves = [out[:blk], out[blk:]]
            for e in range(2):
                lse_tile = jnp.where(lane == 2 * hp + e, lse[e * blk:(e + 1) * blk], lse_tile)
            o_pair = jnp.where(even, halves[0], halves[1])
            if dil == 1:
                rows = pl.ds(pl.multiple_of(start, blk), blk)
                o_ref[rows, hp * LANES:(hp + 1) * LANES] = o_pair.astype(o_ref.dtype)
            elif maybe_mid:
                f = DEINTERLEAVE_FACTOR
                maybe_mid[0][hp * f + bi % f, pl.ds(bi // f, blk, stride=f), :] = o_pair
            else:
                stage[hp, pl.ds(start, blk, stride=dil), :] = o_pair
        if maybe_mid:
            f = DEINTERLEAVE_FACTOR
            maybe_mid[0][ATTN_LANE_CHUNKS * f + bi % f, pl.ds(bi // f, blk, stride=f), :] = lse_tile
        else:
            lse_ref[pl.ds(start, blk, stride=dil), :] = lse_tile
        return carry

    lax.fori_loop(0, ATTN_SLAB_BLOCKS, attend, 0, unroll=ATTN_UNROLL)
    if maybe_mid:
        f = DEINTERLEAVE_FACTOR
        for r in range(f):
            lse_ref[pl.ds(r, ATTN_SLAB // f, stride=f), :] = maybe_mid[0][ATTN_LANE_CHUNKS * f + r]
        for c in range(ATTN_LANE_CHUNKS):
            for r in range(f):
                stage[c, pl.ds(r, ATTN_SLAB // f, stride=f), :] = maybe_mid[0][c * f + r]
    if dil > 1:
        for c in range(ATTN_LANE_CHUNKS):
            o_ref[:, c * LANES:(c + 1) * LANES] = stage[c].astype(o_ref.dtype)


def _attn_group(qkv, gi, batch, seq):
    dil = ATTN_DILATIONS[gi]
    n_back = ATTN_WINDOWS[gi] // dil
    assert n_back <= ATTN_BLOCK and ATTN_SLAB_BLOCKS % dil == 0 and seq % ATTN_SLAB == 0
    slabs = seq // ATTN_SLAB

    def cur(which):
        return pl.BlockSpec((ATTN_SLAB, ATTN_GROUP_WIDTH), lambda b, s: (b * slabs + s, which * ATTN_GROUPS + gi))

    run = ATTN_SLAB_BLOCKS // dil + 1
    de_block = (ATTN_LANE_CHUNKS, ATTN_BLOCK, LANES)
    f = DEINTERLEAVE_FACTOR
    two_pass = dil == f * f and dil == ATTN_SLAB_BLOCKS
    two_pass_scratch = ([pltpu.VMEM(((ATTN_LANE_CHUNKS + 1) * f, ATTN_SLAB // f, LANES), F32)]
                        if two_pass else [])
    return pl.pallas_call(
        functools.partial(_attn_kernel, dil=dil, n_back=n_back),
        grid=(batch, slabs),
        in_specs=[cur(0), cur(1), cur(2)],
        out_specs=[pl.BlockSpec((ATTN_SLAB, ATTN_GROUP_WIDTH), lambda b, s: (b * slabs + s, 0)),
                   pl.BlockSpec((ATTN_SLAB, LANES), lambda b, s: (b * slabs + s, 0))],
        out_shape=[jax.ShapeDtypeStruct((batch * seq, ATTN_GROUP_WIDTH), BF16),
                   jax.ShapeDtypeStruct((batch * seq, LANES), F32)],
        scratch_shapes=[
            pltpu.VMEM((ATTN_LANE_CHUNKS, ATTN_SLAB, LANES), F32),
            pltpu.VMEM((ATTN_SLAB_BLOCKS,) + de_block, BF16),
            pltpu.VMEM((dil * run,) + de_block, BF16),
            pltpu.VMEM((dil * run,) + de_block, BF16),
        ] + two_pass_scratch,
        compiler_params=pltpu.CompilerParams(
            dimension_semantics=("arbitrary", "arbitrary"), vmem_limit_bytes=VMEM_LIMIT),
        name=f"attn_g{gi}",
    )(qkv, qkv, qkv)


def _expand_heads(w, hexp):
    return jnp.dot(jnp.concatenate(_split_bf16(w, 2), axis=1), hexp, preferred_element_type=F32)


def _merge_kernel(x_ref, yssd_ref, gz_ref, o0_ref, o1_ref, o2_ref, l0_ref, l1_ref, l2_ref, gssd_ref, gattn_ref,
                  ng_ref, wssd_ref, wattn_ref, wout_ref, hexp_ref, x1_ref):
    l0, l1, l2 = l0_ref[...], l1_ref[...], l2_ref[...]
    mx = jnp.maximum(jnp.maximum(l0, l1), l2)
    e0, e1, e2 = jnp.exp(l0 - mx), jnp.exp(l1 - mx), jnp.exp(l2 - mx)
    inv = 1.0 / (e0 + e1 + e2)
    rows = o0_ref.shape[0]
    wexp = _expand_heads(jnp.concatenate([e0 * inv, e1 * inv, e2 * inv], axis=0), hexp_ref[...])
    y_attn = jnp.zeros(o0_ref.shape, F32)
    for gi, o_ref in enumerate((o0_ref, o1_ref, o2_ref)):
        y_attn = y_attn + wexp[gi * rows:(gi + 1) * rows] * o_ref[...].astype(F32)
    p_ssd = None
    for g in range(SSD_GROUPS):
        gs = slice(g * SSD_GROUP_WIDTH, (g + 1) * SSD_GROUP_WIDTH)
        y = yssd_ref[:, gs].astype(F32) * gz_ref[:, gs].astype(F32)
        yn = _rms_rows(y, ng_ref[:, gs]).astype(BF16)
        part = jnp.dot(yn, wssd_ref[gs, :], preferred_element_type=F32)
        p_ssd = part if p_ssd is None else p_ssd + part
    p_attn = jnp.dot(y_attn.astype(BF16), wattn_ref[...], preferred_element_type=F32)
    merged = (_sigmoid(gssd_ref[...].astype(F32)) * p_ssd
              + _sigmoid(gattn_ref[...].astype(F32)) * p_attn)
    x1 = x_ref[...] + jnp.dot(merged.astype(BF16), wout_ref[...], preferred_element_type=F32)
    x1_ref[...] = x1


def _merge(x2, y_ssd, outs, lses, proj, ssd_norm_g, w_ssd, w_attn, w_out, head_expand, tm):
    m = x2.shape[0]
    rows = lambda width, cb=0: pl.BlockSpec((tm, width), lambda i: (i, cb))
    const = lambda shape: pl.BlockSpec(shape, lambda i: (0, 0))
    return pl.pallas_call(
        _merge_kernel,
        grid=(m // tm,),
        in_specs=[rows(D_MODEL), rows(D_INNER), rows(D_INNER, COL_Z // D_INNER),
                  rows(ATTN_GROUP_WIDTH), rows(ATTN_GROUP_WIDTH), rows(ATTN_GROUP_WIDTH),
                  rows(LANES), rows(LANES), rows(LANES),
                  rows(D_MODEL, COL_GSSD // D_MODEL), rows(D_MODEL, COL_GATTN // D_MODEL),
                  const((1, D_INNER)), const((D_INNER, D_MODEL)), const((ATTN_GROUP_WIDTH, D_MODEL)),
                  const((D_MODEL, D_MODEL)),
                  const((2 * LANES, ATTN_GROUP_WIDTH))],
        out_specs=rows(D_MODEL),
        out_shape=jax.ShapeDtypeStruct((m, D_MODEL), F32),
        compiler_params=pltpu.CompilerParams(
            dimension_semantics=("arbitrary",), vmem_limit_bytes=VMEM_LIMIT),
        name="merge",
    )(x2, y_ssd, proj, outs[0], outs[1], outs[2], lses[0], lses[1], lses[2], proj, proj,
      ssd_norm_g, w_ssd, w_attn, w_out, head_expand)


def _ffn_kernel(x1_ref, g2_ref, wg_ref, wv_ref, cwg_ref, cwv_ref, cbg_ref, cbv_ref, wd_ref, out_ref,
                ug_scr, uv_scr, halo_g, halo_v, *, tiles_per_seq):
    i = pl.program_id(0)
    tm = x1_ref.shape[0]
    tf = wg_ref.shape[1]
    first = (i % tiles_per_seq) == 0

    @pl.when(i == 0)
    def _():
        halo_g[...] = jnp.zeros(halo_g.shape, F32)
        halo_v[...] = jnp.zeros(halo_v.shape, F32)

    def conv(u, u_scr, halo, w_ref, b_ref, c, scale):
        cs = slice(c * LANES, (c + 1) * LANES)
        u_scr[c, 0:HALO, :] = jnp.where(first, 0.0, halo[c])
        u_scr[c, HALO:HALO + tm, :] = u
        acc = scale * b_ref[:, cs]
        for t in range(FFN_CONV):
            tap = u_scr[c, pl.ds(HALO - (FFN_CONV - 1) + t, tm, stride=1), :]
            acc = acc + tap * (scale * w_ref[t:t + 1, cs])
        halo[c] = u_scr[c, tm:tm + HALO, :]
        return acc

    h = _rms_rows(x1_ref[...], g2_ref[...]).astype(BF16)
    per = FFN_CHUNK // LANES
    total = None
    for fc in range(tf // FFN_CHUNK):
        cols = slice(fc * FFN_CHUNK, (fc + 1) * FFN_CHUNK)
        ug = jnp.dot(h, wg_ref[:, cols], preferred_element_type=F32)
        uv = jnp.dot(h, wv_ref[:, cols], preferred_element_type=F32)
        acts = []
        for ci in range(per):
            c = fc * per + ci
            ls = slice(ci * LANES, (ci + 1) * LANES)
            gate = conv(ug[:, ls], ug_scr, halo_g, cwg_ref, cbg_ref, c, 0.5)
            val = conv(uv[:, ls], uv_scr, halo_v, cwv_ref, cbv_ref, c, 1.0)
            acts.append((_silu_of_half(gate) * val).astype(BF16))
        part = jnp.dot(jnp.concatenate(acts, axis=1), wd_ref[cols, :], preferred_element_type=F32)
        total = part if total is None else total + part
    out_ref[...] = x1_ref[...] + total


def _ffn(x1, g2, w_up, conv_w, conv_b, w_down, tm, seq):
    m = x1.shape[0]
    tf = D_FF
    once = dict(pipeline_mode=pl.Buffered(1))
    gate_half = lambda i: (0, 0)
    value_half = lambda i: (0, 1)
    return pl.pallas_call(
        functools.partial(_ffn_kernel, tiles_per_seq=seq // tm),
        grid=(m // tm,),
        in_specs=[
            pl.BlockSpec((tm, D_MODEL), lambda i: (i, 0)),
            pl.BlockSpec((1, D_MODEL), lambda i: (0, 0), **once),
            pl.BlockSpec((D_MODEL, tf), gate_half, **once),
            pl.BlockSpec((D_MODEL, tf), value_half, **once),
            pl.BlockSpec((FFN_CONV, tf), gate_half, **once),
            pl.BlockSpec((FFN_CONV, tf), value_half, **once),
            pl.BlockSpec((1, tf), gate_half, **once),
            pl.BlockSpec((1, tf), value_half, **once),
            pl.BlockSpec((tf, D_MODEL), lambda i: (0, 0), **once),
        ],
        out_specs=pl.BlockSpec((tm, D_MODEL), lambda i: (i, 0)),
        out_shape=jax.ShapeDtypeStruct((m, D_MODEL), F32),
        scratch_shapes=[
            pltpu.VMEM((tf // LANES, tm + HALO, LANES), F32),
            pltpu.VMEM((tf // LANES, tm + HALO, LANES), F32),
            pltpu.VMEM((tf // LANES, HALO, LANES), F32),
            pltpu.VMEM((tf // LANES, HALO, LANES), F32),
        ],
        compiler_params=pltpu.CompilerParams(
            dimension_semantics=("arbitrary",), vmem_limit_bytes=VMEM_LIMIT),
        name="ffn",
    )(x1, g2, w_up, w_up, conv_w, conv_w, conv_b, conv_b, w_down)


def _pad_lanes(v):
    return jnp.pad(v.astype(F32), (0, LANES - v.shape[0])).reshape(1, LANES)


def kernel(x, norm1_g, w_in, ssd_conv_w, ssd_conv_b, dt_bias, a_log, d_skip, ssd_norm_g, w_ssd_proj,
           q_norm_g, k_norm_g, w_attn_proj, w_out, norm2_g, w_up, ffn_conv_w, ffn_conv_b, w_down):
    batch, seq, _ = x.shape
    assert norm1_g.shape[0] == 1, "single-layer block"
    x2 = x.reshape(batch * seq, D_MODEL)
    g1 = norm1_g[0].reshape(1, D_MODEL)

    w = w_in[0]
    o_xbc = D_INNER
    o_dt = o_xbc + SSD_CONV_DIM
    o_q = o_dt + SSD_HEADS
    o_g = o_q + 3 * ATTN_WIDTH
    w_zx = w[:, :o_dt].astype(BF16)
    w_gate = w[:, o_g:].astype(BF16)
    w_dt = jnp.pad(w[:, o_dt:o_q], ((0, 0), (0, LANES - SSD_HEADS))).astype(BF16)
    w_qkv = w[:, o_q:o_g].astype(BF16)

    proj, dt_raw, h = _in_proj_main(x2, g1, w_zx, w_gate, w_dt, ssd_conv_w[0], ssd_conv_b[0].reshape(1, -1),
                                    tm=TM_IN_PROJ, tn=TN_IN_PROJ, seq=seq)

    heads_per_tile = MXU_WIDTH // ATTN_HEAD_DIM
    qk_gains = jnp.stack([jnp.tile(q_norm_g[0] * (ATTN_HEAD_DIM ** -0.5), heads_per_tile),
                          jnp.tile(k_norm_g[0], heads_per_tile)]).astype(F32)
    lane_head = jnp.arange(MXU_WIDTH) // ATTN_HEAD_DIM
    seg = (lane_head[:, None] == lane_head[None, :]).astype(BF16)
    qkv = _in_proj_qkv(h, w_qkv, qk_gains, seg, tm=TM_IN_PROJ)

    tri = jnp.tril(jnp.ones((SSD_CHUNK, SSD_CHUNK), BF16))
    head_of_col = jnp.arange(D_INNER) // SSD_HEAD_DIM
    expand = (jnp.arange(LANES)[:, None] == head_of_col[None, :]).astype(BF16)
    y_ssd = _ssd(proj, dt_raw, _pad_lanes(dt_bias[0]), _pad_lanes(a_log[0]),
                 jnp.repeat(d_skip[0], SSD_HEAD_DIM).reshape(1, D_INNER),
                 tri, jnp.concatenate([expand, expand], axis=0),
                 batch, seq, chunks_per_step=SSD_CHUNKS_PER_STEP)

    outs, lses = zip(*[_attn_group(qkv, gi, batch, seq) for gi in range(ATTN_GROUPS)])

    attn_head_of_col = jnp.arange(ATTN_GROUP_WIDTH) // ATTN_HEAD_DIM
    head_expand = (jnp.arange(2 * LANES)[:, None] % LANES == attn_head_of_col[None, :]).astype(BF16)
    x1 = _merge(x2, y_ssd, outs, lses, proj, ssd_norm_g[0].reshape(1, D_INNER),
                w_ssd_proj[0].astype(BF16), w_attn_proj[0].astype(BF16), w_out[0].astype(BF16),
                head_expand, tm=TM_MERGE)

    out = _ffn(x1, norm2_g[0].reshape(1, D_MODEL), w_up[0].astype(BF16), ffn_conv_w[0],
               ffn_conv_b[0].reshape(1, -1), w_down[0].astype(BF16), tm=TM_FFN, seq=seq)
    return out.reshape(batch, seq, D_MODEL)
```
